```python
import math
import jax
import jax.numpy as jnp
from jax import lax
import numpy as np

D_MODEL = 2048
BATCH = 2
SEQ = 8192
DEPTH = 1

D_MIX = D_MODEL
NSA_HEADS = 16
NSA_KV = 4
NSA_REP = NSA_HEADS // NSA_KV
HEAD_DIM = 64
D_NSA = NSA_HEADS * HEAD_DIM
D_KV = NSA_KV * HEAD_DIM
CMP_STRIDE = 16
CMP_BLOCK = 2 * CMP_STRIDE
CMP_HIDDEN = 2 * HEAD_DIM
SLC_BLOCK = 64
SLC_TOPN = 16
WINDOW = 512
Q_BLOCK = 128
N_BRANCH = 3
SG_HEADS = 8
D_SG = D_MIX - D_NSA
SG_DIM = D_SG // SG_HEADS
SG_CHUNK = 128
REL_BUCKETS = 32
REL_MAX_DIST = 1024
PEER_HEADS = 8
PEER_NKEYS = 128
PEER_EXPERTS = PEER_NKEYS * PEER_NKEYS
PEER_QDIM = 256
PEER_HALF = PEER_QDIM // 2
PEER_TOPK = 16
PEER_TOKEN_BLOCK = 128
D_IN = D_NSA + 6 * D_KV + N_BRANCH * NSA_HEADS + 2 * D_SG
EPS = 1e-6
NEG = -1e30
BIG = 1e30

kernel_name = 'hymba_nsa_gmlp_peer_layer'


def gelu(x):
    return jax.nn.gelu(x, approximate=False)


def rmsnorm(x, g):
    xf = x.astype(jnp.float32)
    y = xf * lax.rsqrt(jnp.mean(xf * xf, axis=-1, keepdims=True) + EPS)
    return (y * g.astype(jnp.float32)).astype(x.dtype)


def layernorm(x, g, b):
    xf = x.astype(jnp.float32)
    mu = jnp.mean(xf, axis=-1, keepdims=True)
    var = jnp.mean(jnp.square(xf - mu), axis=-1, keepdims=True)
    y = (xf - mu) * lax.rsqrt(var + EPS)
    return (y * g.astype(jnp.float32) + b.astype(jnp.float32)).astype(x.dtype)


def rel_bucket(dist):
    n = jnp.maximum(dist, 0)
    max_exact = REL_BUCKETS // 2
    nf = jnp.maximum(n, 1).astype(jnp.float32)
    large = max_exact + (jnp.log(nf / max_exact) / math.log(REL_MAX_DIST / max_exact)
                         * (REL_BUCKETS - max_exact)).astype(jnp.int32)
    large = jnp.minimum(large, REL_BUCKETS - 1)
    return jnp.where(n < max_exact, n, large)


def compress(k, pe, w1, w2):
    b_, t_, g_, dh = k.shape
    ch = k.reshape(b_, t_ // CMP_STRIDE, CMP_STRIDE, g_, dh)
    blocks = jnp.concatenate([ch[:, :-1], ch[:, 1:]], axis=2) + pe
    nc = blocks.shape[1]
    flat = blocks.transpose(0, 1, 3, 2, 4).reshape(b_, nc, g_, CMP_BLOCK * dh)
    hid = gelu(jnp.einsum('bngf,gfh->bngh', flat, w1))
    return jnp.einsum('bngh,ghd->bngd', hid, w2)


def nsa_attention(q, kc, vc, ks, vs, kw, vw, gate, rel_table):
    b_, t_ = q.shape[:2]
    n_qb = t_ // Q_BLOCK
    n_cmp = kc.shape[1]
    n_slc = t_ // SLC_BLOCK
    top_n = min(SLC_TOPN, n_slc)
    cmp_start = jnp.arange(n_cmp) * CMP_STRIDE
    cmp_end = cmp_start + CMP_BLOCK - 1
    slc_start = jnp.arange(n_slc) * SLC_BLOCK
    overlap = ((cmp_start[:, None] < slc_start[None, :] + SLC_BLOCK)
               & (cmp_start[:, None] + CMP_BLOCK > slc_start[None, :])).astype(jnp.float32)
    ks_blk = ks.reshape(b_, n_slc, SLC_BLOCK, NSA_KV, HEAD_DIM).transpose(0, 3, 1, 2, 4)
    vs_blk = vs.reshape(b_, n_slc, SLC_BLOCK, NSA_KV, HEAD_DIM).transpose(0, 3, 1, 2, 4)
    kw_pad = jnp.pad(kw, ((0, 0), (WINDOW, 0), (0, 0), (0, 0)))
    vw_pad = jnp.pad(vw, ((0, 0), (WINDOW, 0), (0, 0), (0, 0)))
    tab = rel_table.astype(jnp.float32)
    tab_g = tab.reshape(REL_BUCKETS, NSA_KV, NSA_REP).transpose(1, 0, 2)
    bidx = jnp.arange(b_)[:, None, None, None]
    gidx = jnp.arange(NSA_KV)[None, :, None, None]
    gidx5 = jnp.arange(NSA_KV)[None, :, None, None, None]
    slc_j = jnp.arange(n_slc)

    q_blocks = q.reshape(b_, n_qb, Q_BLOCK, NSA_KV, NSA_REP, HEAD_DIM).transpose(1, 0, 2, 3, 4, 5)
    g_blocks = gate.reshape(b_, n_qb, Q_BLOCK, NSA_KV, NSA_REP, N_BRANCH).transpose(1, 0, 2, 3, 4, 5)

    def head_bias(dist):
        bb = tab[rel_bucket(dist)].transpose(2, 0, 1)
        return bb.reshape(NSA_KV, NSA_REP, dist.shape[0], dist.shape[1])

    def one_block(args):
        qb, gb, i = args
        t = i * Q_BLOCK + jnp.arange(Q_BLOCK)
        dist_c = t[:, None] - cmp_end[None, :]
        mask_c = dist_c >= 0
        s_c = jnp.einsum('bqgrd,bngd->bgrqn', qb, kc).astype(jnp.float32) + head_bias(dist_c)
        p_c = jax.nn.softmax(jnp.where(mask_c, s_c, NEG), axis=-1) * mask_c
        o_c = jnp.einsum('bgrqn,bngd->bqgrd', p_c.astype(vc.dtype), vc)
        imp = jnp.einsum('bgrqn,ns->bgqs', p_c, overlap)
        cur = t // SLC_BLOCK
        forced = (slc_j[None, :] == 0) | (slc_j[None, :] == cur[:, None]) | (slc_j[None, :] == cur[:, None] - 1)
        future = slc_start[None, :] > t[:, None]
        imp = jnp.where(future, NEG, jnp.where(forced, BIG, imp))
        _, sel = lax.top_k(imp, top_n)
        k_sel = ks_blk[bidx, gidx, sel]
        v_sel = vs_blk[bidx, gidx, sel]
        pos_s = sel[..., None] * SLC_BLOCK + jnp.arange(SLC_BLOCK)
        dist_s = t[:, None, None] - pos_s
        mask_s = dist_s >= 0
        bias_s = tab_g[gidx5, rel_bucket(dist_s)].transpose(0, 1, 2, 5, 3, 4)
        s_s = jnp.einsum('bqgrd,bgqnsd->bgqrns', qb, k_sel).astype(jnp.float32) + bias_s
        s_s = jnp.where(mask_s[:, :, :, None], s_s, NEG)
        s_s = s_s.reshape(b_, NSA_KV, Q_BLOCK, NSA_REP, top_n * SLC_BLOCK)
        p_s = jax.nn.softmax(s_s, axis=-1)
        o_s = jnp.einsum('bgqrk,bgqkd->bqgrd', p_s.astype(vs.dtype),
                         v_sel.reshape(b_, NSA_KV, Q_BLOCK, top_n * SLC_BLOCK, HEAD_DIM))
        kw_b = lax.dynamic_slice_in_dim(kw_pad, i * Q_BLOCK, Q_BLOCK + WINDOW, axis=1)
        vw_b = lax.dynamic_slice_in_dim(vw_pad, i * Q_BLOCK, Q_BLOCK + WINDOW, axis=1)
        pos_w = i * Q_BLOCK - WINDOW + jnp.arange(Q_BLOCK + WINDOW)
        dist_w = t[:, None] - pos_w[None, :]
        mask_w = (dist_w >= 0) & (dist_w < WINDOW) & (pos_w[None, :] >= 0)
        s_w = jnp.einsum('bqgrd,bkgd->bgrqk', qb, kw_b).astype(jnp.float32) + head_bias(dist_w)
        p_w = jax.nn.softmax(jnp.where(mask_w, s_w, NEG), axis=-1)
        o_w = jnp.einsum('bgrqk,bkgd->bqgrd', p_w.astype(vw.dtype), vw_b)
        return gb[..., 0:1] * o_c + gb[..., 1:2] * o_s + gb[..., 2:3] * o_w

    o = lax.map(one_block, (q_blocks, g_blocks, jnp.arange(n_qb)))
    return o.transpose(1, 0, 2, 3, 4, 5).reshape(b_, t_, D_NSA)


def spatial_gating(u, v, ln_g, ln_b, w_s, b_s):
    b_, t_ = v.shape[:2]
    u = gelu(u)
    v = layernorm(gelu(v), ln_g, ln_b)
    vc = v.reshape(b_, t_ // SG_CHUNK, SG_CHUNK, SG_HEADS, SG_DIM)
    causal = jnp.tril(jnp.ones((SG_CHUNK, SG_CHUNK), dtype=bool))
    w = jnp.where(causal, w_s, 0.0)
    mixed = jnp.einsum('hts,bnshc->bnthc', w, vc) + b_s.T[None, None, :, :, None]
    return (u * mixed.reshape(b_, t_, SG_HEADS, SG_DIM)).reshape(b_, t_, SG_HEADS * SG_DIM)


def peer(h, w_query, sub_keys, expert_u, expert_v):
    b_, t_, d_ = h.shape
    q = jnp.einsum('btd,dq->btq', h, w_query).reshape(b_, t_, PEER_HEADS, 2, PEER_HALF)
    s = jnp.einsum('bthpc,pkc->bthpk', q, sub_keys).astype(jnp.float32)
    s1, i1 = lax.top_k(s[..., 0, :], PEER_TOPK)
    s2, i2 = lax.top_k(s[..., 1, :], PEER_TOPK)
    cand = (s1[..., :, None] + s2[..., None, :]).reshape(b_, t_, PEER_HEADS, PEER_TOPK * PEER_TOPK)
    cand_id = (i1[..., :, None] * PEER_NKEYS + i2[..., None, :]).reshape(b_, t_, PEER_HEADS, PEER_TOPK * PEER_TOPK)
    top_s, pos = lax.top_k(cand, PEER_TOPK)
    ids = jnp.take_along_axis(cand_id, pos, axis=-1)
    g = jax.nn.softmax(top_s, axis=-1).astype(h.dtype)
    n_blk = (b_ * t_) // PEER_TOKEN_BLOCK
    hb = h.reshape(n_blk, PEER_TOKEN_BLOCK, d_)
    idb = ids.reshape(n_blk, PEER_TOKEN_BLOCK, PEER_HEADS * PEER_TOPK)
    gb = g.reshape(n_blk, PEER_TOKEN_BLOCK, PEER_HEADS * PEER_TOPK)

    def one(args):
        hx, e, ge = args
        a = gelu(jnp.einsum('nd,nkd->nk', hx, expert_u[e])) * ge
        return jnp.einsum('nk,nkd->nd', a, expert_v[e])

    y = lax.map(one, (hb, idb, gb))
    return y.reshape(b_, t_, d_)


def setup_inputs(seed: int = 0) -> dict:
    key = jax.random.key(seed)
    ks = jax.random.split(key, 24)
    f32 = jnp.float32
    nrm = lambda k, shape, scale: jax.random.normal(k, shape, f32) * scale
    gain = lambda k, shape: 1.0 + 0.02 * jax.random.normal(k, shape, f32)
    return {
        'x': nrm(ks[0], (BATCH, SEQ, D_MODEL), 1.0),
        'attn_norm_g': gain(ks[1], (D_MODEL,)),
        'w_in': nrm(ks[2], (D_MODEL, D_IN), D_MODEL ** -0.5),
        'q_norm_g': gain(ks[3], (HEAD_DIM,)),
        'k_norm_g': gain(ks[4], (N_BRANCH, HEAD_DIM)),
        'cmp_pe_k': nrm(ks[5], (CMP_BLOCK, NSA_KV, HEAD_DIM), 0.1),
        'cmp_w1_k': nrm(ks[6], (NSA_KV, CMP_BLOCK * HEAD_DIM, CMP_HIDDEN), (CMP_BLOCK * HEAD_DIM) ** -0.5),
        'cmp_w2_k': nrm(ks[7], (NSA_KV, CMP_HIDDEN, HEAD_DIM), CMP_HIDDEN ** -0.5),
        'cmp_pe_v': nrm(ks[8], (CMP_BLOCK, NSA_KV, HEAD_DIM), 0.1),
        'cmp_w1_v': nrm(ks[9], (NSA_KV, CMP_BLOCK * HEAD_DIM, CMP_HIDDEN), (CMP_BLOCK * HEAD_DIM) ** -0.5),
        'cmp_w2_v': nrm(ks[10], (NSA_KV, CMP_HIDDEN, HEAD_DIM), CMP_HIDDEN ** -0.5),
        'rel_bias': nrm(ks[11], (REL_BUCKETS, NSA_HEADS), 0.2),
        'sg_ln_g': gain(ks[12], (SG_HEADS, SG_DIM)),
        'sg_ln_b': nrm(ks[13], (SG_HEADS, SG_DIM), 0.02),
        'sg_w': nrm(ks[14], (SG_HEADS, SG_CHUNK, SG_CHUNK), SG_CHUNK ** -0.5),
        'sg_b': gain(ks[15], (SG_HEADS, SG_CHUNK)),
        'out_norm_nsa': gain(ks[16], (D_NSA,)),
        'out_norm_sg': gain(ks[17], (D_SG,)),
        'w_out': nrm(ks[18], (D_MIX, D_MODEL), D_MIX ** -0.5),
        'ffn_norm_g': gain(ks[19], (D_MODEL,)),
        'peer_w_query': nrm(ks[20], (D_MODEL, PEER_HEADS * PEER_QDIM), D_MODEL ** -0.5),
        'peer_sub_keys': nrm(ks[21], (2, PEER_NKEYS, PEER_HALF), PEER_HALF ** -0.5),
        'peer_u': nrm(ks[22], (PEER_EXPERTS, D_MODEL), D_MODEL ** -0.5),
        'peer_v': nrm(ks[23], (PEER_EXPERTS, D_MODEL), 0.5),
    }


def reference(x, attn_norm_g, w_in, q_norm_g, k_norm_g, cmp_pe_k, cmp_w1_k, cmp_w2_k,
              cmp_pe_v, cmp_w1_v, cmp_w2_v, rel_bias, sg_ln_g, sg_ln_b, sg_w, sg_b,
              out_norm_nsa, out_norm_sg, w_out, ffn_norm_g, peer_w_query, peer_sub_keys,
              peer_u, peer_v):
    b_, t_, _ = x.shape
    for _layer in range(DEPTH):
        h = rmsnorm(x, attn_norm_g)
        proj = jnp.einsum('btd,de->bte', h, w_in)
        cuts = []
        acc = 0
        for size in (D_NSA, D_KV, D_KV, D_KV, D_KV, D_KV, D_KV, N_BRANCH * NSA_HEADS, D_SG):
            acc += size
            cuts.append(acc)
        q, kc, vc, ks_, vs_, kw, vw, gl, u, v = jnp.split(proj, cuts, axis=-1)
        kv_shape = (b_, t_, NSA_KV, HEAD_DIM)
        q = rmsnorm(q.reshape(b_, t_, NSA_HEADS, HEAD_DIM), q_norm_g) * (HEAD_DIM ** -0.5)
        kc = rmsnorm(compress(kc.reshape(kv_shape), cmp_pe_k, cmp_w1_k, cmp_w2_k), k_norm_g[0])
        vc = compress(vc.reshape(kv_shape), cmp_pe_v, cmp_w1_v, cmp_w2_v)
        ks_ = rmsnorm(ks_.reshape(kv_shape), k_norm_g[1])
        vs_ = vs_.reshape(kv_shape)
        kw = rmsnorm(kw.reshape(kv_shape), k_norm_g[2])
        vw = vw.reshape(kv_shape)
        gate = jax.nn.sigmoid(gl.astype(jnp.float32)).reshape(b_, t_, NSA_HEADS, N_BRANCH).astype(x.dtype)
        o_nsa = nsa_attention(q, kc, vc, ks_, vs_, kw, vw, gate, rel_bias)
        o_sg = spatial_gating(u.reshape(b_, t_, SG_HEADS, SG_DIM), v.reshape(b_, t_, SG_HEADS, SG_DIM),
                              sg_ln_g, sg_ln_b, sg_w, sg_b)
        mix = jnp.concatenate([rmsnorm(o_nsa, out_norm_nsa), rmsnorm(o_sg, out_norm_sg)], axis=-1)
        x = x + jnp.einsum('btm,md->btd', mix, w_out)
        x = x + peer(rmsnorm(x, ffn_norm_g), peer_w_query, peer_sub_keys, peer_u, peer_v)
    return x
```

```python
import functools
import math

import jax
import jax.numpy as jnp
import numpy as np
from jax import lax
from jax.experimental import pallas as pl
from jax.experimental.pallas import tpu as pltpu

F32 = jnp.float32
BF16 = jnp.bfloat16
MXU_DTYPE = BF16

EPS = 1e-6
NEG = -1e30
SQRT_HALF = 0.7071067811865476

NSA_HEADS = 16
NSA_KV = 4
NSA_REP = 4
HEAD_DIM = 64
D_NSA = NSA_HEADS * HEAD_DIM
D_KV = NSA_KV * HEAD_DIM
CMP_STRIDE = 16
CMP_BLOCK = 32
CMP_HIDDEN = 128
SLC_BLOCK = 64
SLC_TOPN = 16
WINDOW = 512
Q_BLOCK = 128
SG_HEADS = 8
SG_DIM = 128
SG_CHUNK = 128
D_SG = SG_HEADS * SG_DIM
REL_BUCKETS = 32
REL_MAX_DIST = 1024
PEER_HEADS = 8
PEER_NKEYS = 128
PEER_HALF = 128
PEER_TOPK = 16

LANES = 128
KEY_PAD = 512
CMP_PAD = 512
SEL_TILE = 512
WIN_KEYS = WINDOW + Q_BLOCK
NEAR_TILES = 2
BIAS_TABLE = 2048
FAR_DIST = 897
N_FORCED = 3
VMEM_LIMIT = 56 * 1024 * 1024


def _gelu(x):
    return 0.5 * x * (1.0 + lax.erf(x * SQRT_HALF))


def _dot(a, b):
    return jnp.dot(a.astype(MXU_DTYPE), b.astype(MXU_DTYPE), preferred_element_type=F32)


def _dot_nt(a, b):
    return lax.dot_general(a.astype(MXU_DTYPE), b.astype(MXU_DTYPE),
                           (((1,), (1,)), ((), ())), preferred_element_type=F32)


def _rms(x, g):
    ms = jnp.mean(x * x, axis=-1, keepdims=True)
    return x * lax.rsqrt(ms + EPS) * g


def _params(sem):
    return pltpu.CompilerParams(dimension_semantics=sem, vmem_limit_bytes=VMEM_LIMIT)


def _attn_proj_kernel(x_ref, ng_ref, w_ref, qg_ref, kg_ref,
                      q_out, kcv_out, ks_out, vs_out, kw_out, vw_out, gate_out):
    h = _rms(x_ref[0], ng_ref[...]).astype(MXU_DTYPE)
    tm = h.shape[0]

    pq = _dot(h, w_ref[:, 0:D_NSA])
    qg = qg_ref[...]
    for hh in range(NSA_HEADS):
        sl = slice(HEAD_DIM * hh, HEAD_DIM * (hh + 1))
        q_out[0, :, sl] = (_rms(pq[:, sl], qg) * (HEAD_DIM ** -0.5)).astype(q_out.dtype)

    kcv_out[0] = _dot(h, w_ref[:, D_NSA:D_NSA + 2 * D_KV]).astype(kcv_out.dtype)

    base = D_NSA + 2 * D_KV
    p4 = _dot(h, w_ref[:, base:base + 4 * D_KV])
    ones_col = (lax.broadcasted_iota(jnp.int32, (tm, HEAD_DIM), 1) == 0).astype(F32)
    for g in range(NSA_KV):
        sl = slice(HEAD_DIM * g, HEAD_DIM * (g + 1))
        ks_out[0, g] = _rms(p4[:, 0 * D_KV:1 * D_KV][:, sl], kg_ref[1:2, :]).astype(ks_out.dtype)
        kw_out[0, g] = _rms(p4[:, 2 * D_KV:3 * D_KV][:, sl], kg_ref[2:3, :]).astype(kw_out.dtype)
        vs_out[0, g, :, 0:HEAD_DIM] = p4[:, 1 * D_KV:2 * D_KV][:, sl].astype(vs_out.dtype)
        vs_out[0, g, :, HEAD_DIM:2 * HEAD_DIM] = ones_col.astype(vs_out.dtype)
        vw_out[0, g, :, 0:HEAD_DIM] = p4[:, 3 * D_KV:4 * D_KV][:, sl].astype(vw_out.dtype)
        vw_out[0, g, :, HEAD_DIM:2 * HEAD_DIM] = ones_col.astype(vw_out.dtype)

    base = base + 4 * D_KV
    gate_out[0] = jax.nn.sigmoid(_dot(h, w_ref[:, base:base + NSA_KV * LANES]))


def _attn_proj(x, ng, w, qg, kg, tm):
    b_, t_, d_ = x.shape
    nw = w.shape[1]
    kv4 = lambda wd: pl.BlockSpec((1, NSA_KV, tm, wd), lambda b, i: (b, 0, i, 0))
    return pl.pallas_call(
        _attn_proj_kernel,
        grid=(b_, t_ // tm),
        in_specs=[
            pl.BlockSpec((1, tm, d_), lambda b, i: (b, i, 0)),
            pl.BlockSpec((1, d_), lambda b, i: (0, 0)),
            pl.BlockSpec((d_, nw), lambda b, i: (0, 0), pipeline_mode=pl.Buffered(1)),
            pl.BlockSpec((1, HEAD_DIM), lambda b, i: (0, 0)),
            pl.BlockSpec((3, HEAD_DIM), lambda b, i: (0, 0)),
        ],
        out_specs=[
            pl.BlockSpec((1, tm, D_NSA), lambda b, i: (b, i, 0)),
            pl.BlockSpec((1, tm, 2 * D_KV), lambda b, i: (b, i, 0)),
            kv4(HEAD_DIM), kv4(2 * HEAD_DIM), kv4(HEAD_DIM), kv4(2 * HEAD_DIM),
            pl.BlockSpec((1, tm, NSA_KV * LANES), lambda b, i: (b, i, 0)),
        ],
        out_shape=[
            jax.ShapeDtypeStruct((b_, t_, D_NSA), MXU_DTYPE),
            jax.ShapeDtypeStruct((b_, t_, 2 * D_KV), MXU_DTYPE),
            jax.ShapeDtypeStruct((b_, NSA_KV, t_, HEAD_DIM), MXU_DTYPE),
            jax.ShapeDtypeStruct((b_, NSA_KV, t_, 2 * HEAD_DIM), MXU_DTYPE),
            jax.ShapeDtypeStruct((b_, NSA_KV, t_, HEAD_DIM), MXU_DTYPE),
            jax.ShapeDtypeStruct((b_, NSA_KV, t_, 2 * HEAD_DIM), MXU_DTYPE),
            jax.ShapeDtypeStruct((b_, t_, NSA_KV * LANES), F32),
        ],
        compiler_params=_params(("parallel", "parallel")),
        name="attn_proj",
    )(x, ng, w, qg, kg)


def _sgu_proj_kernel(x_ref, ng_ref, w_ref, lng_ref, lnb_ref, gu_out, vn_out):
    h = _rms(x_ref[0], ng_ref[...]).astype(MXU_DTYPE)
    gu_out[0] = _gelu(_dot(h, w_ref[:, 0:D_SG])).astype(gu_out.dtype)
    gv = _gelu(_dot(h, w_ref[:, D_SG:2 * D_SG]))
    for hh in range(SG_HEADS):
        sl = slice(SG_DIM * hh, SG_DIM * (hh + 1))
        v = gv[:, sl]
        mu = jnp.mean(v, axis=-1, keepdims=True)
        var = jnp.mean(jnp.square(v - mu), axis=-1, keepdims=True)
        y = (v - mu) * lax.rsqrt(var + EPS) * lng_ref[hh:hh + 1, :] + lnb_ref[hh:hh + 1, :]
        vn_out[0, :, sl] = y.astype(vn_out.dtype)


def _sgu_proj(x, ng, w, lng, lnb, tm):
    b_, t_, d_ = x.shape
    return pl.pallas_call(
        _sgu_proj_kernel,
        grid=(b_, t_ // tm),
        in_specs=[
            pl.BlockSpec((1, tm, d_), lambda b, i: (b, i, 0)),
            pl.BlockSpec((1, d_), lambda b, i: (0, 0)),
            pl.BlockSpec((d_, 2 * D_SG), lambda b, i: (0, 0), pipeline_mode=pl.Buffered(1)),
            pl.BlockSpec((SG_HEADS, SG_DIM), lambda b, i: (0, 0)),
            pl.BlockSpec((SG_HEADS, SG_DIM), lambda b, i: (0, 0)),
        ],
        out_specs=[
            pl.BlockSpec((1, tm, D_SG), lambda b, i: (b, i, 0)),
            pl.BlockSpec((1, tm, D_SG), lambda b, i: (b, i, 0)),
        ],
        out_shape=[
            jax.ShapeDtypeStruct((b_, t_, D_SG), MXU_DTYPE),
            jax.ShapeDtypeStruct((b_, t_, D_SG), MXU_DTYPE),
        ],
        compiler_params=_params(("parallel", "parallel")),
        name="sgu_proj",
    )(x, ng, w, lng, lnb)


def _compress_kernel(x_ref, w1_ref, w2_ref, pe_ref, kg_ref, out_ref):
    which = pl.program_id(1)
    x = x_ref[0, 0, 0]
    nch = x.shape[0]
    half = CMP_STRIDE * HEAD_DIM
    a = _dot(x, w1_ref[0, 0, 0:half, :])
    bm = _dot(x, w1_ref[0, 0, half:2 * half, :])
    pec = _dot(pe_ref[0, 0], w1_ref[0, 0])
    hid = _gelu(a + pltpu.roll(bm, shift=nch - 1, axis=0) + pec)
    y = _dot(hid, w2_ref[0, 0])
    y = jnp.where(which == 0, _rms(y, kg_ref[0:1, :]), y)
    out_ref[0, 0, 0, 0:CMP_PAD, :] = jnp.zeros((CMP_PAD, HEAD_DIM), out_ref.dtype)
    out_ref[0, 0, 0, CMP_PAD:CMP_PAD + nch, :] = y.astype(out_ref.dtype)


def _compress(xc, w1, w2, pe, kg):
    b_, _, g_, nch, fl = xc.shape
    return pl.pallas_call(
        _compress_kernel,
        grid=(b_, 2, g_),
        in_specs=[
            pl.BlockSpec((1, 1, 1, nch, fl), lambda b, w, g: (b, w, g, 0, 0)),
            pl.BlockSpec((1, 1, 2 * fl, CMP_HIDDEN), lambda b, w, g: (w, g, 0, 0)),
            pl.BlockSpec((1, 1, CMP_HIDDEN, HEAD_DIM), lambda b, w, g: (w, g, 0, 0)),
            pl.BlockSpec((1, 1, 1, 2 * fl), lambda b, w, g: (w, g, 0, 0)),
            pl.BlockSpec((3, HEAD_DIM), lambda b, w, g: (0, 0)),
        ],
        out_specs=pl.BlockSpec((1, 1, 1, CMP_PAD + nch, HEAD_DIM), lambda b, w, g: (b, w, g, 0, 0)),
        out_shape=jax.ShapeDtypeStruct((b_, 2, g_, CMP_PAD + nch, HEAD_DIM), F32),
        compiler_params=_params(("parallel", "parallel", "parallel")),
        name="nsa_compress",
    )(xc, w1, w2, pe, kg)


def _nsa_kernel(q_ref, gate_ref, kc_ref, vc_ref, ks_ref, vs_ref, kw_ref, vw_ref,
                tbw_ref, wb_ref, cb_ref, cfar_ref, ov_ref, o_ref):
    i = pl.program_id(2)
    qb = Q_BLOCK
    rows = NSA_REP * qb
    ncmp = CMP_PAD

    qblk = q_ref[0]
    q_all = jnp.concatenate(
        [qblk[:, HEAD_DIM * r:HEAD_DIM * (r + 1)] for r in range(NSA_REP)], axis=0)
    cfar = [cfar_ref[0, r][:, 0:1] for r in range(NSA_REP)]

    cstart = pl.multiple_of(8 * i + 8, 8)
    kc = kc_ref[0, 0, 0, pl.ds(cstart, ncmp), :]
    vc = vc_ref[0, 0, 0, pl.ds(cstart, ncmp), :]
    s_c = _dot_nt(q_all, kc)
    r_io = lax.broadcasted_iota(jnp.int32, (qb, ncmp), 0)
    c_io = lax.broadcasted_iota(jnp.int32, (qb, ncmp), 1)
    dist_c = r_io - CMP_STRIDE * c_io + (CMP_STRIDE * ncmp - qb - (CMP_BLOCK - 1))
    vis_c = (dist_c >= 0) & (c_io >= ncmp - 8 - 8 * i)
    vis_cf = vis_c.astype(F32)
    o_c = []
    psum = jnp.zeros((qb, ncmp), F32)
    for r in range(NSA_REP):
        bias = jnp.concatenate(
            [jnp.broadcast_to(cfar[r], (qb, ncmp - LANES)), cb_ref[0, r]], axis=1)
        s = jnp.where(vis_c, s_c[qb * r:qb * (r + 1)] + bias, NEG)
        m = jnp.max(s, axis=-1, keepdims=True)
        p = jnp.exp(s - m) * vis_cf
        l = jnp.sum(p, axis=-1, keepdims=True)
        p = p * jnp.where(l > 0.0, 1.0 / l, 0.0)
        o_c.append(_dot(p, vc))
        psum = psum + p
    p_hi = psum.astype(BF16)
    p_lo = (psum - p_hi.astype(F32)).astype(BF16)
    ov = ov_ref[...]
    imp = (jnp.dot(p_hi, ov, preferred_element_type=F32)
           + jnp.dot(p_lo, ov, preferred_element_type=F32))

    nsel = LANES
    sp = lax.broadcasted_iota(jnp.int32, (qb, nsel), 1)
    rr = lax.broadcasted_iota(jnp.int32, (qb, nsel), 0)
    cur = (nsel - 2) + (rr >= SLC_BLOCK).astype(jnp.int32)
    s_abs = sp + (2 * i + 2 - nsel)
    valid = s_abs >= 0
    forced = ((s_abs == 0) | (sp == cur) | (sp == cur - 1)) & valid
    excluded = forced | (sp > cur) | (~valid)
    cand_t = jnp.where(excluded, NEG, imp).T
    v = cand_t
    tau = None
    for _ in range(SLC_TOPN - N_FORCED):
        tau = jnp.max(v, axis=0, keepdims=True)
        v = jnp.where(v >= tau, 3.0 * NEG, v)
    picked_t = ((cand_t >= tau) & (cand_t > 0.5 * NEG)).astype(F32)
    selmask = jnp.maximum(picked_t.T, forced.astype(F32)).astype(BF16)

    kt = SEL_TILE
    spk = SLC_BLOCK
    e_row = lax.broadcasted_iota(jnp.int32, (nsel, kt), 0)
    e_col = lax.broadcasted_iota(jnp.int32, (nsel, kt), 1) // spk
    t_r = lax.broadcasted_iota(jnp.int32, (qb, kt), 0)
    t_c = lax.broadcasted_iota(jnp.int32, (qb, kt), 1)

    def sel_tile(dd, carry, near):
        m_old, acc = carry
        start = pl.multiple_of(KEY_PAD + qb * (i + 1) - kt * (dd + 1), qb)
        kk = ks_ref[0, 0, pl.ds(start, kt), :]
        vv = vs_ref[0, 0, pl.ds(start, kt), :]
        s = _dot_nt(q_all, kk)
        expand = (e_row == e_col + (nsel - (kt // spk) * (dd + 1))).astype(BF16)
        msk = jnp.dot(selmask, expand, preferred_element_type=F32) > 0.5
        msk = msk & (t_r - t_c + (kt * (dd + 1) - qb) >= 0)
        parts = []
        for r in range(NSA_REP):
            bias = tbw_ref[0, r, dd] if near else cfar[r]
            parts.append(jnp.where(msk, s[qb * r:qb * (r + 1)] + bias, NEG))
        s = jnp.concatenate(parts, axis=0)
        m_new = jnp.maximum(m_old, jnp.max(s, axis=-1, keepdims=True))
        alpha = jnp.exp(m_old - m_new)
        p = jnp.exp(s - m_new)
        acc = alpha * acc + _dot(p, vv)
        return m_new, acc

    n_tiles = (i * qb + qb + kt - 1) // kt
    carry = (jnp.full((rows, 1), NEG, F32), jnp.zeros((rows, 2 * HEAD_DIM), F32))
    carry = lax.fori_loop(0, jnp.minimum(n_tiles, NEAR_TILES),
                          functools.partial(sel_tile, near=True), carry)
    carry = lax.fori_loop(NEAR_TILES, n_tiles, functools.partial(sel_tile, near=False), carry)
    acc_s = carry[1]

    wstart = pl.multiple_of(KEY_PAD + qb * (i + 1) - WIN_KEYS, qb)
    kk = kw_ref[0, 0, pl.ds(wstart, WIN_KEYS), :]
    vv = vw_ref[0, 0, pl.ds(wstart, WIN_KEYS), :]
    s_w = _dot_nt(q_all, kk)
    w_r = lax.broadcasted_iota(jnp.int32, (qb, WIN_KEYS), 0)
    w_c = lax.broadcasted_iota(jnp.int32, (qb, WIN_KEYS), 1)
    dist_w = w_r - w_c + WINDOW
    vis_w = (dist_w >= 0) & (dist_w < WINDOW) & (w_c >= WIN_KEYS - qb * (i + 1))
    parts = []
    for r in range(NSA_REP):
        parts.append(jnp.where(vis_w, s_w[qb * r:qb * (r + 1)] + wb_ref[0, r], NEG))
    s = jnp.concatenate(parts, axis=0)
    p = jnp.exp(s - jnp.max(s, axis=-1, keepdims=True))
    acc_w = _dot(p, vv)

    gt = gate_ref[0]
    for r in range(NSA_REP):
        rs = slice(qb * r, qb * (r + 1))
        o_s = acc_s[rs, 0:HEAD_DIM] / acc_s[rs, HEAD_DIM:HEAD_DIM + 1]
        o_w = acc_w[rs, 0:HEAD_DIM] / acc_w[rs, HEAD_DIM:HEAD_DIM + 1]
        o = (gt[:, 3 * r + 0:3 * r + 1] * o_c[r] + gt[:, 3 * r + 1:3 * r + 2] * o_s
             + gt[:, 3 * r + 2:3 * r + 3] * o_w)
        o_ref[0, :, HEAD_DIM * r:HEAD_DIM * (r + 1)] = o.astype(o_ref.dtype)


def _nsa_attention(qn, gate, kcv, ksp, vsp, kwp, vwp, tbw, wb, cb, cfar, ov):
    b_, t_, _ = qn.shape
    tp = ksp.shape[2]
    ncp = kcv.shape[3]
    gw = NSA_REP * HEAD_DIM
    kvspec = lambda wd: pl.BlockSpec((1, 1, tp, wd), lambda b, g, i: (b, g, 0, 0))
    return pl.pallas_call(
        _nsa_kernel,
        grid=(b_, NSA_KV, t_ // Q_BLOCK),
        in_specs=[
            pl.BlockSpec((1, Q_BLOCK, gw), lambda b, g, i: (b, i, g)),
            pl.BlockSpec((1, Q_BLOCK, LANES), lambda b, g, i: (b, i, g)),
            pl.BlockSpec((1, 1, 1, ncp, HEAD_DIM), lambda b, g, i: (b, 0, g, 0, 0)),
            pl.BlockSpec((1, 1, 1, ncp, HEAD_DIM), lambda b, g, i: (b, 1, g, 0, 0)),
            kvspec(HEAD_DIM), kvspec(2 * HEAD_DIM), kvspec(HEAD_DIM), kvspec(2 * HEAD_DIM),
            pl.BlockSpec((1, NSA_REP, NEAR_TILES, Q_BLOCK, SEL_TILE), lambda b, g, i: (g, 0, 0, 0, 0)),
            pl.BlockSpec((1, NSA_REP, Q_BLOCK, WIN_KEYS), lambda b, g, i: (g, 0, 0, 0)),
            pl.BlockSpec((1, NSA_REP, Q_BLOCK, LANES), lambda b, g, i: (g, 0, 0, 0)),
            pl.BlockSpec((1, NSA_REP, 1, LANES), lambda b, g, i: (g, 0, 0, 0)),
            pl.BlockSpec((CMP_PAD, LANES), lambda b, g, i: (0, 0)),
        ],
        out_specs=pl.BlockSpec((1, Q_BLOCK, gw), lambda b, g, i: (b, i, g)),
        out_shape=jax.ShapeDtypeStruct((b_, t_, D_NSA), F32),
        compiler_params=_params(("parallel", "parallel", "arbitrary")),
        name="nsa_attention",
    )(qn, gate, kcv, kcv, ksp, vsp, kwp, vwp, tbw, wb, cb, cfar, ov)


def _mix_kernel(x_ref, on_ref, gu_ref, vn_ref, sgw_ref, sgb_ref, gn_ref, gs_ref, wo_ref,
                fg_ref, wq_ref, x1_out, h2_out, qp_out):
    tm = x_ref.shape[0]
    tri = (lax.broadcasted_iota(jnp.int32, (SG_CHUNK, SG_CHUNK), 0)
           >= lax.broadcasted_iota(jnp.int32, (SG_CHUNK, SG_CHUNK), 1))
    sgb = sgb_ref[...]
    y = x_ref[...] + _dot(_rms(on_ref[...], gn_ref[...]), wo_ref[0:D_NSA, :])
    parts = []
    for c in range(tm // SG_CHUNK):
        cs = slice(SG_CHUNK * c, SG_CHUNK * (c + 1))
        heads = []
        for hh in range(SG_HEADS):
            sl = slice(SG_DIM * hh, SG_DIM * (hh + 1))
            w = jnp.where(tri, sgw_ref[hh], 0.0)
            mixed = _dot(w, vn_ref[cs, sl]) + sgb[:, hh:hh + 1]
            heads.append(gu_ref[cs, sl].astype(F32) * mixed)
        parts.append(jnp.concatenate(heads, axis=1))
    o_sg = jnp.concatenate(parts, axis=0)
    y = y + _dot(_rms(o_sg, gs_ref[...]), wo_ref[D_NSA:D_NSA + D_SG, :])
    x1_out[...] = y
    h2 = _rms(y, fg_ref[...]).astype(MXU_DTYPE)
    h2_out[...] = h2
    qp_out[...] = _dot(h2, wq_ref[...])


def _mix(x2, onsa, gu, vn, sgw, sgb_t, gn, gs, wo, fg, wq, tm):
    n_, d_ = x2.shape
    row = lambda wd: pl.BlockSpec((tm, wd), lambda i: (i, 0))
    const = lambda shape, **kw: pl.BlockSpec(shape, lambda i: (0,) * len(shape), **kw)
    return pl.pallas_call(
        _mix_kernel,
        grid=(n_ // tm,),
        in_specs=[
            row(d_), row(D_NSA), row(D_SG), row(D_SG),
            const((SG_HEADS, SG_CHUNK, SG_CHUNK)), const((SG_CHUNK, SG_HEADS)),
            const((1, D_NSA)), const((1, D_SG)),
            const((D_NSA + D_SG, d_), pipeline_mode=pl.Buffered(1)),
            const((1, d_)),
            const((d_, wq.shape[1]), pipeline_mode=pl.Buffered(1)),
        ],
        out_specs=[row(d_), row(d_), row(wq.shape[1])],
        out_shape=[
            jax.ShapeDtypeStruct((n_, d_), F32),
            jax.ShapeDtypeStruct((n_, d_), MXU_DTYPE),
            jax.ShapeDtypeStruct((n_, wq.shape[1]), F32),
        ],
        compiler_params=_params(("parallel",)),
        name="mix_out_proj",
    )(x2, onsa, gu, vn, sgw, sgb_t, gn, gs, wo, fg, wq)


def _top_rows(v, k):
    tops = []
    for _ in range(k):
        m = jnp.max(v, axis=0, keepdims=True)
        tops.append(m)
        v = jnp.where(v >= m, NEG, v)
    return jnp.concatenate(tops, axis=0)


def _peer_topk_kernel(qp_ref, sk_ref, s1_out, s2_out, st_out):
    for hh in range(PEER_HEADS):
        base = 2 * PEER_HALF * hh
        s1 = _dot_nt(sk_ref[0], qp_ref[:, base:base + PEER_HALF])
        s2 = _dot_nt(sk_ref[1], qp_ref[:, base + PEER_HALF:base + 2 * PEER_HALF])
        s1_out[hh] = s1
        s2_out[hh] = s2
        a = _top_rows(s1, PEER_TOPK)
        b = _top_rows(s2, PEER_TOPK)
        cand = [a[0:1] + b]
        for ra in range(1, 8):
            cand.append(a[ra:ra + 1] + b[0:8])
        cand.append(a[8:16] + b[0:1])
        top = _top_rows(jnp.concatenate(cand, axis=0), PEER_TOPK)
        z = jnp.sum(jnp.exp(top - top[0:1]), axis=0, keepdims=True)
        st_out[0, hh:hh + 1, :] = top[PEER_TOPK - 1:PEER_TOPK]
        st_out[1, hh:hh + 1, :] = a[0:1]
        st_out[2, hh:hh + 1, :] = b[0:1]
        st_out[3, hh:hh + 1, :] = 1.0 / z


def _peer_topk(qp, sub_keys, tm):
    n_, qd = qp.shape
    return pl.pallas_call(
        _peer_topk_kernel,
        grid=(n_ // tm,),
        in_specs=[
            pl.BlockSpec((tm, qd), lambda i: (i, 0)),
            pl.BlockSpec((2, PEER_NKEYS, PEER_HALF), lambda i: (0, 0, 0)),
        ],
        out_specs=[
            pl.BlockSpec((PEER_HEADS, PEER_NKEYS, tm), lambda i: (0, 0, i)),
            pl.BlockSpec((PEER_HEADS, PEER_NKEYS, tm), lambda i: (0, 0, i)),
            pl.BlockSpec((4, PEER_HEADS, tm), lambda i: (0, 0, i)),
        ],
        out_shape=[
            jax.ShapeDtypeStruct((PEER_HEADS, PEER_NKEYS, n_), F32),
            jax.ShapeDtypeStruct((PEER_HEADS, PEER_NKEYS, n_), F32),
            jax.ShapeDtypeStruct((4, PEER_HEADS, n_), F32),
        ],
        compiler_params=_params(("parallel",)),
        name="peer_topk",
    )(qp, sub_keys)


def _peer_mix_kernel(h_ref, u_ref, vt_ref, s1_ref, s2_ref, st_ref, x1_ref, out_ref,
                     acc_ref, e2_ref, z_ref):
    ie = pl.program_id(1)
    te = u_ref.shape[0]

    @pl.when(ie == 0)
    def _():
        acc_ref[...] = jnp.zeros_like(acc_ref)
        for hh in range(PEER_HEADS):
            e2_ref[hh] = jnp.exp(s2_ref[hh] - st_ref[2, hh:hh + 1, :])

    a_t = _dot_nt(u_ref[...], h_ref[...])
    for k in range(te // PEER_NKEYS):
        i1 = ie * (te // PEER_NKEYS) + k
        g = jnp.zeros((PEER_NKEYS, a_t.shape[1]), F32)
        for hh in range(PEER_HEADS):
            c = s1_ref[hh, pl.ds(i1, 1), :]
            e1 = jnp.exp(c - st_ref[1, hh:hh + 1, :]) * st_ref[3, hh:hh + 1, :]
            hit = (s2_ref[hh] + c) >= st_ref[0, hh:hh + 1, :]
            g = g + jnp.where(hit, e2_ref[hh] * e1, 0.0)
        rs = slice(PEER_NKEYS * k, PEER_NKEYS * (k + 1))
        z_ref[rs, :] = (_gelu(a_t[rs]) * g).astype(z_ref.dtype)
    acc_ref[...] += jnp.dot(vt_ref[...], z_ref[...], preferred_element_type=F32)

    @pl.when(ie == pl.num_programs(1) - 1)
    def _():
        out_ref[...] = x1_ref[...] + acc_ref[...].T


def _peer_mix(h2, u, vt, s1t, s2t, stats, x1, tm, te):
    n_, d_ = h2.shape
    ne = u.shape[0]
    return pl.pallas_call(
        _peer_mix_kernel,
        grid=(n_ // tm, ne // te),
        in_specs=[
            pl.BlockSpec((tm, d_), lambda it, ie: (it, 0)),
            pl.BlockSpec((te, d_), lambda it, ie: (ie, 0)),
            pl.BlockSpec((d_, te), lambda it, ie: (0, ie)),
            pl.BlockSpec((PEER_HEADS, PEER_NKEYS, tm), lambda it, ie: (0, 0, it)),
            pl.BlockSpec((PEER_HEADS, PEER_NKEYS, tm), lambda it, ie: (0, 0, it)),
            pl.BlockSpec((4, PEER_HEADS, tm), lambda it, ie: (0, 0, it)),
            pl.BlockSpec((tm, d_), lambda it, ie: (it, 0)),
        ],
        out_specs=pl.BlockSpec((tm, d_), lambda it, ie: (it, 0)),
        out_shape=jax.ShapeDtypeStruct((n_, d_), F32),
        scratch_shapes=[
            pltpu.VMEM((d_, tm), F32),
            pltpu.VMEM((PEER_HEADS, PEER_NKEYS, tm), F32),
            pltpu.VMEM((te, tm), MXU_DTYPE),
        ],
        compiler_params=_params(("parallel", "arbitrary")),
        name="peer_mix",
    )(h2, u, vt, s1t, s2t, stats, x1)


def _rel_bucket(dist):
    n = jnp.maximum(dist, 0)
    max_exact = REL_BUCKETS // 2
    nf = jnp.maximum(n, 1).astype(F32)
    large = max_exact + (jnp.log(nf / max_exact) / math.log(REL_MAX_DIST / max_exact)
                         * (REL_BUCKETS - max_exact)).astype(jnp.int32)
    large = jnp.minimum(large, REL_BUCKETS - 1)
    return jnp.where(n < max_exact, n, large)


def _toeplitz(btab, off, rows, cols):
    per = rows + cols
    k = np.concatenate([np.arange(cols), np.zeros(1, np.int64), np.arange(-(rows - 1), 0)])
    idx = np.clip(off - k, 0, btab.shape[1] - 1)
    w = jnp.take(btab, jnp.asarray(idx, jnp.int32), axis=1)
    x = jnp.tile(w, (1, rows))[:, :rows * (per - 1)]
    return x.reshape(btab.shape[0], rows, per - 1)[:, :, :cols]


def _bias_tiles(rel_bias):
    dist = jnp.arange(BIAS_TABLE, dtype=jnp.int32)
    btab = rel_bias.astype(F32)[_rel_bucket(dist)].T
    tbw = jnp.stack([_toeplitz(btab, SEL_TILE * (dd + 1) - Q_BLOCK, Q_BLOCK, SEL_TILE)
                     for dd in range(NEAR_TILES)], axis=1)
    wb = _toeplitz(btab, WINDOW, Q_BLOCK, WIN_KEYS)
    off = CMP_STRIDE * (LANES - 8) - (CMP_BLOCK - 1)
    front = CMP_STRIDE * LANES - off
    length = CMP_STRIDE * (LANES + 8 + 2)
    padded = jnp.pad(btab, ((0, 0), (front, length - front - BIAS_TABLE)))
    ch = padded.reshape(btab.shape[0], length // CMP_STRIDE, CMP_STRIDE)
    nwin = LANES + 1
    win = jnp.stack([ch[:, a:a + nwin] for a in range(Q_BLOCK // CMP_STRIDE)], axis=2)
    win = win.reshape(btab.shape[0], nwin, Q_BLOCK)
    cb = jnp.swapaxes(win[:, 1:nwin][:, ::-1], 1, 2)
    cfar = rel_bias.astype(F32)[REL_BUCKETS - 1]
    grp = lambda a: a.reshape((NSA_KV, NSA_REP) + a.shape[1:])
    cfar = jnp.broadcast_to(cfar[:, None, None], (NSA_HEADS, 1, LANES))
    return grp(tbw), grp(wb), grp(cb), grp(cfar)


def _overlap_matrix():
    c = np.arange(CMP_PAD)[:, None]
    s = np.arange(LANES)[None, :]
    per = SLC_BLOCK // CMP_STRIDE
    ov = (s == c // per) | ((c % per == per - 1) & (s == c // per + 1))
    return jnp.asarray(ov.astype(np.float32), BF16)


def kernel(x, attn_norm_g, w_in, q_norm_g, k_norm_g, cmp_pe_k, cmp_w1_k, cmp_w2_k, cmp_pe_v,
           cmp_w1_v, cmp_w2_v, rel_bias, sg_ln_g, sg_ln_b, sg_w, sg_b, out_norm_nsa, out_norm_sg,
           w_out, ffn_norm_g, peer_w_query, peer_sub_keys, peer_u, peer_v):
    b_, t_, d_ = x.shape
    n_ = b_ * t_
    assert t_ % Q_BLOCK == 0 and t_ >= WIN_KEYS and d_ == D_NSA + D_SG

    c_kv = D_NSA + 6 * D_KV
    c_gate = c_kv + 3 * NSA_HEADS
    wg = w_in[:, c_kv:c_gate].reshape(d_, NSA_KV, 3 * NSA_REP)
    wg = jnp.pad(wg, ((0, 0), (0, 0), (0, LANES - 3 * NSA_REP))).reshape(d_, NSA_KV * LANES)
    w_attn = jnp.concatenate([w_in[:, :c_kv], wg], axis=1).astype(MXU_DTYPE)
    w_sgu = w_in[:, c_gate:].astype(MXU_DTYPE)
    ng = attn_norm_g.reshape(1, d_)

    qn, kcv, ks, vs, kw, vw, gate = _attn_proj(x, ng, w_attn, q_norm_g.reshape(1, HEAD_DIM),
                                               k_norm_g, tm=256)
    gu, vn = _sgu_proj(x, ng, w_sgu, sg_ln_g, sg_ln_b, tm=256)

    nch = t_ // CMP_STRIDE
    xc = kcv.reshape(b_, nch, CMP_STRIDE, 2, NSA_KV, HEAD_DIM).transpose(0, 3, 4, 1, 2, 5)
    xc = xc.reshape(b_, 2, NSA_KV, nch, CMP_STRIDE * HEAD_DIM)
    w1 = jnp.stack([cmp_w1_k, cmp_w1_v]).astype(MXU_DTYPE)
    w2 = jnp.stack([cmp_w2_k, cmp_w2_v]).astype(MXU_DTYPE)
    pe = jnp.stack([cmp_pe_k, cmp_pe_v]).transpose(0, 2, 1, 3).reshape(2, NSA_KV, 1, CMP_BLOCK * HEAD_DIM)
    kcv_c = _compress(xc, w1, w2, pe.astype(MXU_DTYPE), k_norm_g)

    padk = lambda a: jnp.pad(a, ((0, 0), (0, 0), (KEY_PAD, 0), (0, 0)))
    tbw, wb, cb, cfar = _bias_tiles(rel_bias)
    onsa = _nsa_attention(qn, gate, kcv_c, padk(ks), padk(vs), padk(kw), padk(vw),
                          tbw, wb, cb, cfar, _overlap_matrix())

    x1, h2, qp = _mix(x.reshape(n_, d_), onsa.reshape(n_, D_NSA), gu.reshape(n_, D_SG),
                      vn.reshape(n_, D_SG), sg_w, sg_b.T, out_norm_nsa.reshape(1, D_NSA),
                      out_norm_sg.reshape(1, D_SG), w_out.astype(MXU_DTYPE),
                      ffn_norm_g.reshape(1, d_), peer_w_query.astype(MXU_DTYPE), tm=256)

    s1t, s2t, stats = _peer_topk(qp, peer_sub_keys, tm=256)
    out = _peer_mix(h2, peer_u.astype(MXU_DTYPE), peer_v.T.astype(MXU_DTYPE), s1t, s2t, stats, x1,
                    tm=min(512, n_), te=512)
    return out.reshape(b_, t_, d_)
```

```python
import functools
import math

import jax
import jax.numpy as jnp
import numpy as np
from jax import lax
from jax.experimental import pallas as pl
from jax.experimental.pallas import tpu as pltpu

F32 = jnp.float32
BF16 = jnp.bfloat16
MXU_DTYPE = BF16

EPS = 1e-6
NEG = -1e30
SQRT_HALF = 0.7071067811865476

NSA_HEADS = 16
NSA_KV = 4
NSA_REP = 4
HEAD_DIM = 64
D_NSA = NSA_HEADS * HEAD_DIM
D_KV = NSA_KV * HEAD_DIM
CMP_STRIDE = 16
CMP_BLOCK = 32
CMP_HIDDEN = 128
SLC_BLOCK = 64
SLC_TOPN = 16
WINDOW = 512
Q_BLOCK = 128
SG_HEADS = 8
SG_DIM = 128
SG_CHUNK = 128
D_SG = SG_HEADS * SG_DIM
REL_BUCKETS = 32
REL_MAX_DIST = 1024
PEER_HEADS = 8
PEER_NKEYS = 128
PEER_HALF = 128
PEER_TOPK = 16

LANES = 128
MXU_ROWS = 256
A_KCHUNK = 1024
GATE_ROWS = 64
KEY_PAD = 512
CMP_PAD = 512
SEL_TILE = 512
WIN_KEYS = WINDOW + Q_BLOCK
NEAR_TILES = 2
BIAS_TABLE = 2048
FAR_DIST = 897
N_FORCED = 3
VMEM_LIMIT = 56 * 1024 * 1024


def _gelu(x):
    return 0.5 * x * (1.0 + lax.erf(x * SQRT_HALF))


def _dot(a, b):
    return jnp.dot(a.astype(MXU_DTYPE), b.astype(MXU_DTYPE), preferred_element_type=F32)


def _dot_nt(a, b):
    return lax.dot_general(a.astype(MXU_DTYPE), b.astype(MXU_DTYPE),
                           (((1,), (1,)), ((), ())), preferred_element_type=F32)


def _rms(x, g):
    ms = jnp.mean(x * x, axis=-1, keepdims=True)
    return x * lax.rsqrt(ms + EPS) * g


def _params(sem):
    return pltpu.CompilerParams(dimension_semantics=sem, vmem_limit_bytes=VMEM_LIMIT)


def _attn_proj_kernel(x_ref, ng_ref, w_ref, qg_ref, kg_ref,
                      q_out, kcv_out, ks_out, vs_out, kw_out, vw_out, gate_out):
    h = _rms(x_ref[0], ng_ref[...]).astype(MXU_DTYPE)
    tm = h.shape[0]

    pq = _dot(h, w_ref[:, 0:D_NSA])
    qg = qg_ref[...]
    for hh in range(NSA_HEADS):
        sl = slice(HEAD_DIM * hh, HEAD_DIM * (hh + 1))
        q_out[0, :, sl] = (_rms(pq[:, sl], qg) * (HEAD_DIM ** -0.5)).astype(q_out.dtype)

    kcv_out[0] = _dot(h, w_ref[:, D_NSA:D_NSA + 2 * D_KV]).astype(kcv_out.dtype)

    base = D_NSA + 2 * D_KV
    p4 = _dot(h, w_ref[:, base:base + 4 * D_KV])
    ones_col = (lax.broadcasted_iota(jnp.int32, (tm, HEAD_DIM), 1) == 0).astype(F32)
    for g in range(NSA_KV):
        sl = slice(HEAD_DIM * g, HEAD_DIM * (g + 1))
        ks_out[0, g] = _rms(p4[:, 0 * D_KV:1 * D_KV][:, sl], kg_ref[1:2, :]).astype(ks_out.dtype)
        kw_out[0, g] = _rms(p4[:, 2 * D_KV:3 * D_KV][:, sl], kg_ref[2:3, :]).astype(kw_out.dtype)
        vs_out[0, g, :, 0:HEAD_DIM] = p4[:, 1 * D_KV:2 * D_KV][:, sl].astype(vs_out.dtype)
        vs_out[0, g, :, HEAD_DIM:2 * HEAD_DIM] = ones_col.astype(vs_out.dtype)
        vw_out[0, g, :, 0:HEAD_DIM] = p4[:, 3 * D_KV:4 * D_KV][:, sl].astype(vw_out.dtype)
        vw_out[0, g, :, HEAD_DIM:2 * HEAD_DIM] = ones_col.astype(vw_out.dtype)

    base = base + 4 * D_KV
    gate_out[0] = jax.nn.sigmoid(_dot(h, w_ref[:, base:base + NSA_KV * LANES]))


def _attn_proj(x, ng, w, qg, kg, tm):
    b_, t_, d_ = x.shape
    nw = w.shape[1]
    kv4 = lambda wd: pl.BlockSpec((1, NSA_KV, tm, wd), lambda b, i: (b, 0, i, 0))
    return pl.pallas_call(
        _attn_proj_kernel,
        grid=(b_, t_ // tm),
        in_specs=[
            pl.BlockSpec((1, tm, d_), lambda b, i: (b, i, 0)),
            pl.BlockSpec((1, d_), lambda b, i: (0, 0)),
            pl.BlockSpec((d_, nw), lambda b, i: (0, 0), pipeline_mode=pl.Buffered(1)),
            pl.BlockSpec((1, HEAD_DIM), lambda b, i: (0, 0)),
            pl.BlockSpec((3, HEAD_DIM), lambda b, i: (0, 0)),
        ],
        out_specs=[
            pl.BlockSpec((1, tm, D_NSA), lambda b, i: (b, i, 0)),
            pl.BlockSpec((1, tm, 2 * D_KV), lambda b, i: (b, i, 0)),
            kv4(HEAD_DIM), kv4(2 * HEAD_DIM), kv4(HEAD_DIM), kv4(2 * HEAD_DIM),
            pl.BlockSpec((1, tm, NSA_KV * LANES), lambda b, i: (b, i, 0)),
        ],
        out_shape=[
            jax.ShapeDtypeStruct((b_, t_, D_NSA), MXU_DTYPE),
            jax.ShapeDtypeStruct((b_, t_, 2 * D_KV), MXU_DTYPE),
            jax.ShapeDtypeStruct((b_, NSA_KV, t_, HEAD_DIM), MXU_DTYPE),
            jax.ShapeDtypeStruct((b_, NSA_KV, t_, 2 * HEAD_DIM), MXU_DTYPE),
            jax.ShapeDtypeStruct((b_, NSA_KV, t_, HEAD_DIM), MXU_DTYPE),
            jax.ShapeDtypeStruct((b_, NSA_KV, t_, 2 * HEAD_DIM), MXU_DTYPE),
            jax.ShapeDtypeStruct((b_, t_, NSA_KV * LANES), F32),
        ],
        compiler_params=_params(("parallel", "parallel")),
        name="attn_proj",
    )(x, ng, w, qg, kg)


def _sgu_proj_kernel(x_ref, ng_ref, w_ref, lng_ref, lnb_ref, gu_out, vn_out):
    h = _rms(x_ref[0], ng_ref[...]).astype(MXU_DTYPE)
    gu_out[0] = _gelu(_dot(h, w_ref[:, 0:D_SG])).astype(gu_out.dtype)
    gv = _gelu(_dot(h, w_ref[:, D_SG:2 * D_SG]))
    for hh in range(SG_HEADS):
        sl = slice(SG_DIM * hh, SG_DIM * (hh + 1))
        v = gv[:, sl]
        mu = jnp.mean(v, axis=-1, keepdims=True)
        var = jnp.mean(jnp.square(v - mu), axis=-1, keepdims=True)
        y = (v - mu) * lax.rsqrt(var + EPS) * lng_ref[hh:hh + 1, :] + lnb_ref[hh:hh + 1, :]
        vn_out[0, :, sl] = y.astype(vn_out.dtype)


def _sgu_proj(x, ng, w, lng, lnb, tm):
    b_, t_, d_ = x.shape
    return pl.pallas_call(
        _sgu_proj_kernel,
        grid=(b_, t_ // tm),
        in_specs=[
            pl.BlockSpec((1, tm, d_), lambda b, i: (b, i, 0)),
            pl.BlockSpec((1, d_), lambda b, i: (0, 0)),
            pl.BlockSpec((d_, 2 * D_SG), lambda b, i: (0, 0), pipeline_mode=pl.Buffered(1)),
            pl.BlockSpec((SG_HEADS, SG_DIM), lambda b, i: (0, 0)),
            pl.BlockSpec((SG_HEADS, SG_DIM), lambda b, i: (0, 0)),
        ],
        out_specs=[
            pl.BlockSpec((1, tm, D_SG), lambda b, i: (b, i, 0)),
            pl.BlockSpec((1, tm, D_SG), lambda b, i: (b, i, 0)),
        ],
        out_shape=[
            jax.ShapeDtypeStruct((b_, t_, D_SG), MXU_DTYPE),
            jax.ShapeDtypeStruct((b_, t_, D_SG), MXU_DTYPE),
        ],
        compiler_params=_params(("parallel", "parallel")),
        name="sgu_proj",
    )(x, ng, w, lng, lnb)


def _compress_kernel(x_ref, w1_ref, w2_ref, pe_ref, kg_ref, out_ref):
    which = pl.program_id(1)
    x = x_ref[0, 0, 0]
    nch = x.shape[0]
    half = CMP_STRIDE * HEAD_DIM
    a = _dot(x, w1_ref[0, 0, 0:half, :])
    bm = _dot(x, w1_ref[0, 0, half:2 * half, :])
    pec = _dot(pe_ref[0, 0], w1_ref[0, 0])
    hid = _gelu(a + pltpu.roll(bm, shift=nch - 1, axis=0) + pec)
    y = _dot(hid, w2_ref[0, 0])
    y = jnp.where(which == 0, _rms(y, kg_ref[0:1, :]), y)
    out_ref[0, 0, 0, 0:CMP_PAD, :] = jnp.zeros((CMP_PAD, HEAD_DIM), out_ref.dtype)
    out_ref[0, 0, 0, CMP_PAD:CMP_PAD + nch, :] = y.astype(out_ref.dtype)


def _compress(xc, w1, w2, pe, kg):
    b_, _, g_, nch, fl = xc.shape
    return pl.pallas_call(
        _compress_kernel,
        grid=(b_, 2, g_),
        in_specs=[
            pl.BlockSpec((1, 1, 1, nch, fl), lambda b, w, g: (b, w, g, 0, 0)),
            pl.BlockSpec((1, 1, 2 * fl, CMP_HIDDEN), lambda b, w, g: (w, g, 0, 0)),
            pl.BlockSpec((1, 1, CMP_HIDDEN, HEAD_DIM), lambda b, w, g: (w, g, 0, 0)),
            pl.BlockSpec((1, 1, 1, 2 * fl), lambda b, w, g: (w, g, 0, 0)),
            pl.BlockSpec((3, HEAD_DIM), lambda b, w, g: (0, 0)),
        ],
        out_specs=pl.BlockSpec((1, 1, 1, CMP_PAD + nch, HEAD_DIM), lambda b, w, g: (b, w, g, 0, 0)),
        out_shape=jax.ShapeDtypeStruct((b_, 2, g_, CMP_PAD + nch, HEAD_DIM), F32),
        compiler_params=_params(("parallel", "parallel", "parallel")),
        name="nsa_compress",
    )(xc, w1, w2, pe, kg)


def _nsa_kernel(q_ref, gate_ref, kc_ref, vc_ref, ks_ref, vs_ref, kw_ref, vw_ref,
                tbw_ref, wb_ref, cb_ref, cfar_ref, ov_ref, o_ref):
    i = pl.program_id(2)
    qb = Q_BLOCK
    rows = NSA_REP * qb
    ncmp = CMP_PAD

    qblk = q_ref[0]
    q_all = jnp.concatenate(
        [qblk[:, HEAD_DIM * r:HEAD_DIM * (r + 1)] for r in range(NSA_REP)], axis=0)
    cfar = [cfar_ref[0, r][:, 0:1] for r in range(NSA_REP)]

    cstart = pl.multiple_of(8 * i + 8, 8)
    kc = kc_ref[0, 0, 0, pl.ds(cstart, ncmp), :]
    vc = vc_ref[0, 0, 0, pl.ds(cstart, ncmp), :]
    s_c = _dot_nt(q_all, kc)
    r_io = lax.broadcasted_iota(jnp.int32, (qb, ncmp), 0)
    c_io = lax.broadcasted_iota(jnp.int32, (qb, ncmp), 1)
    dist_c = r_io - CMP_STRIDE * c_io + (CMP_STRIDE * ncmp - qb - (CMP_BLOCK - 1))
    vis_c = (dist_c >= 0) & (c_io >= ncmp - 8 - 8 * i)
    vis_cf = vis_c.astype(F32)
    o_c = []
    psum = jnp.zeros((qb, ncmp), F32)
    for r in range(NSA_REP):
        bias = jnp.concatenate(
            [jnp.broadcast_to(cfar[r], (qb, ncmp - LANES)), cb_ref[0, r]], axis=1)
        s = jnp.where(vis_c, s_c[qb * r:qb * (r + 1)] + bias, NEG)
        m = jnp.max(s, axis=-1, keepdims=True)
        p = jnp.exp(s - m) * vis_cf
        l = jnp.sum(p, axis=-1, keepdims=True)
        p = p * jnp.where(l > 0.0, 1.0 / l, 0.0)
        o_c.append(_dot(p, vc))
        psum = psum + p
    p_hi = psum.astype(BF16)
    p_lo = (psum - p_hi.astype(F32)).astype(BF16)
    ov = ov_ref[...]
    imp = (jnp.dot(p_hi, ov, preferred_element_type=F32)
           + jnp.dot(p_lo, ov, preferred_element_type=F32))

    nsel = LANES
    sp = lax.broadcasted_iota(jnp.int32, (qb, nsel), 1)
    rr = lax.broadcasted_iota(jnp.int32, (qb, nsel), 0)
    cur = (nsel - 2) + (rr >= SLC_BLOCK).astype(jnp.int32)
    s_abs = sp + (2 * i + 2 - nsel)
    valid = s_abs >= 0
    forced = ((s_abs == 0) | (sp == cur) | (sp == cur - 1)) & valid
    excluded = forced | (sp > cur) | (~valid)
    cand_t = jnp.where(excluded, NEG, imp).T
    v = cand_t
    tau = None
    for _ in range(SLC_TOPN - N_FORCED):
        tau = jnp.max(v, axis=0, keepdims=True)
        v = jnp.where(v >= tau, 3.0 * NEG, v)
    picked_t = ((cand_t >= tau) & (cand_t > 0.5 * NEG)).astype(F32)
    selmask = jnp.maximum(picked_t.T, forced.astype(F32)).astype(BF16)

    kt = SEL_TILE
    spk = SLC_BLOCK
    e_row = lax.broadcasted_iota(jnp.int32, (nsel, kt), 0)
    e_col = lax.broadcasted_iota(jnp.int32, (nsel, kt), 1) // spk
    t_r = lax.broadcasted_iota(jnp.int32, (qb, kt), 0)
    t_c = lax.broadcasted_iota(jnp.int32, (qb, kt), 1)

    def sel_tile(dd, carry, near):
        m_old, acc = carry
        start = pl.multiple_of(KEY_PAD + qb * (i + 1) - kt * (dd + 1), qb)
        kk = ks_ref[0, 0, pl.ds(start, kt), :]
        vv = vs_ref[0, 0, pl.ds(start, kt), :]
        s = _dot_nt(q_all, kk)
        expand = (e_row == e_col + (nsel - (kt // spk) * (dd + 1))).astype(BF16)
        msk = jnp.dot(selmask, expand, preferred_element_type=F32) > 0.5
        msk = msk & (t_r - t_c + (kt * (dd + 1) - qb) >= 0)
        parts = []
        for r in range(NSA_REP):
            bias = tbw_ref[0, r, dd] if near else cfar[r]
            parts.append(jnp.where(msk, s[qb * r:qb * (r + 1)] + bias, NEG))
        s = jnp.concatenate(parts, axis=0)
        m_new = jnp.maximum(m_old, jnp.max(s, axis=-1, keepdims=True))
        alpha = jnp.exp(m_old - m_new)
        p = jnp.exp(s - m_new)
        acc = alpha * acc + _dot(p, vv)
        return m_new, acc

    n_tiles = (i * qb + qb + kt - 1) // kt
    carry = (jnp.full((rows, 1), NEG, F32), jnp.zeros((rows, 2 * HEAD_DIM), F32))
    carry = lax.fori_loop(0, jnp.minimum(n_tiles, NEAR_TILES),
                          functools.partial(sel_tile, near=True), carry)
    carry = lax.fori_loop(NEAR_TILES, n_tiles, functools.partial(sel_tile, near=False), carry)
    acc_s = carry[1]

    wstart = pl.multiple_of(KEY_PAD + qb * (i + 1) - WIN_KEYS, qb)
    kk = kw_ref[0, 0, pl.ds(wstart, WIN_KEYS), :]
    vv = vw_ref[0, 0, pl.ds(wstart, WIN_KEYS), :]
    s_w = _dot_nt(q_all, kk)
    w_r = lax.broadcasted_iota(jnp.int32, (qb, WIN_KEYS), 0)
    w_c = lax.broadcasted_iota(jnp.int32, (qb, WIN_KEYS), 1)
    dist_w = w_r - w_c + WINDOW
    vis_w = (dist_w >= 0) & (dist_w < WINDOW) & (w_c >= WIN_KEYS - qb * (i + 1))
    parts = []
    for r in range(NSA_REP):
        parts.append(jnp.where(vis_w, s_w[qb * r:qb * (r + 1)] + wb_ref[0, r], NEG))
    s = jnp.concatenate(parts, axis=0)
    p = jnp.exp(s - jnp.max(s, axis=-1, keepdims=True))
    acc_w = _dot(p, vv)

    gt = gate_ref[0]
    for r in range(NSA_REP):
        rs = slice(qb * r, qb * (r + 1))
        o_s = acc_s[rs, 0:HEAD_DIM] / acc_s[rs, HEAD_DIM:HEAD_DIM + 1]
        o_w = acc_w[rs, 0:HEAD_DIM] / acc_w[rs, HEAD_DIM:HEAD_DIM + 1]
        o = (gt[:, 3 * r + 0:3 * r + 1] * o_c[r] + gt[:, 3 * r + 1:3 * r + 2] * o_s
             + gt[:, 3 * r + 2:3 * r + 3] * o_w)
        o_ref[0, :, HEAD_DIM * r:HEAD_DIM * (r + 1)] = o.astype(o_ref.dtype)


def _nsa_attention(qn, gate, kcv, ksp, vsp, kwp, vwp, tbw, wb, cb, cfar, ov):
    b_, t_, _ = qn.shape
    tp = ksp.shape[2]
    ncp = kcv.shape[3]
    gw = NSA_REP * HEAD_DIM
    kvspec = lambda wd: pl.BlockSpec((1, 1, tp, wd), lambda b, g, i: (b, g, 0, 0))
    return pl.pallas_call(
        _nsa_kernel,
        grid=(b_, NSA_KV, t_ // Q_BLOCK),
        in_specs=[
            pl.BlockSpec((1, Q_BLOCK, gw), lambda b, g, i: (b, i, g)),
            pl.BlockSpec((1, Q_BLOCK, LANES), lambda b, g, i: (b, i, g)),
            pl.BlockSpec((1, 1, 1, ncp, HEAD_DIM), lambda b, g, i: (b, 0, g, 0, 0)),
            pl.BlockSpec((1, 1, 1, ncp, HEAD_DIM), lambda b, g, i: (b, 1, g, 0, 0)),
            kvspec(HEAD_DIM), kvspec(2 * HEAD_DIM), kvspec(HEAD_DIM), kvspec(2 * HEAD_DIM),
            pl.BlockSpec((1, NSA_REP, NEAR_TILES, Q_BLOCK, SEL_TILE), lambda b, g, i: (g, 0, 0, 0, 0)),
            pl.BlockSpec((1, NSA_REP, Q_BLOCK, WIN_KEYS), lambda b, g, i: (g, 0, 0, 0)),
            pl.BlockSpec((1, NSA_REP, Q_BLOCK, LANES), lambda b, g, i: (g, 0, 0, 0)),
            pl.BlockSpec((1, NSA_REP, 1, LANES), lambda b, g, i: (g, 0, 0, 0)),
            pl.BlockSpec((CMP_PAD, LANES), lambda b, g, i: (0, 0)),
        ],
        out_specs=pl.BlockSpec((1, Q_BLOCK, gw), lambda b, g, i: (b, i, g)),
        out_shape=jax.ShapeDtypeStruct((b_, t_, D_NSA), F32),
        compiler_params=_params(("parallel", "parallel", "arbitrary")),
        name="nsa_attention",
    )(qn, gate, kcv, kcv, ksp, vsp, kwp, vwp, tbw, wb, cb, cfar, ov)


def _mix_kernel(x_ref, on_ref, gu_ref, vn_ref, sgw_ref, sgb_ref, gn_ref, gs_ref, wo_ref,
                fg_ref, wq_ref, x1_out, h2t_out, qp_out):
    tm = x_ref.shape[0]
    tri = (lax.broadcasted_iota(jnp.int32, (SG_CHUNK, SG_CHUNK), 0)
           >= lax.broadcasted_iota(jnp.int32, (SG_CHUNK, SG_CHUNK), 1))
    sgb = sgb_ref[...]
    y = x_ref[...] + _dot(_rms(on_ref[...], gn_ref[...]), wo_ref[0:D_NSA, :])
    parts = []
    for c in range(tm // SG_CHUNK):
        cs = slice(SG_CHUNK * c, SG_CHUNK * (c + 1))
        heads = []
        for hh in range(SG_HEADS):
            sl = slice(SG_DIM * hh, SG_DIM * (hh + 1))
            w = jnp.where(tri, sgw_ref[hh], 0.0)
            mixed = _dot(w, vn_ref[cs, sl]) + sgb[:, hh:hh + 1]
            heads.append(gu_ref[cs, sl].astype(F32) * mixed)
        parts.append(jnp.concatenate(heads, axis=1))
    o_sg = jnp.concatenate(parts, axis=0)
    y = y + _dot(_rms(o_sg, gs_ref[...]), wo_ref[D_NSA:D_NSA + D_SG, :])
    x1_out[...] = y
    h2 = _rms(y, fg_ref[...])
    h2t_out[...] = h2.T.astype(h2t_out.dtype)
    qp_out[...] = _dot(h2, wq_ref[...])


def _mix(x2, onsa, gu, vn, sgw, sgb_t, gn, gs, wo, fg, wq, tm):
    n_, d_ = x2.shape
    row = lambda wd: pl.BlockSpec((tm, wd), lambda i: (i, 0))
    const = lambda shape, **kw: pl.BlockSpec(shape, lambda i: (0,) * len(shape), **kw)
    return pl.pallas_call(
        _mix_kernel,
        grid=(n_ // tm,),
        in_specs=[
            row(d_), row(D_NSA), row(D_SG), row(D_SG),
            const((SG_HEADS, SG_CHUNK, SG_CHUNK)), const((SG_CHUNK, SG_HEADS)),
            const((1, D_NSA)), const((1, D_SG)),
            const((D_NSA + D_SG, d_), pipeline_mode=pl.Buffered(1)),
            const((1, d_)),
            const((d_, wq.shape[1]), pipeline_mode=pl.Buffered(1)),
        ],
        out_specs=[row(d_), pl.BlockSpec((d_, tm), lambda i: (0, i)), row(wq.shape[1])],
        out_shape=[
            jax.ShapeDtypeStruct((n_, d_), F32),
            jax.ShapeDtypeStruct((d_, n_), MXU_DTYPE),
            jax.ShapeDtypeStruct((n_, wq.shape[1]), F32),
        ],
        compiler_params=_params(("parallel",)),
        name="mix_out_proj",
    )(x2, onsa, gu, vn, sgw, sgb_t, gn, gs, wo, fg, wq)


def _top_rows(v, k):
    tops = []
    for _ in range(k):
        m = jnp.max(v, axis=0, keepdims=True)
        tops.append(m)
        v = jnp.where(v >= m, NEG, v)
    return jnp.concatenate(tops, axis=0)


def _peer_topk_kernel(qp_ref, sk_ref, s1_out, s2_out, st_out):
    for hh in range(PEER_HEADS):
        base = 2 * PEER_HALF * hh
        s1 = _dot_nt(sk_ref[0], qp_ref[:, base:base + PEER_HALF])
        s2 = _dot_nt(sk_ref[1], qp_ref[:, base + PEER_HALF:base + 2 * PEER_HALF])
        s1_out[hh] = s1
        for j in range(s2.shape[1] // LANES):
            s2_out[hh, j] = s2[:, LANES * j:LANES * (j + 1)]
        a = _top_rows(s1, PEER_TOPK)
        b = _top_rows(s2, PEER_TOPK)
        cand = [a[0:1] + b]
        for ra in range(1, 8):
            cand.append(a[ra:ra + 1] + b[0:8])
        cand.append(a[8:16] + b[0:1])
        top = _top_rows(jnp.concatenate(cand, axis=0), PEER_TOPK)
        z = jnp.sum(jnp.exp(top - top[0:1]), axis=0, keepdims=True)
        st_out[0, hh:hh + 1, :] = top[PEER_TOPK - 1:PEER_TOPK]
        st_out[1, hh:hh + 1, :] = b[0:1]
        st_out[2, hh:hh + 1, :] = a[0:1] + jnp.log(z)


def _peer_topk(qp, sub_keys, tm):
    n_, qd = qp.shape
    return pl.pallas_call(
        _peer_topk_kernel,
        grid=(n_ // tm,),
        in_specs=[
            pl.BlockSpec((tm, qd), lambda i: (i, 0)),
            pl.BlockSpec((2, PEER_NKEYS, PEER_HALF), lambda i: (0, 0, 0)),
        ],
        out_specs=[
            pl.BlockSpec((PEER_HEADS, PEER_NKEYS, tm), lambda i: (0, 0, i)),
            pl.BlockSpec((PEER_HEADS, tm // LANES, PEER_NKEYS, LANES), lambda i: (0, i, 0, 0)),
            pl.BlockSpec((3, PEER_HEADS,tm), lambda i: (0, 0, i)),
        ],
        out_shape=[
            jax.ShapeDtypeStruct((PEER_HEADS, PEER_NKEYS, n_), F32),
            jax.ShapeDtypeStruct((PEER_HEADS, n_ // LANES, PEER_NKEYS, LANES), F32),
            jax.ShapeDtypeStruct((3, PEER_HEADS,n_), F32),
        ],
        compiler_params=_params(("parallel",)),
        name="peer_topk",
    )(qp, sub_keys)


def _peer_mix_kernel(ht_ref, u_ref, vt_ref, s1_ref, s2_ref, st_ref, x1_ref, out_ref,
                     acc_ref, e2_ref, crow_ref, a0_ref, a1_ref, z0_ref, z1_ref):
    ie = pl.program_id(1)
    te = u_ref.shape[0]
    tm = ht_ref.shape[1]
    n_tiles = pl.num_programs(1) - 2
    per_tile = te // PEER_NKEYS

    @pl.when(ie == 0)
    def _():
        acc_ref[...] = jnp.zeros_like(acc_ref)
        a1_ref[...] = jnp.zeros_like(a1_ref)
        z0_ref[...] = jnp.zeros_like(z0_ref)
        for hh in range(PEER_HEADS):
            for j in range(tm // LANES):
                e2_ref[hh, j] = jnp.exp(s2_ref[hh, j] - st_ref[1, hh:hh + 1, LANES * j:LANES * (j + 1)])

    d_ = vt_ref.shape[0]
    n_lane = tm // LANES
    n_blocks = per_tile * n_lane
    a_units = [(r, c, kc) for r in range(te // MXU_ROWS) for c in range(tm // MXU_ROWS)
               for kc in range(d_ // A_KCHUNK)]
    c_units = [(r, c) for r in range(d_ // MXU_ROWS) for c in range(tm // MXU_ROWS)]

    def step(a_w, a_r, z_w, z_r):
        tile = jnp.clip(ie - 1, 0, n_tiles - 1)

        def a_unit(r, c, kc):
            rs = slice(MXU_ROWS * r, MXU_ROWS * (r + 1))
            cs = slice(MXU_ROWS * c, MXU_ROWS * (c + 1))
            ds = slice(A_KCHUNK * kc, A_KCHUNK * (kc + 1))
            res = jnp.dot(u_ref[rs, ds], ht_ref[ds, cs], preferred_element_type=F32)
            for jj in range(MXU_ROWS // LANES):
                j = c * (MXU_ROWS // LANES) + jj
                part = res[:, LANES * jj:LANES * (jj + 1)]
                if kc == 0:
                    a_w[j, rs, :] = part
                else:
                    a_w[j, rs, :] += part

        def c_unit(r, c):
            rs = slice(MXU_ROWS * r, MXU_ROWS * (r + 1))
            cs = slice(MXU_ROWS * c, MXU_ROWS * (c + 1))
            acc_ref[rs, cs] += jnp.dot(vt_ref[rs, :], z_r[:, cs], preferred_element_type=F32)

        def b_block(k, j):
            ls = slice(LANES * j, LANES * (j + 1))
            for part in range(PEER_NKEYS // GATE_ROWS):
                ks = slice(GATE_ROWS * part, GATE_ROWS * (part + 1))
                rs = slice(PEER_NKEYS * k + GATE_ROWS * part, PEER_NKEYS * k + GATE_ROWS * (part + 1))
                g = jnp.zeros((GATE_ROWS, LANES), F32)
                for hh in range(PEER_HEADS):
                    row = PEER_HEADS * k + hh
                    hit = (s2_ref[hh, j, ks, :] + crow_ref[0, row:row + 1, ls]) >= st_ref[0, hh:hh + 1, ls]
                    g = g + jnp.where(hit, e2_ref[hh, j, ks, :] * crow_ref[1, row:row + 1, ls], 0.0)
                z_w[rs, ls] = (_gelu(a_r[j, rs, :]) * g).astype(z_w.dtype)

        for k in range(per_tile):
            i1 = tile * per_tile + k
            hs = slice(PEER_HEADS * k, PEER_HEADS * (k + 1))
            for hh in range(PEER_HEADS):
                row = PEER_HEADS * k + hh
                crow_ref[0, row:row + 1, :] = s1_ref[hh, pl.ds(i1, 1), :]
            crow_ref[1, hs, :] = jnp.exp(crow_ref[0, hs, :] - st_ref[2])

        a_iter = iter(a_units)
        c_iter = iter(c_units)
        a_every = max(n_blocks // len(a_units), 1)
        c_per = -(-len(c_units) // n_blocks)
        for blk in range(n_blocks):
            k, j = divmod(blk, n_lane)
            if blk % a_every == 0:
                unit = next(a_iter, None)
                if unit is not None:
                    a_unit(*unit)
            for _ in range(c_per):
                unit = next(c_iter, None)
                if unit is not None:
                    c_unit(*unit)
            b_block(k, j)
        for unit in a_iter:
            a_unit(*unit)
        for unit in c_iter:
            c_unit(*unit)

    @pl.when(ie % 2 == 0)
    def _():
        step(a0_ref, a1_ref, z1_ref, z0_ref)

    @pl.when(ie % 2 == 1)
    def _():
        step(a1_ref, a0_ref, z0_ref, z1_ref)

    @pl.when(ie == pl.num_programs(1) - 1)
    def _():
        out_ref[...] = x1_ref[...] + acc_ref[...].T


def _peer_mix(h2t, u, vt, s1t, s2t, stats, x1, tm, te):
    d_, n_ = h2t.shape
    n_tiles = u.shape[0] // te
    return pl.pallas_call(
        _peer_mix_kernel,
        grid=(n_ // tm, n_tiles + 2),
        in_specs=[
            pl.BlockSpec((d_, tm), lambda it, ie: (0, it)),
            pl.BlockSpec((te, d_), lambda it, ie: (jnp.minimum(ie, n_tiles - 1), 0)),
            pl.BlockSpec((d_, te), lambda it, ie: (0, jnp.maximum(ie - 2, 0))),
            pl.BlockSpec((PEER_HEADS, PEER_NKEYS, tm), lambda it, ie: (0, 0, it)),
            pl.BlockSpec((PEER_HEADS, tm // LANES, PEER_NKEYS, LANES), lambda it, ie: (0, it, 0, 0)),
            pl.BlockSpec((3, PEER_HEADS,tm), lambda it, ie: (0, 0, it)),
            pl.BlockSpec((tm, d_), lambda it, ie: (it, 0)),
        ],
        out_specs=pl.BlockSpec((tm, d_), lambda it, ie: (it, 0)),
        out_shape=jax.ShapeDtypeStruct((n_, d_), F32),
        scratch_shapes=[
            pltpu.VMEM((d_, tm), F32),
            pltpu.VMEM((PEER_HEADS, tm // LANES, PEER_NKEYS, LANES), F32),
            pltpu.VMEM((2, PEER_HEADS * (te // PEER_NKEYS), tm), F32),
            pltpu.VMEM((tm // LANES, te, LANES), F32), pltpu.VMEM((tm // LANES, te, LANES), F32),
            pltpu.VMEM((te, tm), MXU_DTYPE), pltpu.VMEM((te, tm), MXU_DTYPE),
        ],
        compiler_params=_params(("parallel", "arbitrary")),
        name="peer_mix",
    )(h2t, u, vt, s1t, s2t, stats, x1)


def _rel_bucket(dist):
    n = jnp.maximum(dist, 0)
    max_exact = REL_BUCKETS // 2
    nf = jnp.maximum(n, 1).astype(F32)
    large = max_exact + (jnp.log(nf / max_exact) / math.log(REL_MAX_DIST / max_exact)
                         * (REL_BUCKETS - max_exact)).astype(jnp.int32)
    large = jnp.minimum(large, REL_BUCKETS - 1)
    return jnp.where(n < max_exact, n, large)


def _toeplitz(btab, off, rows, cols):
    per = rows + cols
    k = np.concatenate([np.arange(cols), np.zeros(1, np.int64), np.arange(-(rows - 1), 0)])
    idx = np.clip(off - k, 0, btab.shape[1] - 1)
    w = jnp.take(btab, jnp.asarray(idx, jnp.int32), axis=1)
    x = jnp.tile(w, (1, rows))[:, :rows * (per - 1)]
    return x.reshape(btab.shape[0], rows, per - 1)[:, :, :cols]


def _bias_tiles(rel_bias):
    dist = jnp.arange(BIAS_TABLE, dtype=jnp.int32)
    btab = rel_bias.astype(F32)[_rel_bucket(dist)].T
    tbw = jnp.stack([_toeplitz(btab, SEL_TILE * (dd + 1) - Q_BLOCK, Q_BLOCK, SEL_TILE)
                     for dd in range(NEAR_TILES)], axis=1)
    wb = _toeplitz(btab, WINDOW, Q_BLOCK, WIN_KEYS)
    off = CMP_STRIDE * (LANES - 8) - (CMP_BLOCK - 1)
    front = CMP_STRIDE * LANES - off
    length = CMP_STRIDE * (LANES + 8 + 2)
    padded = jnp.pad(btab, ((0, 0), (front, length - front - BIAS_TABLE)))
    ch = padded.reshape(btab.shape[0], length // CMP_STRIDE, CMP_STRIDE)
    nwin = LANES + 1
    win = jnp.stack([ch[:, a:a + nwin] for a in range(Q_BLOCK // CMP_STRIDE)], axis=2)
    win = win.reshape(btab.shape[0], nwin, Q_BLOCK)
    cb = jnp.swapaxes(win[:, 1:nwin][:, ::-1], 1, 2)
    cfar = rel_bias.astype(F32)[REL_BUCKETS - 1]
    grp = lambda a: a.reshape((NSA_KV, NSA_REP) + a.shape[1:])
    cfar = jnp.broadcast_to(cfar[:, None, None], (NSA_HEADS, 1, LANES))
    return grp(tbw), grp(wb), grp(cb), grp(cfar)


def _overlap_matrix():
    c = np.arange(CMP_PAD)[:, None]
    s = np.arange(LANES)[None, :]
    per = SLC_BLOCK // CMP_STRIDE
    ov = (s == c // per) | ((c % per == per - 1) & (s == c // per + 1))
    return jnp.asarray(ov.astype(np.float32), BF16)


def kernel(x, attn_norm_g, w_in, q_norm_g, k_norm_g, cmp_pe_k, cmp_w1_k, cmp_w2_k, cmp_pe_v,
           cmp_w1_v, cmp_w2_v, rel_bias, sg_ln_g, sg_ln_b, sg_w, sg_b, out_norm_nsa, out_norm_sg,
           w_out, ffn_norm_g, peer_w_query, peer_sub_keys, peer_u, peer_v):
    b_, t_, d_ = x.shape
    n_ = b_ * t_
    assert t_ % Q_BLOCK == 0 and t_ >= WIN_KEYS and d_ == D_NSA + D_SG

    c_kv = D_NSA + 6 * D_KV
    c_gate = c_kv + 3 * NSA_HEADS
    wg = w_in[:, c_kv:c_gate].reshape(d_, NSA_KV, 3 * NSA_REP)
    wg = jnp.pad(wg, ((0, 0), (0, 0), (0, LANES - 3 * NSA_REP))).reshape(d_, NSA_KV * LANES)
    w_attn = jnp.concatenate([w_in[:, :c_kv], wg], axis=1).astype(MXU_DTYPE)
    w_sgu = w_in[:, c_gate:].astype(MXU_DTYPE)
    ng = attn_norm_g.reshape(1, d_)

    qn, kcv, ks, vs, kw, vw, gate = _attn_proj(x, ng, w_attn, q_norm_g.reshape(1, HEAD_DIM),
                                               k_norm_g, tm=256)
    gu, vn = _sgu_proj(x, ng, w_sgu, sg_ln_g, sg_ln_b, tm=256)

    nch = t_ // CMP_STRIDE
    xc = kcv.reshape(b_, nch, CMP_STRIDE, 2, NSA_KV, HEAD_DIM).transpose(0, 3, 4, 1, 2, 5)
    xc = xc.reshape(b_, 2, NSA_KV, nch, CMP_STRIDE * HEAD_DIM)
    w1 = jnp.stack([cmp_w1_k, cmp_w1_v]).astype(MXU_DTYPE)
    w2 = jnp.stack([cmp_w2_k, cmp_w2_v]).astype(MXU_DTYPE)
    pe = jnp.stack([cmp_pe_k, cmp_pe_v]).transpose(0, 2, 1, 3).reshape(2, NSA_KV, 1, CMP_BLOCK * HEAD_DIM)
    kcv_c = _compress(xc, w1, w2, pe.astype(MXU_DTYPE), k_norm_g)

    padk = lambda a: jnp.pad(a, ((0, 0), (0, 0), (KEY_PAD, 0), (0, 0)))
    tbw, wb, cb, cfar = _bias_tiles(rel_bias)
    onsa = _nsa_attention(qn, gate, kcv_c, padk(ks), padk(vs), padk(kw), padk(vw),
                          tbw, wb, cb, cfar, _overlap_matrix())

    x1, h2t, qp = _mix(x.reshape(n_, d_), onsa.reshape(n_, D_NSA), gu.reshape(n_, D_SG),
                      vn.reshape(n_, D_SG), sg_w, sg_b.T, out_norm_nsa.reshape(1, D_NSA),
                      out_norm_sg.reshape(1, D_SG), w_out.astype(MXU_DTYPE),
                      ffn_norm_g.reshape(1, d_), peer_w_query.astype(MXU_DTYPE), tm=256)

    s1t, s2t, stats = _peer_topk(qp, peer_sub_keys, tm=256)
    out = _peer_mix(h2t, peer_u.astype(MXU_DTYPE), peer_v.T.astype(MXU_DTYPE), s1t, s2t, stats, x1,
                    tm=min(512, n_), te=512)
    return out.reshape(b_, t_, d_)
```

```python
import functools
import math

import jax
import jax.numpy as jnp
import numpy as np
from jax import lax
from jax.experimental import pallas as pl
from jax.experimental.pallas import tpu as pltpu

F32 = jnp.float32
BF16 = jnp.bfloat16
MXU_DTYPE = BF16

EPS = 1e-6
NEG = -1e30
SQRT_HALF = 0.7071067811865476

NSA_HEADS = 16
NSA_KV = 4
NSA_REP = 4
HEAD_DIM = 64
D_NSA = NSA_HEADS * HEAD_DIM
D_KV = NSA_KV * HEAD_DIM
CMP_STRIDE = 16
CMP_BLOCK = 32
CMP_HIDDEN = 128
SLC_BLOCK = 64
SLC_TOPN = 16
WINDOW = 512
Q_BLOCK = 128
SG_HEADS = 8
SG_DIM = 128
SG_CHUNK = 128
D_SG = SG_HEADS * SG_DIM
REL_BUCKETS = 32
REL_MAX_DIST = 1024
PEER_HEADS = 8
PEER_NKEYS = 128
PEER_HALF = 128
PEER_TOPK = 16

LANES = 128
MXU_ROWS = 256
A_KCHUNK = 1024
GATE_ROWS = 64
KEY_PAD = 1024
CMP_PAD = 512
SEL_TILE = 512
WIN_KEYS = WINDOW + Q_BLOCK
NEAR_TILES = 2
BIAS_TABLE = 2048
FAR_DIST = 897
FAR_HI_LANE = 64
FAR_LO_LANE = 65
PAD_FLAG_LANE = 66
N_FORCED = 3
VMEM_LIMIT = 56 * 1024 * 1024


def _gelu(x):
    return 0.5 * x * (1.0 + lax.erf(x * SQRT_HALF))


def _dot(a, b):
    return jnp.dot(a.astype(MXU_DTYPE), b.astype(MXU_DTYPE), preferred_element_type=F32)


def _dot_nt(a, b):
    return lax.dot_general(a.astype(MXU_DTYPE), b.astype(MXU_DTYPE),
                           (((1,), (1,)), ((), ())), preferred_element_type=F32)


def _rms(x, g):
    ms = jnp.mean(x * x, axis=-1, keepdims=True)
    return x * lax.rsqrt(ms + EPS) * g


def _params(sem):
    return pltpu.CompilerParams(dimension_semantics=sem, vmem_limit_bytes=VMEM_LIMIT)


def _attn_proj_kernel(x_ref, ng_ref, w_ref, qg_ref, kg_ref,
                      q_out, kcv_out, ks_out, vs_out, kw_out, vw_out, gate_out):
    h = _rms(x_ref[0], ng_ref[...]).astype(MXU_DTYPE)
    tm = h.shape[0]

    pq = _dot(h, w_ref[:, 0:D_NSA])
    qg = qg_ref[...]
    for hh in range(NSA_HEADS):
        sl = slice(HEAD_DIM * hh, HEAD_DIM * (hh + 1))
        q_out[0, :, sl] = (_rms(pq[:, sl], qg) * (HEAD_DIM ** -0.5)).astype(q_out.dtype)

    kcv_out[0] = _dot(h, w_ref[:, D_NSA:D_NSA + 2 * D_KV]).astype(kcv_out.dtype)

    base = D_NSA + 2 * D_KV
    p4 = _dot(h, w_ref[:, base:base + 4 * D_KV])
    ones_col = (lax.broadcasted_iota(jnp.int32, (tm, HEAD_DIM), 1) == 0).astype(F32)
    for g in range(NSA_KV):
        sl = slice(HEAD_DIM * g, HEAD_DIM * (g + 1))
        ks_out[0, g] = _rms(p4[:, 0 * D_KV:1 * D_KV][:, sl], kg_ref[1:2, :]).astype(ks_out.dtype)
        kw_out[0, g] = _rms(p4[:, 2 * D_KV:3 * D_KV][:, sl], kg_ref[2:3, :]).astype(kw_out.dtype)
        vs_out[0, g, :, 0:HEAD_DIM] = p4[:, 1 * D_KV:2 * D_KV][:, sl].astype(vs_out.dtype)
        vs_out[0, g, :, HEAD_DIM:2 * HEAD_DIM] = ones_col.astype(vs_out.dtype)
        vw_out[0, g, :, 0:HEAD_DIM] = p4[:, 3 * D_KV:4 * D_KV][:, sl].astype(vw_out.dtype)
        vw_out[0, g, :, HEAD_DIM:2 * HEAD_DIM] = ones_col.astype(vw_out.dtype)

    base = base + 4 * D_KV
    gate_out[0] = jax.nn.sigmoid(_dot(h, w_ref[:, base:base + NSA_KV * LANES]))


def _attn_proj(x, ng, w, qg, kg, tm):
    b_, t_, d_ = x.shape
    nw = w.shape[1]
    kv4 = lambda wd: pl.BlockSpec((1, NSA_KV, tm, wd), lambda b, i: (b, 0, i, 0))
    return pl.pallas_call(
        _attn_proj_kernel,
        grid=(b_, t_ // tm),
        in_specs=[
            pl.BlockSpec((1, tm, d_), lambda b, i: (b, i, 0)),
            pl.BlockSpec((1, d_), lambda b, i: (0, 0)),
            pl.BlockSpec((d_, nw), lambda b, i: (0, 0), pipeline_mode=pl.Buffered(1)),
            pl.BlockSpec((1, HEAD_DIM), lambda b, i: (0, 0)),
            pl.BlockSpec((3, HEAD_DIM), lambda b, i: (0, 0)),
        ],
        out_specs=[
            pl.BlockSpec((1, tm, D_NSA), lambda b, i: (b, i, 0)),
            pl.BlockSpec((1, tm, 2 * D_KV), lambda b, i: (b, i, 0)),
            kv4(HEAD_DIM), kv4(2 * HEAD_DIM), kv4(HEAD_DIM), kv4(2 * HEAD_DIM),
            pl.BlockSpec((1, tm, NSA_KV * LANES), lambda b, i: (b, i, 0)),
        ],
        out_shape=[
            jax.ShapeDtypeStruct((b_, t_, D_NSA), MXU_DTYPE),
            jax.ShapeDtypeStruct((b_, t_, 2 * D_KV), MXU_DTYPE),
            jax.ShapeDtypeStruct((b_, NSA_KV, t_, HEAD_DIM), MXU_DTYPE),
            jax.ShapeDtypeStruct((b_, NSA_KV, t_, 2 * HEAD_DIM), MXU_DTYPE),
            jax.ShapeDtypeStruct((b_, NSA_KV, t_, HEAD_DIM), MXU_DTYPE),
            jax.ShapeDtypeStruct((b_, NSA_KV, t_, 2 * HEAD_DIM), MXU_DTYPE),
            jax.ShapeDtypeStruct((b_, t_, NSA_KV * LANES), F32),
        ],
        compiler_params=_params(("parallel", "parallel")),
        name="attn_proj",
    )(x, ng, w, qg, kg)


def _sgu_proj_kernel(x_ref, ng_ref, w_ref, lng_ref, lnb_ref, gu_out, vn_out):
    h = _rms(x_ref[0], ng_ref[...]).astype(MXU_DTYPE)
    gu_out[0] = _gelu(_dot(h, w_ref[:, 0:D_SG])).astype(gu_out.dtype)
    gv = _gelu(_dot(h, w_ref[:, D_SG:2 * D_SG]))
    for hh in range(SG_HEADS):
        sl = slice(SG_DIM * hh, SG_DIM * (hh + 1))
        v = gv[:, sl]
        mu = jnp.mean(v, axis=-1, keepdims=True)
        var = jnp.mean(jnp.square(v - mu), axis=-1, keepdims=True)
        y = (v - mu) * lax.rsqrt(var + EPS) * lng_ref[hh:hh + 1, :] + lnb_ref[hh:hh + 1, :]
        vn_out[0, :, sl] = y.astype(vn_out.dtype)


def _sgu_proj(x, ng, w, lng, lnb, tm):
    b_, t_, d_ = x.shape
    return pl.pallas_call(
        _sgu_proj_kernel,
        grid=(b_, t_ // tm),
        in_specs=[
            pl.BlockSpec((1, tm, d_), lambda b, i: (b, i, 0)),
            pl.BlockSpec((1, d_), lambda b, i: (0, 0)),
            pl.BlockSpec((d_, 2 * D_SG), lambda b, i: (0, 0), pipeline_mode=pl.Buffered(1)),
            pl.BlockSpec((SG_HEADS, SG_DIM), lambda b, i: (0, 0)),
            pl.BlockSpec((SG_HEADS, SG_DIM), lambda b, i: (0, 0)),
        ],
        out_specs=[
            pl.BlockSpec((1, tm, D_SG), lambda b, i: (b, i, 0)),
            pl.BlockSpec((1, tm, D_SG), lambda b, i: (b, i, 0)),
        ],
        out_shape=[
            jax.ShapeDtypeStruct((b_, t_, D_SG), MXU_DTYPE),
            jax.ShapeDtypeStruct((b_, t_, D_SG), MXU_DTYPE),
        ],
        compiler_params=_params(("parallel", "parallel")),
        name="sgu_proj",
    )(x, ng, w, lng, lnb)


def _compress_kernel(x_ref, w1_ref, w2_ref, pe_ref, kg_ref, out_ref):
    which = pl.program_id(1)
    x = x_ref[0, 0, 0]
    nch = x.shape[0]
    half = CMP_STRIDE * HEAD_DIM
    a = _dot(x, w1_ref[0, 0, 0:half, :])
    bm = _dot(x, w1_ref[0, 0, half:2 * half, :])
    pec = _dot(pe_ref[0, 0], w1_ref[0, 0])
    hid = _gelu(a + pltpu.roll(bm, shift=nch - 1, axis=0) + pec)
    y = _dot(hid, w2_ref[0, 0])
    y = jnp.where(which == 0, _rms(y, kg_ref[0:1, :]), y)
    lane = lax.broadcasted_iota(jnp.int32, (CMP_PAD, LANES), 1)
    flag = ((lane == PAD_FLAG_LANE) & (which == 0)).astype(out_ref.dtype)
    out_ref[0, 0, 0, 0:CMP_PAD, :] = flag
    out_ref[0, 0, 0, CMP_PAD:CMP_PAD + nch, 0:HEAD_DIM] = y.astype(out_ref.dtype)
    out_ref[0, 0, 0, CMP_PAD:CMP_PAD + nch, HEAD_DIM:LANES] = jnp.zeros((nch, LANES - HEAD_DIM), out_ref.dtype)


def _compress(xc, w1, w2, pe, kg):
    b_, _, g_, nch, fl = xc.shape
    return pl.pallas_call(
        _compress_kernel,
        grid=(b_, 2, g_),
        in_specs=[
            pl.BlockSpec((1, 1, 1, nch, fl), lambda b, w, g: (b, w, g, 0, 0)),
            pl.BlockSpec((1, 1, 2 * fl, CMP_HIDDEN), lambda b, w, g: (w, g, 0, 0)),
            pl.BlockSpec((1, 1, CMP_HIDDEN, HEAD_DIM), lambda b, w, g: (w, g, 0, 0)),
            pl.BlockSpec((1, 1, 1, 2 * fl), lambda b, w, g: (w, g, 0, 0)),
            pl.BlockSpec((3, HEAD_DIM), lambda b, w, g: (0, 0)),
        ],
        out_specs=pl.BlockSpec((1, 1, 1, CMP_PAD + nch, LANES), lambda b, w, g: (b, w, g, 0, 0)),
        out_shape=jax.ShapeDtypeStruct((b_, 2, g_, CMP_PAD + nch, LANES), F32),
        compiler_params=_params(("parallel", "parallel", "parallel")),
        name="nsa_compress",
    )(xc, w1, w2, pe, kg)


def _nsa_kernel(q_ref, gate_ref, kc_ref, vc_ref, ks_ref, vs_ref, kw_ref, vw_ref,
                tbm_ref, wbm_ref, cbm_ref, qmid_ref, ov_ref, o_ref, qa_ref):
    i = pl.program_id(2)
    qb = Q_BLOCK
    rows = NSA_REP * qb
    ncmp = CMP_PAD

    qblk = q_ref[0]
    for r in range(NSA_REP):
        rs = slice(qb * r, qb * (r + 1))
        qa_ref[rs, 0:HEAD_DIM] = qblk[:, HEAD_DIM * r:HEAD_DIM * (r + 1)]
        qa_ref[rs, HEAD_DIM:LANES] = jnp.broadcast_to(
            qmid_ref[0, r][:, 0:LANES - HEAD_DIM], (qb, LANES - HEAD_DIM)).astype(qa_ref.dtype)
    q_lo = qa_ref[:, 0:LANES]

    wstart = pl.multiple_of(KEY_PAD + qb * (i + 1) - WIN_KEYS, qb)
    s_w = _dot_nt(q_lo, kw_ref[0, 0, pl.ds(wstart, WIN_KEYS), :])
    s = jnp.concatenate([s_w[qb * r:qb * (r + 1)] + wbm_ref[0, r] for r in range(NSA_REP)], axis=0)
    p = jnp.exp(s - jnp.max(s, axis=-1, keepdims=True))
    acc_w = _dot(p, vw_ref[0, 0, pl.ds(wstart, WIN_KEYS), :])

    cstart = pl.multiple_of(8 * i + 8, 8)
    kc = kc_ref[0, 0, 0, pl.ds(cstart, ncmp), :]
    vc = vc_ref[0, 0, 0, pl.ds(cstart, ncmp), :]
    s_c = _dot_nt(q_lo, kc)
    o_c = []
    psum = jnp.zeros((qb, ncmp), F32)
    for r in range(NSA_REP):
        s = s_c[qb * r:qb * (r + 1)] + cbm_ref[0, r]
        m = jnp.max(s, axis=-1, keepdims=True)
        p = jnp.exp(s - m)
        l = jnp.sum(p, axis=-1, keepdims=True)
        p = p * jnp.where(m > 0.5 * NEG, 1.0 / l, 0.0)
        o_c.append(_dot(p, vc)[:, 0:HEAD_DIM])
        psum = psum + p
    p_hi = psum.astype(BF16)
    p_lo = (psum - p_hi.astype(F32)).astype(BF16)
    ov = ov_ref[...]
    imp = (jnp.dot(p_hi, ov, preferred_element_type=F32)
           + jnp.dot(p_lo, ov, preferred_element_type=F32))

    nsel = LANES
    sp = lax.broadcasted_iota(jnp.int32, (qb, nsel), 1)
    rr = lax.broadcasted_iota(jnp.int32, (qb, nsel), 0)
    cur = (nsel - 2) + (rr >= SLC_BLOCK).astype(jnp.int32)
    s_abs = sp + (2 * i + 2 - nsel)
    valid = s_abs >= 0
    forced = ((s_abs == 0) | (sp == cur) | (sp == cur - 1)) & valid
    excluded = forced | (sp > cur) | (~valid)
    cand_t = jnp.where(excluded, NEG, imp).T
    v = cand_t
    tau = None
    for _ in range(SLC_TOPN - N_FORCED):
        tau = jnp.max(v, axis=0, keepdims=True)
        v = jnp.where(v >= tau, 3.0 * NEG, v)
    picked = jnp.where((cand_t >= tau) & (cand_t > 0.5 * NEG), 1.0, 0.0).T
    sel_neg = jnp.where((picked > 0.5) | forced, 0.0, NEG)
    sel_neg = pltpu.roll(sel_neg, shift=(2 * i + 2) % nsel, axis=1).astype(qa_ref.dtype)
    for r in range(NSA_REP):
        qa_ref[qb * r:qb * (r + 1), LANES:2 * LANES] = sel_neg
    q_ext = qa_ref[...]

    kt = SEL_TILE

    def tile_start(dd):
        return pl.multiple_of(KEY_PAD + qb * (i + 1) - kt * (dd + 1), qb)

    def tile_scores(dd):
        return _dot_nt(q_ext, ks_ref[0, 0, pl.ds(tile_start(dd), kt), :])

    def tile_softmax(s, dd):
        m_t = jnp.max(s, axis=-1, keepdims=True)
        return m_t, _dot(jnp.exp(s - m_t), vs_ref[0, 0, pl.ds(tile_start(dd), kt), :])

    def near_tile(dd, carry):
        m_old, acc = carry
        s = tile_scores(dd)
        s = jnp.concatenate([s[qb * r:qb * (r + 1)] + tbm_ref[0, r, dd]
                             for r in range(NSA_REP)], axis=0)
        m_t, pv = tile_softmax(s, dd)
        m_new = jnp.maximum(m_old, m_t)
        return m_new, jnp.exp(m_old - m_new) * acc + jnp.exp(m_t - m_new) * pv

    def far_pair(jj, carry):
        m_old, acc = carry
        dd_a = NEAR_TILES + 2 * jj
        dd_b = dd_a + 1
        s_a = tile_scores(dd_a)
        s_b = tile_scores(dd_b)
        m_a, pv_a = tile_softmax(s_a, dd_a)
        m_b, pv_b = tile_softmax(s_b, dd_b)
        m_new = jnp.maximum(m_old, jnp.maximum(m_a, m_b))
        acc = (jnp.exp(m_old - m_new) * acc + jnp.exp(m_a - m_new) * pv_a
               + jnp.exp(m_b - m_new) * pv_b)
        return m_new, acc

    n_tiles = (i * qb + qb + kt - 1) // kt
    carry = (jnp.full((rows, 1), NEG, F32), jnp.zeros((rows, 2 * HEAD_DIM), F32))
    carry = lax.fori_loop(0, jnp.minimum(n_tiles, NEAR_TILES), near_tile, carry)
    carry = lax.fori_loop(0, (jnp.maximum(n_tiles - NEAR_TILES, 0) + 1) // 2, far_pair, carry)
    acc_s = carry[1]

    gt = gate_ref[0]
    for r in range(NSA_REP):
        rs = slice(qb * r, qb * (r + 1))
        o_s = acc_s[rs, 0:HEAD_DIM] / acc_s[rs, HEAD_DIM:HEAD_DIM + 1]
        o_w = acc_w[rs, 0:HEAD_DIM] / acc_w[rs, HEAD_DIM:HEAD_DIM + 1]
        o = (gt[:, 3 * r + 0:3 * r + 1] * o_c[r] + gt[:, 3 * r + 1:3 * r + 2] * o_s
             + gt[:, 3 * r + 2:3 * r + 3] * o_w)
        o_ref[0, :, HEAD_DIM * r:HEAD_DIM * (r + 1)] = o.astype(o_ref.dtype)


def _nsa_attention(qn, gate, kcv, ksa, vsp, kwa, vwp, tbm, wbm, cbm, qmid, ov):
    b_, t_, _ = qn.shape
    tp = ksa.shape[2]
    ncp = kcv.shape[3]
    gw = NSA_REP * HEAD_DIM
    assert t_ // SLC_BLOCK <= LANES
    kvspec = lambda wd: pl.BlockSpec((1, 1, tp, wd), lambda b, g, i: (b, g, 0, 0))
    return pl.pallas_call(
        _nsa_kernel,
        grid=(b_, NSA_KV, t_ // Q_BLOCK),
        in_specs=[
            pl.BlockSpec((1, Q_BLOCK, gw), lambda b, g, i: (b, i, g)),
            pl.BlockSpec((1, Q_BLOCK, LANES), lambda b, g, i: (b, i, g)),
            pl.BlockSpec((1, 1, 1, ncp, LANES), lambda b, g, i: (b, 0, g, 0, 0)),
            pl.BlockSpec((1, 1, 1, ncp, LANES), lambda b, g, i: (b, 1, g, 0, 0)),
            kvspec(2 * LANES), kvspec(LANES), kvspec(LANES), kvspec(LANES),
            pl.BlockSpec((1, NSA_REP, NEAR_TILES, Q_BLOCK, SEL_TILE), lambda b, g, i: (g, 0, 0, 0, 0)),
            pl.BlockSpec((1, NSA_REP, Q_BLOCK, WIN_KEYS), lambda b, g, i: (g, 0, 0, 0)),
            pl.BlockSpec((1, NSA_REP, Q_BLOCK, CMP_PAD), lambda b, g, i: (g, 0, 0, 0)),
            pl.BlockSpec((1, NSA_REP, 1, LANES), lambda b, g, i: (g, 0, 0, 0)),
            pl.BlockSpec((CMP_PAD, LANES), lambda b, g, i: (0, 0)),
        ],
        out_specs=pl.BlockSpec((1, Q_BLOCK, gw), lambda b, g, i: (b, i, g)),
        out_shape=jax.ShapeDtypeStruct((b_, t_, D_NSA), F32),
        scratch_shapes=[pltpu.VMEM((NSA_REP * Q_BLOCK, 2 * LANES), MXU_DTYPE)],
        compiler_params=_params(("parallel", "parallel", "arbitrary")),
        name="nsa_attention",
    )(qn, gate, kcv, kcv, ksa, vsp, kwa, vwp, tbm, wbm, cbm, qmid, ov)


def _mix_kernel(x_ref, on_ref, gu_ref, vn_ref, sgw_ref, sgb_ref, gn_ref, gs_ref, wo_ref,
                fg_ref, wq_ref, x1_out, h2t_out, qp_out):
    tm = x_ref.shape[0]
    tri = (lax.broadcasted_iota(jnp.int32, (SG_CHUNK, SG_CHUNK), 0)
           >= lax.broadcasted_iota(jnp.int32, (SG_CHUNK, SG_CHUNK), 1))
    sgb = sgb_ref[...]
    y = x_ref[...] + _dot(_rms(on_ref[...], gn_ref[...]), wo_ref[0:D_NSA, :])
    parts = []
    for c in range(tm // SG_CHUNK):
        cs = slice(SG_CHUNK * c, SG_CHUNK * (c + 1))
        heads = []
        for hh in range(SG_HEADS):
            sl = slice(SG_DIM * hh, SG_DIM * (hh + 1))
            w = jnp.where(tri, sgw_ref[hh], 0.0)
            mixed = _dot(w, vn_ref[cs, sl]) + sgb[:, hh:hh + 1]
            heads.append(gu_ref[cs, sl].astype(F32) * mixed)
        parts.append(jnp.concatenate(heads, axis=1))
    o_sg = jnp.concatenate(parts, axis=0)
    y = y + _dot(_rms(o_sg, gs_ref[...]), wo_ref[D_NSA:D_NSA + D_SG, :])
    x1_out[...] = y
    h2 = _rms(y, fg_ref[...])
    h2t_out[...] = h2.T.astype(h2t_out.dtype)
    qp_out[...] = _dot(h2, wq_ref[...])


def _mix(x2, onsa, gu, vn, sgw, sgb_t, gn, gs, wo, fg, wq, tm):
    n_, d_ = x2.shape
    row = lambda wd: pl.BlockSpec((tm, wd), lambda i: (i, 0))
    const = lambda shape, **kw: pl.BlockSpec(shape, lambda i: (0,) * len(shape), **kw)
    return pl.pallas_call(
        _mix_kernel,
        grid=(n_ // tm,),
        in_specs=[
            row(d_), row(D_NSA), row(D_SG), row(D_SG),
            const((SG_HEADS, SG_CHUNK, SG_CHUNK)), const((SG_CHUNK, SG_HEADS)),
            const((1, D_NSA)), const((1, D_SG)),
            const((D_NSA + D_SG, d_), pipeline_mode=pl.Buffered(1)),
            const((1, d_)),
            const((d_, wq.shape[1]), pipeline_mode=pl.Buffered(1)),
        ],
        out_specs=[row(d_), pl.BlockSpec((d_, tm), lambda i: (0, i)), row(wq.shape[1])],
        out_shape=[
            jax.ShapeDtypeStruct((n_, d_), F32),
            jax.ShapeDtypeStruct((d_, n_), MXU_DTYPE),
            jax.ShapeDtypeStruct((n_, wq.shape[1]), F32),
        ],
        compiler_params=_params(("parallel",)),
        name="mix_out_proj",
    )(x2, onsa, gu, vn, sgw, sgb_t, gn, gs, wo, fg, wq)


def _top_rows(v, k):
    tops = []
    for _ in range(k):
        m = jnp.max(v, axis=0, keepdims=True)
        tops.append(m)
        v = jnp.where(v >= m, NEG, v)
    return jnp.concatenate(tops, axis=0)


def _peer_topk_kernel(qp_ref, sk_ref, s1_out, s2_out, st_out):
    for hh in range(PEER_HEADS):
        base = 2 * PEER_HALF * hh
        s1 = _dot_nt(sk_ref[0], qp_ref[:, base:base + PEER_HALF])
        s2 = _dot_nt(sk_ref[1], qp_ref[:, base + PEER_HALF:base + 2 * PEER_HALF])
        s1_out[hh] = s1
        for j in range(s2.shape[1] // LANES):
            s2_out[hh, j] = s2[:, LANES * j:LANES * (j + 1)]
        a = _top_rows(s1, PEER_TOPK)
        b = _top_rows(s2, PEER_TOPK)
        cand = [a[0:1] + b]
        for ra in range(1, 8):
            cand.append(a[ra:ra + 1] + b[0:8])
        cand.append(a[8:16] + b[0:1])
        top = _top_rows(jnp.concatenate(cand, axis=0), PEER_TOPK)
        z = jnp.sum(jnp.exp(top - top[0:1]), axis=0, keepdims=True)
        st_out[0, hh:hh + 1, :] = top[PEER_TOPK - 1:PEER_TOPK]
        st_out[1, hh:hh + 1, :] = b[0:1]
        st_out[2, hh:hh + 1, :] = a[0:1] + jnp.log(z)


def _peer_topk(qp, sub_keys, tm):
    n_, qd = qp.shape
    return pl.pallas_call(
        _peer_topk_kernel,
        grid=(n_ // tm,),
        in_specs=[
            pl.BlockSpec((tm, qd), lambda i: (i, 0)),
            pl.BlockSpec((2, PEER_NKEYS, PEER_HALF), lambda i: (0, 0, 0)),
        ],
        out_specs=[
            pl.BlockSpec((PEER_HEADS, PEER_NKEYS, tm), lambda i: (0, 0, i)),
            pl.BlockSpec((PEER_HEADS, tm // LANES, PEER_NKEYS, LANES), lambda i: (0, i, 0, 0)),
            pl.BlockSpec((3, PEER_HEADS,tm), lambda i: (0, 0, i)),
        ],
        out_shape=[
            jax.ShapeDtypeStruct((PEER_HEADS, PEER_NKEYS, n_), F32),
            jax.ShapeDtypeStruct((PEER_HEADS, n_ // LANES, PEER_NKEYS, LANES), F32),
            jax.ShapeDtypeStruct((3, PEER_HEADS,n_), F32),
        ],
        compiler_params=_params(("parallel",)),
        name="peer_topk",
    )(qp, sub_keys)


def _peer_mix_kernel(ht_ref, u_ref, vt_ref, s1_ref, s2_ref, st_ref, x1_ref, out_ref,
                     acc_ref, e2_ref, crow_ref, a0_ref, a1_ref, z0_ref, z1_ref):
    ie = pl.program_id(1)
    te = u_ref.shape[0]
    tm = ht_ref.shape[1]
    n_tiles = pl.num_programs(1) - 2
    per_tile = te // PEER_NKEYS

    @pl.when(ie == 0)
    def _():
        acc_ref[...] = jnp.zeros_like(acc_ref)
        a1_ref[...] = jnp.zeros_like(a1_ref)
        z0_ref[...] = jnp.zeros_like(z0_ref)
        for hh in range(PEER_HEADS):
            for j in range(tm // LANES):
                e2_ref[hh, j] = jnp.exp(s2_ref[hh, j] - st_ref[1, hh:hh + 1, LANES * j:LANES * (j + 1)])

    d_ = vt_ref.shape[0]
    n_lane = tm // LANES
    n_blocks = per_tile * n_lane
    a_units = [(r, c, kc) for r in range(te // MXU_ROWS) for c in range(tm // MXU_ROWS)
               for kc in range(d_ // A_KCHUNK)]
    c_units = [(r, c) for r in range(d_ // MXU_ROWS) for c in range(tm // MXU_ROWS)]

    def step(a_w, a_r, z_w, z_r):
        tile = jnp.clip(ie - 1, 0, n_tiles - 1)

        def a_unit(r, c, kc):
            rs = slice(MXU_ROWS * r, MXU_ROWS * (r + 1))
            cs = slice(MXU_ROWS * c, MXU_ROWS * (c + 1))
            ds = slice(A_KCHUNK * kc, A_KCHUNK * (kc + 1))
            res = jnp.dot(u_ref[rs, ds], ht_ref[ds, cs], preferred_element_type=F32)
            for jj in range(MXU_ROWS // LANES):
                j = c * (MXU_ROWS // LANES) + jj
                part = res[:, LANES * jj:LANES * (jj + 1)]
                if kc == 0:
                    a_w[j, rs, :] = part
                else:
                    a_w[j, rs, :] += part

        def c_unit(r, c):
            rs = slice(MXU_ROWS * r, MXU_ROWS * (r + 1))
            cs = slice(MXU_ROWS * c, MXU_ROWS * (c + 1))
            acc_ref[rs, cs] += jnp.dot(vt_ref[rs, :], z_r[:, cs], preferred_element_type=F32)

        def b_block(k, j):
            ls = slice(LANES * j, LANES * (j + 1))
            for part in range(PEER_NKEYS // GATE_ROWS):
                ks = slice(GATE_ROWS * part, GATE_ROWS * (part + 1))
                rs = slice(PEER_NKEYS * k + GATE_ROWS * part, PEER_NKEYS * k + GATE_ROWS * (part + 1))
                g = jnp.zeros((GATE_ROWS, LANES), F32)
                for hh in range(PEER_HEADS):
                    row = PEER_HEADS * k + hh
                    hit = (s2_ref[hh, j, ks, :] + crow_ref[0, row:row + 1, ls]) >= st_ref[0, hh:hh + 1, ls]
                    g = g + jnp.where(hit, e2_ref[hh, j, ks, :] * crow_ref[1, row:row + 1, ls], 0.0)
                z_w[rs, ls] = (_gelu(a_r[j, rs, :]) * g).astype(z_w.dtype)

        for k in range(per_tile):
            i1 = tile * per_tile + k
            hs = slice(PEER_HEADS * k, PEER_HEADS * (k + 1))
            for hh in range(PEER_HEADS):
                row = PEER_HEADS * k + hh
                crow_ref[0, row:row + 1, :] = s1_ref[hh, pl.ds(i1, 1), :]
            crow_ref[1, hs, :] = jnp.exp(crow_ref[0, hs, :] - st_ref[2])

        a_iter = iter(a_units)
        c_iter = iter(c_units)
        a_every = max(n_blocks // len(a_units), 1)
        c_per = -(-len(c_units) // n_blocks)
        for blk in range(n_blocks):
            k, j = divmod(blk, n_lane)
            if blk % a_every == 0:
                unit = next(a_iter, None)
                if unit is not None:
                    a_unit(*unit)
            for _ in range(c_per):
                unit = next(c_iter, None)
                if unit is not None:
                    c_unit(*unit)
            b_block(k, j)
        for unit in a_iter:
            a_unit(*unit)
        for unit in c_iter:
            c_unit(*unit)

    @pl.when(ie % 2 == 0)
    def _():
        step(a0_ref, a1_ref, z1_ref, z0_ref)

    @pl.when(ie % 2 == 1)
    def _():
        step(a1_ref, a0_ref, z0_ref, z1_ref)

    @pl.when(ie == pl.num_programs(1) - 1)
    def _():
        out_ref[...] = x1_ref[...] + acc_ref[...].T


def _peer_mix(h2t, u, vt, s1t, s2t, stats, x1, tm, te):
    d_, n_ = h2t.shape
    n_tiles = u.shape[0] // te
    return pl.pallas_call(
        _peer_mix_kernel,
        grid=(n_ // tm, n_tiles + 2),
        in_specs=[
            pl.BlockSpec((d_, tm), lambda it, ie: (0, it)),
            pl.BlockSpec((te, d_), lambda it, ie: (jnp.minimum(ie, n_tiles - 1), 0)),
            pl.BlockSpec((d_, te), lambda it, ie: (0, jnp.maximum(ie - 2, 0))),
            pl.BlockSpec((PEER_HEADS, PEER_NKEYS, tm), lambda it, ie: (0, 0, it)),
            pl.BlockSpec((PEER_HEADS, tm // LANES, PEER_NKEYS, LANES), lambda it, ie: (0, it, 0, 0)),
            pl.BlockSpec((3, PEER_HEADS,tm), lambda it, ie: (0, 0, it)),
            pl.BlockSpec((tm, d_), lambda it, ie: (it, 0)),
        ],
        out_specs=pl.BlockSpec((tm, d_), lambda it, ie: (it, 0)),
        out_shape=jax.ShapeDtypeStruct((n_, d_), F32),
        scratch_shapes=[
            pltpu.VMEM((d_, tm), F32),
            pltpu.VMEM((PEER_HEADS, tm // LANES, PEER_NKEYS, LANES), F32),
            pltpu.VMEM((2, PEER_HEADS * (te // PEER_NKEYS), tm), F32),
            pltpu.VMEM((tm // LANES, te, LANES), F32), pltpu.VMEM((tm // LANES, te, LANES), F32),
            pltpu.VMEM((te, tm), MXU_DTYPE), pltpu.VMEM((te, tm), MXU_DTYPE),
        ],
        compiler_params=_params(("parallel", "arbitrary")),
        name="peer_mix",
    )(h2t, u, vt, s1t, s2t, stats, x1)


def _rel_bucket(dist):
    n = jnp.maximum(dist, 0)
    max_exact = REL_BUCKETS // 2
    nf = jnp.maximum(n, 1).astype(F32)
    large = max_exact + (jnp.log(nf / max_exact) / math.log(REL_MAX_DIST / max_exact)
                         * (REL_BUCKETS - max_exact)).astype(jnp.int32)
    large = jnp.minimum(large, REL_BUCKETS - 1)
    return jnp.where(n < max_exact, n, large)


def _toeplitz(btab, off, rows, cols):
    per = rows + cols
    k = np.concatenate([np.arange(cols), np.zeros(1, np.int64), np.arange(-(rows - 1), 0)])
    idx = np.clip(off - k, 0, btab.shape[1] - 1)
    w = jnp.take(btab, jnp.asarray(idx, jnp.int32), axis=1)
    x = jnp.tile(w, (1, rows))[:, :rows * (per - 1)]
    return x.reshape(btab.shape[0], rows, per - 1)[:, :, :cols]


def _bias_tiles(rel_bias):
    dist = jnp.arange(BIAS_TABLE, dtype=jnp.int32)
    btab = rel_bias.astype(F32)[_rel_bucket(dist)].T
    tbw = jnp.stack([_toeplitz(btab, SEL_TILE * (dd + 1) - Q_BLOCK, Q_BLOCK, SEL_TILE)
                     for dd in range(NEAR_TILES)], axis=1)
    wb = _toeplitz(btab, WINDOW, Q_BLOCK, WIN_KEYS)
    off = CMP_STRIDE * (LANES - 8) - (CMP_BLOCK - 1)
    front = CMP_STRIDE * LANES - off
    length = CMP_STRIDE * (LANES + 8 + 2)
    padded = jnp.pad(btab, ((0, 0), (front, length - front - BIAS_TABLE)))
    ch = padded.reshape(btab.shape[0], length // CMP_STRIDE, CMP_STRIDE)
    nwin = LANES + 1
    win = jnp.stack([ch[:, a:a + nwin] for a in range(Q_BLOCK // CMP_STRIDE)], axis=2)
    win = win.reshape(btab.shape[0], nwin, Q_BLOCK)
    cb = jnp.swapaxes(win[:, 1:nwin][:, ::-1], 1, 2)
    cfar = rel_bias.astype(F32)[REL_BUCKETS - 1]
    grp = lambda a: a.reshape((NSA_KV, NSA_REP) + a.shape[1:])

    r = np.arange(Q_BLOCK)[:, None]
    near_ok = np.stack([r - np.arange(SEL_TILE)[None, :] + SEL_TILE * (dd + 1) - Q_BLOCK >= 0
                        for dd in range(NEAR_TILES)])
    tbm = jnp.where(near_ok, tbw - cfar[:, None, None, None], NEG)
    dist_w = r - np.arange(WIN_KEYS)[None, :] + WINDOW
    wbm = jnp.where((dist_w >= 0) & (dist_w < WINDOW), wb, NEG)
    dist_c = r - CMP_STRIDE * np.arange(CMP_PAD)[None, :] + (CMP_STRIDE * CMP_PAD - Q_BLOCK - (CMP_BLOCK - 1))
    cbias = jnp.concatenate(
        [jnp.broadcast_to(cfar[:, None, None], (NSA_HEADS, Q_BLOCK, CMP_PAD - LANES)), cb], axis=2)
    cbm = jnp.where(dist_c >= 0, cbias, NEG)

    hi = cfar.astype(BF16).astype(F32)
    lo = (cfar - hi).astype(BF16).astype(F32)
    qmid = jnp.zeros((NSA_HEADS, 1, LANES), F32)
    qmid = qmid.at[:, 0, FAR_HI_LANE - HEAD_DIM].set(hi).at[:, 0, FAR_LO_LANE - HEAD_DIM].set(lo)
    qmid = qmid.at[:, 0, PAD_FLAG_LANE - HEAD_DIM].set(NEG)
    return grp(tbm), grp(wbm), grp(cbm), grp(qmid)


def _extend_keys(ks, kw):
    tp = ks.shape[2] + KEY_PAD
    pos = np.arange(tp) - KEY_PAD
    real = pos >= 0
    mid_s = np.zeros((tp, LANES - HEAD_DIM), np.float32)
    mid_s[:, FAR_HI_LANE - HEAD_DIM] = real
    mid_s[:, FAR_LO_LANE - HEAD_DIM] = real
    mid_s[:, PAD_FLAG_LANE - HEAD_DIM] = ~real
    onehot = ((pos[:, None] // SLC_BLOCK) == np.arange(LANES)[None, :]) & real[:, None]
    mid_w = np.zeros((tp, LANES - HEAD_DIM), np.float32)
    mid_w[:, PAD_FLAG_LANE - HEAD_DIM] = ~real
    padk = lambda a: jnp.pad(a, ((0, 0), (0, 0), (KEY_PAD, 0), (0, 0)))
    bc = lambda a: jnp.broadcast_to(jnp.asarray(a, ks.dtype), ks.shape[:2] + a.shape)
    ksa = jnp.concatenate([padk(ks), bc(mid_s), bc(onehot.astype(np.float32))], axis=-1)
    kwa = jnp.concatenate([padk(kw), bc(mid_w)], axis=-1)
    return ksa, kwa


def _overlap_matrix():
    c = np.arange(CMP_PAD)[:, None]
    s = np.arange(LANES)[None, :]
    per = SLC_BLOCK // CMP_STRIDE
    ov = (s == c // per) | ((c % per == per - 1) & (s == c // per + 1))
    return jnp.asarray(ov.astype(np.float32), BF16)


def kernel(x, attn_norm_g, w_in, q_norm_g, k_norm_g, cmp_pe_k, cmp_w1_k, cmp_w2_k, cmp_pe_v,
           cmp_w1_v, cmp_w2_v, rel_bias, sg_ln_g, sg_ln_b, sg_w, sg_b, out_norm_nsa, out_norm_sg,
           w_out, ffn_norm_g, peer_w_query, peer_sub_keys, peer_u, peer_v):
    b_, t_, d_ = x.shape
    n_ = b_ * t_
    assert t_ % Q_BLOCK == 0 and t_ >= WIN_KEYS and d_ == D_NSA + D_SG

    c_kv = D_NSA + 6 * D_KV
    c_gate = c_kv + 3 * NSA_HEADS
    wg = w_in[:, c_kv:c_gate].reshape(d_, NSA_KV, 3 * NSA_REP)
    wg = jnp.pad(wg, ((0, 0), (0, 0), (0, LANES - 3 * NSA_REP))).reshape(d_, NSA_KV * LANES)
    w_attn = jnp.concatenate([w_in[:, :c_kv], wg], axis=1).astype(MXU_DTYPE)
    w_sgu = w_in[:, c_gate:].astype(MXU_DTYPE)
    ng = attn_norm_g.reshape(1, d_)

    qn, kcv, ks, vs, kw, vw, gate = _attn_proj(x, ng, w_attn, q_norm_g.reshape(1, HEAD_DIM),
                                               k_norm_g, tm=256)
    gu, vn = _sgu_proj(x, ng, w_sgu, sg_ln_g, sg_ln_b, tm=256)

    nch = t_ // CMP_STRIDE
    xc = kcv.reshape(b_, nch, CMP_STRIDE, 2, NSA_KV, HEAD_DIM).transpose(0, 3, 4, 1, 2, 5)
    xc = xc.reshape(b_, 2, NSA_KV, nch, CMP_STRIDE * HEAD_DIM)
    w1 = jnp.stack([cmp_w1_k, cmp_w1_v]).astype(MXU_DTYPE)
    w2 = jnp.stack([cmp_w2_k, cmp_w2_v]).astype(MXU_DTYPE)
    pe = jnp.stack([cmp_pe_k, cmp_pe_v]).transpose(0, 2, 1, 3).reshape(2, NSA_KV, 1, CMP_BLOCK * HEAD_DIM)
    kcv_c = _compress(xc, w1, w2, pe.astype(MXU_DTYPE), k_norm_g)

    padk = lambda a: jnp.pad(a, ((0, 0), (0, 0), (KEY_PAD, 0), (0, 0)))
    tbm, wbm, cbm, qmid = _bias_tiles(rel_bias)
    ksa, kwa = _extend_keys(ks, kw)
    onsa = _nsa_attention(qn, gate, kcv_c, ksa, padk(vs), kwa, padk(vw),
                          tbm, wbm, cbm, qmid, _overlap_matrix())

    x1, h2t, qp = _mix(x.reshape(n_, d_), onsa.reshape(n_, D_NSA), gu.reshape(n_, D_SG),
                      vn.reshape(n_, D_SG), sg_w, sg_b.T, out_norm_nsa.reshape(1, D_NSA),
                      out_norm_sg.reshape(1, D_SG), w_out.astype(MXU_DTYPE),
                      ffn_norm_g.reshape(1, d_), peer_w_query.astype(MXU_DTYPE), tm=256)

    s1t, s2t, stats = _peer_topk(qp, peer_sub_keys, tm=256)
    out = _peer_mix(h2t, peer_u.astype(MXU_DTYPE), peer_v.T.astype(MXU_DTYPE), s1t, s2t, stats, x1,
                    tm=min(512, n_), te=512)
    return out.reshape(b_, t_, d_)
```

```python
import functools
import math

import jax
import jax.numpy as jnp
import numpy as np
from jax import lax
from jax.experimental import pallas as pl
from jax.experimental.pallas import tpu as pltpu

F32 = jnp.float32
BF16 = jnp.bfloat16
MXU_DTYPE = BF16

EPS = 1e-6
NEG = -1e30
SQRT_HALF = 0.7071067811865476

NSA_HEADS = 16
NSA_KV = 4
NSA_REP = 4
HEAD_DIM = 64
D_NSA = NSA_HEADS * HEAD_DIM
D_KV = NSA_KV * HEAD_DIM
CMP_STRIDE = 16
CMP_BLOCK = 32
CMP_HIDDEN = 128
SLC_BLOCK = 64
SLC_TOPN = 16
WINDOW = 512
Q_BLOCK = 128
SG_HEADS = 8
SG_DIM = 128
SG_CHUNK = 128
D_SG = SG_HEADS * SG_DIM
REL_BUCKETS = 32
REL_MAX_DIST = 1024
PEER_HEADS = 8
PEER_NKEYS = 128
PEER_HALF = 128
PEER_TOPK = 16

LANES = 128
MXU_ROWS = 256
A_KCHUNK = 1024
GATE_ROWS = 64
KEY_PAD = 1024
CMP_PAD = 512
SEL_TILE = 512
WIN_KEYS = WINDOW + Q_BLOCK
NEAR_TILES = 2
BIAS_TABLE = 2048
FAR_DIST = 897
FAR_HI_LANE = 64
FAR_LO_LANE = 65
PAD_FLAG_LANE = 66
N_FORCED = 3
VMEM_LIMIT = 56 * 1024 * 1024


def _gelu(x):
    return 0.5 * x * (1.0 + lax.erf(x * SQRT_HALF))


def _dot(a, b):
    return jnp.dot(a.astype(MXU_DTYPE), b.astype(MXU_DTYPE), preferred_element_type=F32)


def _dot_nt(a, b):
    return lax.dot_general(a.astype(MXU_DTYPE), b.astype(MXU_DTYPE),
                           (((1,), (1,)), ((), ())), preferred_element_type=F32)


def _rms(x, g):
    ms = jnp.mean(x * x, axis=-1, keepdims=True)
    return x * lax.rsqrt(ms + EPS) * g


def _params(sem):
    return pltpu.CompilerParams(dimension_semantics=sem, vmem_limit_bytes=VMEM_LIMIT)


def _attn_proj_kernel(x_ref, ng_ref, w_ref, qg_ref, kg_ref,
                      q_out, kcv_out, ks_out, vs_out, kw_out, vw_out, gate_out):
    h = _rms(x_ref[0], ng_ref[...]).astype(MXU_DTYPE)
    tm = h.shape[0]

    pq = _dot(h, w_ref[:, 0:D_NSA])
    qg = qg_ref[...]
    for hh in range(NSA_HEADS):
        sl = slice(HEAD_DIM * hh, HEAD_DIM * (hh + 1))
        q_out[0, :, sl] = (_rms(pq[:, sl], qg) * (HEAD_DIM ** -0.5)).astype(q_out.dtype)

    kcv_out[0] = _dot(h, w_ref[:, D_NSA:D_NSA + 2 * D_KV]).astype(kcv_out.dtype)

    base = D_NSA + 2 * D_KV
    p4 = _dot(h, w_ref[:, base:base + 4 * D_KV])
    ones_col = (lax.broadcasted_iota(jnp.int32, (tm, HEAD_DIM), 1) == 0).astype(F32)
    for g in range(NSA_KV):
        sl = slice(HEAD_DIM * g, HEAD_DIM * (g + 1))
        ks_out[0, g] = _rms(p4[:, 0 * D_KV:1 * D_KV][:, sl], kg_ref[1:2, :]).astype(ks_out.dtype)
        kw_out[0, g] = _rms(p4[:, 2 * D_KV:3 * D_KV][:, sl], kg_ref[2:3, :]).astype(kw_out.dtype)
        vs_out[0, g, :, 0:HEAD_DIM] = p4[:, 1 * D_KV:2 * D_KV][:, sl].astype(vs_out.dtype)
        vs_out[0, g, :, HEAD_DIM:2 * HEAD_DIM] = ones_col.astype(vs_out.dtype)
        vw_out[0, g, :, 0:HEAD_DIM] = p4[:, 3 * D_KV:4 * D_KV][:, sl].astype(vw_out.dtype)
        vw_out[0, g, :, HEAD_DIM:2 * HEAD_DIM] = ones_col.astype(vw_out.dtype)

    base = base + 4 * D_KV
    gate_out[0] = jax.nn.sigmoid(_dot(h, w_ref[:, base:base + NSA_KV * LANES]))


def _attn_proj(x, ng, w, qg, kg, tm):
    b_, t_, d_ = x.shape
    nw = w.shape[1]
    kv4 = lambda wd: pl.BlockSpec((1, NSA_KV, tm, wd), lambda b, i: (b, 0, i, 0))
    return pl.pallas_call(
        _attn_proj_kernel,
        grid=(b_, t_ // tm),
        in_specs=[
            pl.BlockSpec((1, tm, d_), lambda b, i: (b, i, 0)),
            pl.BlockSpec((1, d_), lambda b, i: (0, 0)),
            pl.BlockSpec((d_, nw), lambda b, i: (0, 0), pipeline_mode=pl.Buffered(1)),
            pl.BlockSpec((1, HEAD_DIM), lambda b, i: (0, 0)),
            pl.BlockSpec((3, HEAD_DIM), lambda b, i: (0, 0)),
        ],
        out_specs=[
            pl.BlockSpec((1, tm, D_NSA), lambda b, i: (b, i, 0)),
            pl.BlockSpec((1, tm, 2 * D_KV), lambda b, i: (b, i, 0)),
            kv4(HEAD_DIM), kv4(2 * HEAD_DIM), kv4(HEAD_DIM), kv4(2 * HEAD_DIM),
            pl.BlockSpec((1, tm, NSA_KV * LANES), lambda b, i: (b, i, 0)),
        ],
        out_shape=[
            jax.ShapeDtypeStruct((b_, t_, D_NSA), MXU_DTYPE),
            jax.ShapeDtypeStruct((b_, t_, 2 * D_KV), MXU_DTYPE),
            jax.ShapeDtypeStruct((b_, NSA_KV, t_, HEAD_DIM), MXU_DTYPE),
            jax.ShapeDtypeStruct((b_, NSA_KV, t_, 2 * HEAD_DIM), MXU_DTYPE),
            jax.ShapeDtypeStruct((b_, NSA_KV, t_, HEAD_DIM), MXU_DTYPE),
            jax.ShapeDtypeStruct((b_, NSA_KV, t_, 2 * HEAD_DIM), MXU_DTYPE),
            jax.ShapeDtypeStruct((b_, t_, NSA_KV * LANES), F32),
        ],
        compiler_params=_params(("parallel", "parallel")),
        name="attn_proj",
    )(x, ng, w, qg, kg)


def _sgu_proj_kernel(x_ref, ng_ref, w_ref, lng_ref, lnb_ref, gu_out, vn_out):
    h = _rms(x_ref[0], ng_ref[...]).astype(MXU_DTYPE)
    gu_out[0] = _gelu(_dot(h, w_ref[:, 0:D_SG])).astype(gu_out.dtype)
    gv = _gelu(_dot(h, w_ref[:, D_SG:2 * D_SG]))
    for hh in range(SG_HEADS):
        sl = slice(SG_DIM * hh, SG_DIM * (hh + 1))
        v = gv[:, sl]
        mu = jnp.mean(v, axis=-1, keepdims=True)
        var = jnp.mean(jnp.square(v - mu), axis=-1, keepdims=True)
        y = (v - mu) * lax.rsqrt(var + EPS) * lng_ref[hh:hh + 1, :] + lnb_ref[hh:hh + 1, :]
        vn_out[0, :, sl] = y.astype(vn_out.dtype)


def _sgu_proj(x, ng, w, lng, lnb, tm):
    b_, t_, d_ = x.shape
    return pl.pallas_call(
        _sgu_proj_kernel,
        grid=(b_, t_ // tm),
        in_specs=[
            pl.BlockSpec((1, tm, d_), lambda b, i: (b, i, 0)),
            pl.BlockSpec((1, d_), lambda b, i: (0, 0)),
            pl.BlockSpec((d_, 2 * D_SG), lambda b, i: (0, 0), pipeline_mode=pl.Buffered(1)),
            pl.BlockSpec((SG_HEADS, SG_DIM), lambda b, i: (0, 0)),
            pl.BlockSpec((SG_HEADS, SG_DIM), lambda b, i: (0, 0)),
        ],
        out_specs=[
            pl.BlockSpec((1, tm, D_SG), lambda b, i: (b, i, 0)),
            pl.BlockSpec((1, tm, D_SG), lambda b, i: (b, i, 0)),
        ],
        out_shape=[
            jax.ShapeDtypeStruct((b_, t_, D_SG), MXU_DTYPE),
            jax.ShapeDtypeStruct((b_, t_, D_SG), MXU_DTYPE),
        ],
        compiler_params=_params(("parallel", "parallel")),
        name="sgu_proj",
    )(x, ng, w, lng, lnb)


def _compress_kernel(x_ref, w1_ref, w2_ref, pe_ref, kg_ref, out_ref):
    which = pl.program_id(1)
    x = x_ref[0, 0, 0]
    nch = x.shape[0]
    half = CMP_STRIDE * HEAD_DIM
    a = _dot(x, w1_ref[0, 0, 0:half, :])
    bm = _dot(x, w1_ref[0, 0, half:2 * half, :])
    pec = _dot(pe_ref[0, 0], w1_ref[0, 0])
    hid = _gelu(a + pltpu.roll(bm, shift=nch - 1, axis=0) + pec)
    y = _dot(hid, w2_ref[0, 0])
    y = jnp.where(which == 0, _rms(y, kg_ref[0:1, :]), y)
    lane = lax.broadcasted_iota(jnp.int32, (CMP_PAD, LANES), 1)
    flag = ((lane == PAD_FLAG_LANE) & (which == 0)).astype(out_ref.dtype)
    out_ref[0, 0, 0, 0:CMP_PAD, :] = flag
    out_ref[0, 0, 0, CMP_PAD:CMP_PAD + nch, 0:HEAD_DIM] = y.astype(out_ref.dtype)
    out_ref[0, 0, 0, CMP_PAD:CMP_PAD + nch, HEAD_DIM:LANES] = jnp.zeros((nch, LANES - HEAD_DIM), out_ref.dtype)


def _compress(xc, w1, w2, pe, kg):
    b_, _, g_, nch, fl = xc.shape
    return pl.pallas_call(
        _compress_kernel,
        grid=(b_, 2, g_),
        in_specs=[
            pl.BlockSpec((1, 1, 1, nch, fl), lambda b, w, g: (b, w, g, 0, 0)),
            pl.BlockSpec((1, 1, 2 * fl, CMP_HIDDEN), lambda b, w, g: (w, g, 0, 0)),
            pl.BlockSpec((1, 1, CMP_HIDDEN, HEAD_DIM), lambda b, w, g: (w, g, 0, 0)),
            pl.BlockSpec((1, 1, 1, 2 * fl), lambda b, w, g: (w, g, 0, 0)),
            pl.BlockSpec((3, HEAD_DIM), lambda b, w, g: (0, 0)),
        ],
        out_specs=pl.BlockSpec((1, 1, 1, CMP_PAD + nch, LANES), lambda b, w, g: (b, w, g, 0, 0)),
        out_shape=jax.ShapeDtypeStruct((b_, 2, g_, CMP_PAD + nch, LANES), F32),
        compiler_params=_params(("parallel", "parallel", "parallel")),
        name="nsa_compress",
    )(xc, w1, w2, pe, kg)


def _nsa_kernel(q_ref, gate_ref, kc_ref, vc_ref, ks_ref, vs_ref, kw_ref, vw_ref,
                tbm_ref, wbm_ref, cbm_ref, qmid_ref, ov_ref, o_ref, qa_ref):
    i = pl.program_id(2)
    qb = Q_BLOCK
    rows = NSA_REP * qb
    ncmp = CMP_PAD

    qblk = q_ref[0]
    for r in range(NSA_REP):
        rs = slice(qb * r, qb * (r + 1))
        qa_ref[rs, 0:HEAD_DIM] = qblk[:, HEAD_DIM * r:HEAD_DIM * (r + 1)]
        qa_ref[rs, HEAD_DIM:LANES] = jnp.broadcast_to(
            qmid_ref[0, r][:, 0:LANES - HEAD_DIM], (qb, LANES - HEAD_DIM)).astype(qa_ref.dtype)
    q_lo = qa_ref[:, 0:LANES]

    wstart = pl.multiple_of(KEY_PAD + qb * (i + 1) - WIN_KEYS, qb)
    s_w = _dot_nt(q_lo, kw_ref[0, 0, pl.ds(wstart, WIN_KEYS), :])
    s = jnp.concatenate([s_w[qb * r:qb * (r + 1)] + wbm_ref[0, r] for r in range(NSA_REP)], axis=0)
    p = jnp.exp(s - jnp.max(s, axis=-1, keepdims=True))
    acc_w = _dot(p, vw_ref[0, 0, pl.ds(wstart, WIN_KEYS), :])

    cstart = pl.multiple_of(8 * i + 8, 8)
    kc = kc_ref[0, 0, 0, pl.ds(cstart, ncmp), :]
    vc = vc_ref[0, 0, 0, pl.ds(cstart, ncmp), :]
    s_c = _dot_nt(q_lo, kc)
    o_c = []
    psum = jnp.zeros((qb, ncmp), F32)
    for r in range(NSA_REP):
        s = s_c[qb * r:qb * (r + 1)] + cbm_ref[0, r]
        m = jnp.max(s, axis=-1, keepdims=True)
        p = jnp.exp(s - m)
        l = jnp.sum(p, axis=-1, keepdims=True)
        p = p * jnp.where(m > 0.5 * NEG, 1.0 / l, 0.0)
        o_c.append(_dot(p, vc)[:, 0:HEAD_DIM])
        psum = psum + p
    p_hi = psum.astype(BF16)
    p_lo = (psum - p_hi.astype(F32)).astype(BF16)
    ov = ov_ref[...]
    imp = (jnp.dot(p_hi, ov, preferred_element_type=F32)
           + jnp.dot(p_lo, ov, preferred_element_type=F32))

    nsel = LANES
    sp = lax.broadcasted_iota(jnp.int32, (qb, nsel), 1)
    rr = lax.broadcasted_iota(jnp.int32, (qb, nsel), 0)
    cur = (nsel - 2) + (rr >= SLC_BLOCK).astype(jnp.int32)
    s_abs = sp + (2 * i + 2 - nsel)
    valid = s_abs >= 0
    forced = ((s_abs == 0) | (sp == cur) | (sp == cur - 1)) & valid
    excluded = forced | (sp > cur) | (~valid)
    cand_t = jnp.where(excluded, NEG, imp).T
    v = cand_t
    tau = None
    for _ in range(SLC_TOPN - N_FORCED):
        tau = jnp.max(v, axis=0, keepdims=True)
        v = jnp.where(v >= tau, 3.0 * NEG, v)
    picked = jnp.where((cand_t >= tau) & (cand_t > 0.5 * NEG), 1.0, 0.0).T
    sel_neg = jnp.where((picked > 0.5) | forced, 0.0, NEG)
    sel_neg = pltpu.roll(sel_neg, shift=(2 * i + 2) % nsel, axis=1).astype(qa_ref.dtype)
    for r in range(NSA_REP):
        qa_ref[qb * r:qb * (r + 1), LANES:2 * LANES] = sel_neg
    q_ext = qa_ref[...]

    kt = SEL_TILE

    def tile_start(dd):
        return pl.multiple_of(KEY_PAD + qb * (i + 1) - kt * (dd + 1), qb)

    def tile_scores(dd):
        return _dot_nt(q_ext, ks_ref[0, 0, pl.ds(tile_start(dd), kt), :])

    def tile_softmax(s, dd):
        m_t = jnp.max(s, axis=-1, keepdims=True)
        return m_t, _dot(jnp.exp(s - m_t), vs_ref[0, 0, pl.ds(tile_start(dd), kt), :])

    def near_tile(dd, carry):
        m_old, acc = carry
        s = tile_scores(dd)
        s = jnp.concatenate([s[qb * r:qb * (r + 1)] + tbm_ref[0, r, dd]
                             for r in range(NSA_REP)], axis=0)
        m_t, pv = tile_softmax(s, dd)
        m_new = jnp.maximum(m_old, m_t)
        return m_new, jnp.exp(m_old - m_new) * acc + jnp.exp(m_t - m_new) * pv

    def far_pair(jj, carry):
        m_old, acc = carry
        dd_a = NEAR_TILES + 2 * jj
        dd_b = dd_a + 1
        s_a = tile_scores(dd_a)
        s_b = tile_scores(dd_b)
        m_a, pv_a = tile_softmax(s_a, dd_a)
        m_b, pv_b = tile_softmax(s_b, dd_b)
        m_new = jnp.maximum(m_old, jnp.maximum(m_a, m_b))
        acc = (jnp.exp(m_old - m_new) * acc + jnp.exp(m_a - m_new) * pv_a
               + jnp.exp(m_b - m_new) * pv_b)
        return m_new, acc

    n_tiles = (i * qb + qb + kt - 1) // kt
    carry = (jnp.full((rows, 1), NEG, F32), jnp.zeros((rows, 2 * HEAD_DIM), F32))
    carry = lax.fori_loop(0, jnp.minimum(n_tiles, NEAR_TILES), near_tile, carry)
    carry = lax.fori_loop(0, (jnp.maximum(n_tiles - NEAR_TILES, 0) + 1) // 2, far_pair, carry)
    acc_s = carry[1]

    gt = gate_ref[0]
    for r in range(NSA_REP):
        rs = slice(qb * r, qb * (r + 1))
        o_s = acc_s[rs, 0:HEAD_DIM] / acc_s[rs, HEAD_DIM:HEAD_DIM + 1]
        o_w = acc_w[rs, 0:HEAD_DIM] / acc_w[rs, HEAD_DIM:HEAD_DIM + 1]
        o = (gt[:, 3 * r + 0:3 * r + 1] * o_c[r] + gt[:, 3 * r + 1:3 * r + 2] * o_s
             + gt[:, 3 * r + 2:3 * r + 3] * o_w)
        o_ref[0, :, HEAD_DIM * r:HEAD_DIM * (r + 1)] = o.astype(o_ref.dtype)


def _nsa_attention(qn, gate, kcv, ksa, vsp, kwa, vwp, tbm, wbm, cbm, qmid, ov):
    b_, t_, _ = qn.shape
    tp = ksa.shape[2]
    ncp = kcv.shape[3]
    gw = NSA_REP * HEAD_DIM
    assert t_ // SLC_BLOCK <= LANES
    kvspec = lambda wd: pl.BlockSpec((1, 1, tp, wd), lambda b, g, i: (b, g, 0, 0))
    return pl.pallas_call(
        _nsa_kernel,
        grid=(b_, NSA_KV, t_ // Q_BLOCK),
        in_specs=[
            pl.BlockSpec((1, Q_BLOCK, gw), lambda b, g, i: (b, i, g)),
            pl.BlockSpec((1, Q_BLOCK, LANES), lambda b, g, i: (b, i, g)),
            pl.BlockSpec((1, 1, 1, ncp, LANES), lambda b, g, i: (b, 0, g, 0, 0)),
            pl.BlockSpec((1, 1, 1, ncp, LANES), lambda b, g, i: (b, 1, g, 0, 0)),
            kvspec(2 * LANES), kvspec(LANES), kvspec(LANES), kvspec(LANES),
            pl.BlockSpec((1, NSA_REP, NEAR_TILES, Q_BLOCK, SEL_TILE), lambda b, g, i: (g, 0, 0, 0, 0)),
            pl.BlockSpec((1, NSA_REP, Q_BLOCK, WIN_KEYS), lambda b, g, i: (g, 0, 0, 0)),
            pl.BlockSpec((1, NSA_REP, Q_BLOCK, CMP_PAD), lambda b, g, i: (g, 0, 0, 0)),
            pl.BlockSpec((1, NSA_REP, 1, LANES), lambda b, g, i: (g, 0, 0, 0)),
            pl.BlockSpec((CMP_PAD, LANES), lambda b, g, i: (0, 0)),
        ],
        out_specs=pl.BlockSpec((1, Q_BLOCK, gw), lambda b, g, i: (b, i, g)),
        out_shape=jax.ShapeDtypeStruct((b_, t_, D_NSA), F32),
        scratch_shapes=[pltpu.VMEM((NSA_REP * Q_BLOCK, 2 * LANES), MXU_DTYPE)],
        compiler_params=_params(("parallel", "parallel", "arbitrary")),
        name="nsa_attention",
    )(qn, gate, kcv, kcv, ksa, vsp, kwa, vwp, tbm, wbm, cbm, qmid, ov)


def _mix_kernel(x_ref, on_ref, gu_ref, vn_ref, sgw_ref, sgb_ref, gn_ref, gs_ref, wo_ref,
                fg_ref, wq_ref, x1_out, h2t_out, qp_out):
    tm = x_ref.shape[0]
    tri = (lax.broadcasted_iota(jnp.int32, (SG_CHUNK, SG_CHUNK), 0)
           >= lax.broadcasted_iota(jnp.int32, (SG_CHUNK, SG_CHUNK), 1))
    sgb = sgb_ref[...]
    y = x_ref[...] + _dot(_rms(on_ref[...], gn_ref[...]), wo_ref[0:D_NSA, :])
    parts = []
    for c in range(tm // SG_CHUNK):
        cs = slice(SG_CHUNK * c, SG_CHUNK * (c + 1))
        heads = []
        for hh in range(SG_HEADS):
            sl = slice(SG_DIM * hh, SG_DIM * (hh + 1))
            w = jnp.where(tri, sgw_ref[hh], 0.0)
            mixed = _dot(w, vn_ref[cs, sl]) + sgb[:, hh:hh + 1]
            heads.append(gu_ref[cs, sl].astype(F32) * mixed)
        parts.append(jnp.concatenate(heads, axis=1))
    o_sg = jnp.concatenate(parts, axis=0)
    y = y + _dot(_rms(o_sg, gs_ref[...]), wo_ref[D_NSA:D_NSA + D_SG, :])
    x1_out[...] = y
    h2 = _rms(y, fg_ref[...])
    h2t_out[...] = h2.T.astype(h2t_out.dtype)
    qp_out[...] = _dot(h2, wq_ref[...])


def _mix(x2, onsa, gu, vn, sgw, sgb_t, gn, gs, wo, fg, wq, tm):
    n_, d_ = x2.shape
    row = lambda wd: pl.BlockSpec((tm, wd), lambda i: (i, 0))
    const = lambda shape, **kw: pl.BlockSpec(shape, lambda i: (0,) * len(shape), **kw)
    return pl.pallas_call(
        _mix_kernel,
        grid=(n_ // tm,),
        in_specs=[
            row(d_), row(D_NSA), row(D_SG), row(D_SG),
            const((SG_HEADS, SG_CHUNK, SG_CHUNK)), const((SG_CHUNK, SG_HEADS)),
            const((1, D_NSA)), const((1, D_SG)),
            const((D_NSA + D_SG, d_), pipeline_mode=pl.Buffered(1)),
            const((1, d_)),
            const((d_, wq.shape[1]), pipeline_mode=pl.Buffered(1)),
        ],
        out_specs=[row(d_), pl.BlockSpec((d_, tm), lambda i: (0, i)), row(wq.shape[1])],
        out_shape=[
            jax.ShapeDtypeStruct((n_, d_), F32),
            jax.ShapeDtypeStruct((d_, n_), MXU_DTYPE),
            jax.ShapeDtypeStruct((n_, wq.shape[1]), F32),
        ],
        compiler_params=_params(("parallel",)),
        name="mix_out_proj",
    )(x2, onsa, gu, vn, sgw, sgb_t, gn, gs, wo, fg, wq)


def _top_rows(v, k):
    tops = []
    for _ in range(k):
        m = jnp.max(v, axis=0, keepdims=True)
        tops.append(m)
        v = jnp.where(v >= m, 0.0, v)
    return jnp.concatenate(tops, axis=0)


def _pair_products(a, b):
    cand = [a[0:1] * b]
    for ra in range(1, 8):
        cand.append(a[ra:ra + 1] * b[0:8])
    cand.append(a[8:16] * b[0:1])
    return jnp.concatenate(cand, axis=0)


def _peer_topk_kernel(qp_ref, sk_ref, e1_out, e2_out, eps_out):
    for hh in range(PEER_HEADS):
        base = 2 * PEER_HALF * hh
        s1 = _dot_nt(sk_ref[0], qp_ref[:, base:base + PEER_HALF])
        s2 = _dot_nt(sk_ref[1], qp_ref[:, base + PEER_HALF:base + 2 * PEER_HALF])
        e1 = jnp.exp(s1 - jnp.max(s1, axis=0, keepdims=True))
        e2 = jnp.exp(s2 - jnp.max(s2, axis=0, keepdims=True))
        a = _top_rows(e1, PEER_TOPK)
        b = _top_rows(e2, PEER_TOPK)
        z = jnp.sum(_top_rows(_pair_products(a, b), PEER_TOPK), axis=0, keepdims=True)
        zinv = 1.0 / z
        gates = _top_rows(_pair_products(a * zinv, b), PEER_TOPK)
        e1_out[hh] = e1 * zinv
        for j in range(e2.shape[1] // LANES):
            e2_out[hh, j] = e2[:, LANES * j:LANES * (j + 1)]
        eps_out[hh:hh + 1, :] = gates[PEER_TOPK - 1:PEER_TOPK]


def _peer_topk(qp, sub_keys, tm):
    n_, qd = qp.shape
    return pl.pallas_call(
        _peer_topk_kernel,
        grid=(n_ // tm,),
        in_specs=[
            pl.BlockSpec((tm, qd), lambda i: (i, 0)),
            pl.BlockSpec((2, PEER_NKEYS, PEER_HALF), lambda i: (0, 0, 0)),
        ],
        out_specs=[
            pl.BlockSpec((PEER_HEADS, PEER_NKEYS, tm), lambda i: (0, 0, i)),
            pl.BlockSpec((PEER_HEADS, tm // LANES, PEER_NKEYS, LANES), lambda i: (0, i, 0, 0)),
            pl.BlockSpec((PEER_HEADS, tm), lambda i: (0, i)),
        ],
        out_shape=[
            jax.ShapeDtypeStruct((PEER_HEADS, PEER_NKEYS, n_), F32),
            jax.ShapeDtypeStruct((PEER_HEADS, n_ // LANES, PEER_NKEYS, LANES), F32),
            jax.ShapeDtypeStruct((PEER_HEADS, n_), F32),
        ],
        compiler_params=_params(("parallel",)),
        name="peer_topk",
    )(qp, sub_keys)


def _peer_mix_kernel(ht_ref, u_ref, vt_ref, e1_ref, e2_ref, eps_ref, x1_ref, out_ref,
                     acc_ref, crow_ref, a0_ref, a1_ref, z0_ref, z1_ref):
    ie = pl.program_id(1)
    te = u_ref.shape[0]
    tm = ht_ref.shape[1]
    n_tiles = pl.num_programs(1) - 2
    per_tile = te // PEER_NKEYS

    @pl.when(ie == 0)
    def _():
        acc_ref[...] = jnp.zeros_like(acc_ref)
        a1_ref[...] = jnp.zeros_like(a1_ref)
        z0_ref[...] = jnp.zeros_like(z0_ref)

    d_ = vt_ref.shape[0]
    n_lane = tm // LANES
    n_blocks = per_tile * n_lane
    a_units = [(r, c, kc) for r in range(te // MXU_ROWS) for c in range(tm // MXU_ROWS)
               for kc in range(d_ // A_KCHUNK)]
    c_units = [(r, c) for r in range(d_ // MXU_ROWS) for c in range(tm // MXU_ROWS)]

    def step(a_w, a_r, z_w, z_r):
        tile = jnp.clip(ie - 1, 0, n_tiles - 1)

        def a_unit(r, c, kc):
            rs = slice(MXU_ROWS * r, MXU_ROWS * (r + 1))
            cs = slice(MXU_ROWS * c, MXU_ROWS * (c + 1))
            ds = slice(A_KCHUNK * kc, A_KCHUNK * (kc + 1))
            res = jnp.dot(u_ref[rs, ds], ht_ref[ds, cs], preferred_element_type=F32)
            for jj in range(MXU_ROWS // LANES):
                j = c * (MXU_ROWS // LANES) + jj
                part = res[:, LANES * jj:LANES * (jj + 1)]
                if kc == 0:
                    a_w[j, rs, :] = part
                else:
                    a_w[j, rs, :] += part

        def c_unit(r, c):
            rs = slice(MXU_ROWS * r, MXU_ROWS * (r + 1))
            cs = slice(MXU_ROWS * c, MXU_ROWS * (c + 1))
            acc_ref[rs, cs] += jnp.dot(vt_ref[rs, :], z_r[:, cs], preferred_element_type=F32)

        def b_block(k, j):
            ls = slice(LANES * j, LANES * (j + 1))
            for part in range(PEER_NKEYS // GATE_ROWS):
                ks = slice(GATE_ROWS * part, GATE_ROWS * (part + 1))
                rs = slice(PEER_NKEYS * k + GATE_ROWS * part, PEER_NKEYS * k + GATE_ROWS * (part + 1))
                g = jnp.zeros((GATE_ROWS, LANES), F32)
                for hh in range(PEER_HEADS):
                    row = PEER_HEADS * k + hh
                    gate = e2_ref[hh, j, ks, :] * crow_ref[row:row + 1, ls]
                    g = g + jnp.where(gate >= eps_ref[hh:hh + 1, ls], gate, 0.0)
                z_w[rs, ls] = (_gelu(a_r[j, rs, :]) * g).astype(z_w.dtype)

        for k in range(per_tile):
            i1 = tile * per_tile + k
            for hh in range(PEER_HEADS):
                row = PEER_HEADS * k + hh
                crow_ref[row:row + 1, :] = e1_ref[hh, pl.ds(i1, 1), :]

        a_iter = iter(a_units)
        c_iter = iter(c_units)
        a_every = max(n_blocks // len(a_units), 1)
        c_per = -(-len(c_units) // n_blocks)
        for blk in range(n_blocks):
            k, j = divmod(blk, n_lane)
            if blk % a_every == 0:
                unit = next(a_iter, None)
                if unit is not None:
                    a_unit(*unit)
            for _ in range(c_per):
                unit = next(c_iter, None)
                if unit is not None:
                    c_unit(*unit)
            b_block(k, j)
        for unit in a_iter:
            a_unit(*unit)
        for unit in c_iter:
            c_unit(*unit)

    @pl.when(ie % 2 == 0)
    def _():
        step(a0_ref, a1_ref, z1_ref, z0_ref)

    @pl.when(ie % 2 == 1)
    def _():
        step(a1_ref, a0_ref, z0_ref, z1_ref)

    @pl.when(ie == pl.num_programs(1) - 1)
    def _():
        out_ref[...] = x1_ref[...] + acc_ref[...].T


def _peer_mix(h2t, u, vt, e1t, e2t, eps, x1, tm, te):
    d_, n_ = h2t.shape
    n_tiles = u.shape[0] // te
    return pl.pallas_call(
        _peer_mix_kernel,
        grid=(n_ // tm, n_tiles + 2),
        in_specs=[
            pl.BlockSpec((d_, tm), lambda it, ie: (0, it)),
            pl.BlockSpec((te, d_), lambda it, ie: (jnp.minimum(ie, n_tiles - 1), 0)),
            pl.BlockSpec((d_, te), lambda it, ie: (0, jnp.maximum(ie - 2, 0))),
            pl.BlockSpec((PEER_HEADS, PEER_NKEYS, tm), lambda it, ie: (0, 0, it)),
            pl.BlockSpec((PEER_HEADS, tm // LANES, PEER_NKEYS, LANES), lambda it, ie: (0, it, 0, 0)),
            pl.BlockSpec((PEER_HEADS, tm), lambda it, ie: (0, it)),
            pl.BlockSpec((tm, d_), lambda it, ie: (it, 0)),
        ],
        out_specs=pl.BlockSpec((tm, d_), lambda it, ie: (it, 0)),
        out_shape=jax.ShapeDtypeStruct((n_, d_), F32),
        scratch_shapes=[
            pltpu.VMEM((d_, tm), F32),
            pltpu.VMEM((PEER_HEADS * (te // PEER_NKEYS), tm), F32),
            pltpu.VMEM((tm // LANES, te, LANES), F32), pltpu.VMEM((tm // LANES, te, LANES), F32),
            pltpu.VMEM((te, tm), MXU_DTYPE), pltpu.VMEM((te, tm), MXU_DTYPE),
        ],
        compiler_params=_params(("parallel", "arbitrary")),
        name="peer_mix",
    )(h2t, u, vt, e1t, e2t, eps, x1)


def _rel_bucket(dist):
    n = jnp.maximum(dist, 0)
    max_exact = REL_BUCKETS // 2
    nf = jnp.maximum(n, 1).astype(F32)
    large = max_exact + (jnp.log(nf / max_exact) / math.log(REL_MAX_DIST / max_exact)
                         * (REL_BUCKETS - max_exact)).astype(jnp.int32)
    large = jnp.minimum(large, REL_BUCKETS - 1)
    return jnp.where(n < max_exact, n, large)


def _toeplitz(btab, off, rows, cols):
    per = rows + cols
    k = np.concatenate([np.arange(cols), np.zeros(1, np.int64), np.arange(-(rows - 1), 0)])
    idx = np.clip(off - k, 0, btab.shape[1] - 1)
    w = jnp.take(btab, jnp.asarray(idx, jnp.int32), axis=1)
    x = jnp.tile(w, (1, rows))[:, :rows * (per - 1)]
    return x.reshape(btab.shape[0], rows, per - 1)[:, :, :cols]


def _bias_tiles(rel_bias):
    dist = jnp.arange(BIAS_TABLE, dtype=jnp.int32)
    btab = rel_bias.astype(F32)[_rel_bucket(dist)].T
    tbw = jnp.stack([_toeplitz(btab, SEL_TILE * (dd + 1) - Q_BLOCK, Q_BLOCK, SEL_TILE)
                     for dd in range(NEAR_TILES)], axis=1)
    wb = _toeplitz(btab, WINDOW, Q_BLOCK, WIN_KEYS)
    off = CMP_STRIDE * (LANES - 8) - (CMP_BLOCK - 1)
    front = CMP_STRIDE * LANES - off
    length = CMP_STRIDE * (LANES + 8 + 2)
    padded = jnp.pad(btab, ((0, 0), (front, length - front - BIAS_TABLE)))
    ch = padded.reshape(btab.shape[0], length // CMP_STRIDE, CMP_STRIDE)
    nwin = LANES + 1
    win = jnp.stack([ch[:, a:a + nwin] for a in range(Q_BLOCK // CMP_STRIDE)], axis=2)
    win = win.reshape(btab.shape[0], nwin, Q_BLOCK)
    cb = jnp.swapaxes(win[:, 1:nwin][:, ::-1], 1, 2)
    cfar = rel_bias.astype(F32)[REL_BUCKETS - 1]
    grp = lambda a: a.reshape((NSA_KV, NSA_REP) + a.shape[1:])

    r = np.arange(Q_BLOCK)[:, None]
    near_ok = np.stack([r - np.arange(SEL_TILE)[None, :] + SEL_TILE * (dd + 1) - Q_BLOCK >= 0
                        for dd in range(NEAR_TILES)])
    tbm = jnp.where(near_ok, tbw - cfar[:, None, None, None], NEG)
    dist_w = r - np.arange(WIN_KEYS)[None, :] + WINDOW
    wbm = jnp.where((dist_w >= 0) & (dist_w < WINDOW), wb, NEG)
    dist_c = r - CMP_STRIDE * np.arange(CMP_PAD)[None, :] + (CMP_STRIDE * CMP_PAD - Q_BLOCK - (CMP_BLOCK - 1))
    cbias = jnp.concatenate(
        [jnp.broadcast_to(cfar[:, None, None], (NSA_HEADS, Q_BLOCK, CMP_PAD - LANES)), cb], axis=2)
    cbm = jnp.where(dist_c >= 0, cbias, NEG)

    hi = cfar.astype(BF16).astype(F32)
    lo = (cfar - hi).astype(BF16).astype(F32)
    qmid = jnp.zeros((NSA_HEADS, 1, LANES), F32)
    qmid = qmid.at[:, 0, FAR_HI_LANE - HEAD_DIM].set(hi).at[:, 0, FAR_LO_LANE - HEAD_DIM].set(lo)
    qmid = qmid.at[:, 0, PAD_FLAG_LANE - HEAD_DIM].set(NEG)
    return grp(tbm), grp(wbm), grp(cbm), grp(qmid)


def _extend_keys(ks, kw):
    tp = ks.shape[2] + KEY_PAD
    pos = np.arange(tp) - KEY_PAD
    real = pos >= 0
    mid_s = np.zeros((tp, LANES - HEAD_DIM), np.float32)
    mid_s[:, FAR_HI_LANE - HEAD_DIM] = real
    mid_s[:, FAR_LO_LANE - HEAD_DIM] = real
    mid_s[:, PAD_FLAG_LANE - HEAD_DIM] = ~real
    onehot = ((pos[:, None] // SLC_BLOCK) == np.arange(LANES)[None, :]) & real[:, None]
    mid_w = np.zeros((tp, LANES - HEAD_DIM), np.float32)
    mid_w[:, PAD_FLAG_LANE - HEAD_DIM] = ~real
    padk = lambda a: jnp.pad(a, ((0, 0), (0, 0), (KEY_PAD, 0), (0, 0)))
    bc = lambda a: jnp.broadcast_to(jnp.asarray(a, ks.dtype), ks.shape[:2] + a.shape)
    ksa = jnp.concatenate([padk(ks), bc(mid_s), bc(onehot.astype(np.float32))], axis=-1)
    kwa = jnp.concatenate([padk(kw), bc(mid_w)], axis=-1)
    return ksa, kwa


def _overlap_matrix():
    c = np.arange(CMP_PAD)[:, None]
    s = np.arange(LANES)[None, :]
    per = SLC_BLOCK // CMP_STRIDE
    ov = (s == c // per) | ((c % per == per - 1) & (s == c // per + 1))
    return jnp.asarray(ov.astype(np.float32), BF16)


def kernel(x, attn_norm_g, w_in, q_norm_g, k_norm_g, cmp_pe_k, cmp_w1_k, cmp_w2_k, cmp_pe_v,
           cmp_w1_v, cmp_w2_v, rel_bias, sg_ln_g, sg_ln_b, sg_w, sg_b, out_norm_nsa, out_norm_sg,
           w_out, ffn_norm_g, peer_w_query, peer_sub_keys, peer_u, peer_v):
    b_, t_, d_ = x.shape
    n_ = b_ * t_
    assert t_ % Q_BLOCK == 0 and t_ >= WIN_KEYS and d_ == D_NSA + D_SG

    c_kv = D_NSA + 6 * D_KV
    c_gate = c_kv + 3 * NSA_HEADS
    wg = w_in[:, c_kv:c_gate].reshape(d_, NSA_KV, 3 * NSA_REP)
    wg = jnp.pad(wg, ((0, 0), (0, 0), (0, LANES - 3 * NSA_REP))).reshape(d_, NSA_KV * LANES)
    w_attn = jnp.concatenate([w_in[:, :c_kv], wg], axis=1).astype(MXU_DTYPE)
    w_sgu = w_in[:, c_gate:].astype(MXU_DTYPE)
    ng = attn_norm_g.reshape(1, d_)

    qn, kcv, ks, vs, kw, vw, gate = _attn_proj(x, ng, w_attn, q_norm_g.reshape(1, HEAD_DIM),
                                               k_norm_g, tm=256)
    gu, vn = _sgu_proj(x, ng, w_sgu, sg_ln_g, sg_ln_b, tm=256)

    nch = t_ // CMP_STRIDE
    xc = kcv.reshape(b_, nch, CMP_STRIDE, 2, NSA_KV, HEAD_DIM).transpose(0, 3, 4, 1, 2, 5)
    xc = xc.reshape(b_, 2, NSA_KV, nch, CMP_STRIDE * HEAD_DIM)
    w1 = jnp.stack([cmp_w1_k, cmp_w1_v]).astype(MXU_DTYPE)
    w2 = jnp.stack([cmp_w2_k, cmp_w2_v]).astype(MXU_DTYPE)
    pe = jnp.stack([cmp_pe_k, cmp_pe_v]).transpose(0, 2, 1, 3).reshape(2, NSA_KV, 1, CMP_BLOCK * HEAD_DIM)
    kcv_c = _compress(xc, w1, w2, pe.astype(MXU_DTYPE), k_norm_g)

    padk = lambda a: jnp.pad(a, ((0, 0), (0, 0), (KEY_PAD, 0), (0, 0)))
    tbm, wbm, cbm, qmid = _bias_tiles(rel_bias)
    ksa, kwa = _extend_keys(ks, kw)
    onsa = _nsa_attention(qn, gate, kcv_c, ksa, padk(vs), kwa, padk(vw),
                          tbm, wbm, cbm, qmid, _overlap_matrix())

    x1, h2t, qp = _mix(x.reshape(n_, d_), onsa.reshape(n_, D_NSA), gu.reshape(n_, D_SG),
                      vn.reshape(n_, D_SG), sg_w, sg_b.T, out_norm_nsa.reshape(1, D_NSA),
                      out_norm_sg.reshape(1, D_SG), w_out.astype(MXU_DTYPE),
                      ffn_norm_g.reshape(1, d_), peer_w_query.astype(MXU_DTYPE), tm=256)

    e1t, e2t, eps = _peer_topk(qp, peer_sub_keys, tm=256)
    out = _peer_mix(h2t, peer_u.astype(MXU_DTYPE), peer_v.T.astype(MXU_DTYPE), e1t, e2t, eps, x1,
                    tm=min(512, n_), te=512)
    return out.reshape(b_, t_, d_)
```

```python
import functools
import math

import jax
import jax.numpy as jnp
import numpy as np
from jax import lax
from jax.experimental import pallas as pl
from jax.experimental.pallas import tpu as pltpu

F32 = jnp.float32
BF16 = jnp.bfloat16
MXU_DTYPE = BF16

EPS = 1e-6
NEG = -1e30
SQRT_HALF = 0.7071067811865476
LOG2E = 1.4426950408889634

NSA_HEADS = 16
NSA_KV = 4
NSA_REP = 4
HEAD_DIM = 64
D_NSA = NSA_HEADS * HEAD_DIM
D_KV = NSA_KV * HEAD_DIM
CMP_STRIDE = 16
CMP_BLOCK = 32
CMP_HIDDEN = 128
SLC_BLOCK = 64
SLC_TOPN = 16
WINDOW = 512
Q_BLOCK = 128
SG_HEADS = 8
SG_DIM = 128
SG_CHUNK = 128
D_SG = SG_HEADS * SG_DIM
REL_BUCKETS = 32
REL_MAX_DIST = 1024
PEER_HEADS = 8
PEER_NKEYS = 128
PEER_HALF = 128
PEER_TOPK = 16

LANES = 128
MXU_ROWS = 256
A_KCHUNK = 1024
GATE_ROWS = 64
KEY_PAD = 1024
CMP_PAD = 512
SEL_TILE = 512
WIN_KEYS = WINDOW + Q_BLOCK
NEAR_TILES = 2
FAR_GROUP = 4
BIAS_TABLE = 2048
FAR_DIST = 897
FAR_HI_LANE = 64
FAR_LO_LANE = 65
PAD_FLAG_LANE = 66
N_FORCED = 3
VMEM_LIMIT = 56 * 1024 * 1024


def _gelu(x):
    return 0.5 * x * (1.0 + lax.erf(x * SQRT_HALF))


def _dot(a, b):
    return jnp.dot(a.astype(MXU_DTYPE), b.astype(MXU_DTYPE), preferred_element_type=F32)


def _dot_nt(a, b):
    return lax.dot_general(a.astype(MXU_DTYPE), b.astype(MXU_DTYPE),
                           (((1,), (1,)), ((), ())), preferred_element_type=F32)


def _rms(x, g):
    ms = jnp.mean(x * x, axis=-1, keepdims=True)
    return x * lax.rsqrt(ms + EPS) * g


def _params(sem):
    return pltpu.CompilerParams(dimension_semantics=sem, vmem_limit_bytes=VMEM_LIMIT)


def _attn_proj_kernel(x_ref, ng_ref, w_ref, qg_ref, kg_ref,
                      q_out, kcv_out, ks_out, vs_out, kw_out, vw_out, gate_out):
    h = _rms(x_ref[0], ng_ref[...]).astype(MXU_DTYPE)
    tm = h.shape[0]

    pq = _dot(h, w_ref[:, 0:D_NSA])
    qg = qg_ref[...]
    for hh in range(NSA_HEADS):
        sl = slice(HEAD_DIM * hh, HEAD_DIM * (hh + 1))
        q_out[0, :, sl] = (_rms(pq[:, sl], qg) * (HEAD_DIM ** -0.5 * LOG2E)).astype(q_out.dtype)

    kcv_out[0] = _dot(h, w_ref[:, D_NSA:D_NSA + 2 * D_KV]).astype(kcv_out.dtype)

    base = D_NSA + 2 * D_KV
    p4 = _dot(h, w_ref[:, base:base + 4 * D_KV])
    ones_col = (lax.broadcasted_iota(jnp.int32, (tm, HEAD_DIM), 1) == 0).astype(F32)
    for g in range(NSA_KV):
        sl = slice(HEAD_DIM * g, HEAD_DIM * (g + 1))
        ks_out[0, g] = _rms(p4[:, 0 * D_KV:1 * D_KV][:, sl], kg_ref[1:2, :]).astype(ks_out.dtype)
        kw_out[0, g] = _rms(p4[:, 2 * D_KV:3 * D_KV][:, sl], kg_ref[2:3, :]).astype(kw_out.dtype)
        vs_out[0, g, :, 0:HEAD_DIM] = p4[:, 1 * D_KV:2 * D_KV][:, sl].astype(vs_out.dtype)
        vs_out[0, g, :, HEAD_DIM:2 * HEAD_DIM] = ones_col.astype(vs_out.dtype)
        vw_out[0, g, :, 0:HEAD_DIM] = p4[:, 3 * D_KV:4 * D_KV][:, sl].astype(vw_out.dtype)
        vw_out[0, g, :, HEAD_DIM:2 * HEAD_DIM] = ones_col.astype(vw_out.dtype)

    base = base + 4 * D_KV
    gate_out[0] = jax.nn.sigmoid(_dot(h, w_ref[:, base:base + NSA_KV * LANES]))


def _attn_proj(x, ng, w, qg, kg, tm):
    b_, t_, d_ = x.shape
    nw = w.shape[1]
    kv4 = lambda wd: pl.BlockSpec((1, NSA_KV, tm, wd), lambda b, i: (b, 0, i, 0))
    return pl.pallas_call(
        _attn_proj_kernel,
        grid=(b_, t_ // tm),
        in_specs=[
            pl.BlockSpec((1, tm, d_), lambda b, i: (b, i, 0)),
            pl.BlockSpec((1, d_), lambda b, i: (0, 0)),
            pl.BlockSpec((d_, nw), lambda b, i: (0, 0), pipeline_mode=pl.Buffered(1)),
            pl.BlockSpec((1, HEAD_DIM), lambda b, i: (0, 0)),
            pl.BlockSpec((3, HEAD_DIM), lambda b, i: (0, 0)),
        ],
        out_specs=[
            pl.BlockSpec((1, tm, D_NSA), lambda b, i: (b, i, 0)),
            pl.BlockSpec((1, tm, 2 * D_KV), lambda b, i: (b, i, 0)),
            kv4(HEAD_DIM), kv4(2 * HEAD_DIM), kv4(HEAD_DIM), kv4(2 * HEAD_DIM),
            pl.BlockSpec((1, tm, NSA_KV * LANES), lambda b, i: (b, i, 0)),
        ],
        out_shape=[
            jax.ShapeDtypeStruct((b_, t_, D_NSA), MXU_DTYPE),
            jax.ShapeDtypeStruct((b_, t_, 2 * D_KV), MXU_DTYPE),
            jax.ShapeDtypeStruct((b_, NSA_KV, t_, HEAD_DIM), MXU_DTYPE),
            jax.ShapeDtypeStruct((b_, NSA_KV, t_, 2 * HEAD_DIM), MXU_DTYPE),
            jax.ShapeDtypeStruct((b_, NSA_KV, t_, HEAD_DIM), MXU_DTYPE),
            jax.ShapeDtypeStruct((b_, NSA_KV, t_, 2 * HEAD_DIM), MXU_DTYPE),
            jax.ShapeDtypeStruct((b_, t_, NSA_KV * LANES), F32),
        ],
        compiler_params=_params(("parallel", "parallel")),
        name="attn_proj",
    )(x, ng, w, qg, kg)


def _sgu_proj_kernel(x_ref, ng_ref, w_ref, lng_ref, lnb_ref, gu_out, vn_out):
    h = _rms(x_ref[0], ng_ref[...]).astype(MXU_DTYPE)
    gu_out[0] = _gelu(_dot(h, w_ref[:, 0:D_SG])).astype(gu_out.dtype)
    gv = _gelu(_dot(h, w_ref[:, D_SG:2 * D_SG]))
    for hh in range(SG_HEADS):
        sl = slice(SG_DIM * hh, SG_DIM * (hh + 1))
        v = gv[:, sl]
        mu = jnp.mean(v, axis=-1, keepdims=True)
        var = jnp.mean(jnp.square(v - mu), axis=-1, keepdims=True)
        y = (v - mu) * lax.rsqrt(var + EPS) * lng_ref[hh:hh + 1, :] + lnb_ref[hh:hh + 1, :]
        vn_out[0, :, sl] = y.astype(vn_out.dtype)


def _sgu_proj(x, ng, w, lng, lnb, tm):
    b_, t_, d_ = x.shape
    return pl.pallas_call(
        _sgu_proj_kernel,
        grid=(b_, t_ // tm),
        in_specs=[
            pl.BlockSpec((1, tm, d_), lambda b, i: (b, i, 0)),
            pl.BlockSpec((1, d_), lambda b, i: (0, 0)),
            pl.BlockSpec((d_, 2 * D_SG), lambda b, i: (0, 0), pipeline_mode=pl.Buffered(1)),
            pl.BlockSpec((SG_HEADS, SG_DIM), lambda b, i: (0, 0)),
            pl.BlockSpec((SG_HEADS, SG_DIM), lambda b, i: (0, 0)),
        ],
        out_specs=[
            pl.BlockSpec((1, tm, D_SG), lambda b, i: (b, i, 0)),
            pl.BlockSpec((1, tm, D_SG), lambda b, i: (b, i, 0)),
        ],
        out_shape=[
            jax.ShapeDtypeStruct((b_, t_, D_SG), MXU_DTYPE),
            jax.ShapeDtypeStruct((b_, t_, D_SG), MXU_DTYPE),
        ],
        compiler_params=_params(("parallel", "parallel")),
        name="sgu_proj",
    )(x, ng, w, lng, lnb)


def _compress_kernel(x_ref, w1_ref, w2_ref, pe_ref, kg_ref, out_ref):
    which = pl.program_id(1)
    x = x_ref[0, 0, 0]
    nch = x.shape[0]
    half = CMP_STRIDE * HEAD_DIM
    a = _dot(x, w1_ref[0, 0, 0:half, :])
    bm = _dot(x, w1_ref[0, 0, half:2 * half, :])
    pec = _dot(pe_ref[0, 0], w1_ref[0, 0])
    hid = _gelu(a + pltpu.roll(bm, shift=nch - 1, axis=0) + pec)
    y = _dot(hid, w2_ref[0, 0])
    y = jnp.where(which == 0, _rms(y, kg_ref[0:1, :]), y)
    lane = lax.broadcasted_iota(jnp.int32, (CMP_PAD, LANES), 1)
    flag = ((lane == PAD_FLAG_LANE) & (which == 0)).astype(out_ref.dtype)
    out_ref[0, 0, 0, 0:CMP_PAD, :] = flag
    out_ref[0, 0, 0, CMP_PAD:CMP_PAD + nch, 0:HEAD_DIM] = y.astype(out_ref.dtype)
    out_ref[0, 0, 0, CMP_PAD:CMP_PAD + nch, HEAD_DIM:LANES] = jnp.zeros((nch, LANES - HEAD_DIM), out_ref.dtype)


def _compress(xc, w1, w2, pe, kg):
    b_, _, g_, nch, fl = xc.shape
    return pl.pallas_call(
        _compress_kernel,
        grid=(b_, 2, g_),
        in_specs=[
            pl.BlockSpec((1, 1, 1, nch, fl), lambda b, w, g: (b, w, g, 0, 0)),
            pl.BlockSpec((1, 1, 2 * fl, CMP_HIDDEN), lambda b, w, g: (w, g, 0, 0)),
            pl.BlockSpec((1, 1, CMP_HIDDEN, HEAD_DIM), lambda b, w, g: (w, g, 0, 0)),
            pl.BlockSpec((1, 1, 1, 2 * fl), lambda b, w, g: (w, g, 0, 0)),
            pl.BlockSpec((3, HEAD_DIM), lambda b, w, g: (0, 0)),
        ],
        out_specs=pl.BlockSpec((1, 1, 1, CMP_PAD + nch, LANES), lambda b, w, g: (b, w, g, 0, 0)),
        out_shape=jax.ShapeDtypeStruct((b_, 2, g_, CMP_PAD + nch, LANES), F32),
        compiler_params=_params(("parallel", "parallel", "parallel")),
        name="nsa_compress",
    )(xc, w1, w2, pe, kg)


def _nsa_kernel(q_ref, gate_ref, kc_ref, vc_ref, ks_ref, vs_ref, kw_ref, vw_ref,
                tbm_ref, wbm_ref, cbm_ref, qmid_ref, ov_ref, o_ref, qa_ref):
    i = pl.program_id(2)
    qb = Q_BLOCK
    rows = NSA_REP * qb
    ncmp = CMP_PAD

    qblk = q_ref[0]
    for r in range(NSA_REP):
        rs = slice(qb * r, qb * (r + 1))
        qa_ref[rs, 0:HEAD_DIM] = qblk[:, HEAD_DIM * r:HEAD_DIM * (r + 1)]
        qa_ref[rs, HEAD_DIM:LANES] = jnp.broadcast_to(
            qmid_ref[0, r][:, 0:LANES - HEAD_DIM], (qb, LANES - HEAD_DIM)).astype(qa_ref.dtype)
    q_lo = qa_ref[:, 0:LANES]

    wstart = pl.multiple_of(KEY_PAD + qb * (i + 1) - WIN_KEYS, qb)
    s_w = _dot_nt(q_lo, kw_ref[0, 0, pl.ds(wstart, WIN_KEYS), :])

    def window_probs(r):
        s = s_w[qb * r:qb * (r + 1)] + wbm_ref[0, r]
        return jnp.exp2(s - jnp.max(s, axis=-1, keepdims=True))

    cstart = pl.multiple_of(8 * i + 8, 8)
    kc = kc_ref[0, 0, 0, pl.ds(cstart, ncmp), :]
    vc = vc_ref[0, 0, 0, pl.ds(cstart, ncmp), :]
    s_c = _dot_nt(q_lo, kc)
    o_c = []
    psum = jnp.zeros((qb, ncmp), F32)
    for r in range(NSA_REP):
        s = s_c[qb * r:qb * (r + 1)] + cbm_ref[0, r]
        m = jnp.max(s, axis=-1, keepdims=True)
        p = jnp.exp2(s - m)
        l = jnp.sum(p, axis=-1, keepdims=True)
        p = p * jnp.where(m > 0.5 * NEG, 1.0 / l, 0.0)
        o_c.append(_dot(p, vc)[:, 0:HEAD_DIM])
        psum = psum + p
    p_hi = psum.astype(BF16)
    p_lo = (psum - p_hi.astype(F32)).astype(BF16)
    ov = ov_ref[...]
    imp = (jnp.dot(p_hi, ov, preferred_element_type=F32)
           + jnp.dot(p_lo, ov, preferred_element_type=F32))

    nsel = LANES
    sp = lax.broadcasted_iota(jnp.int32, (qb, nsel), 1)
    rr = lax.broadcasted_iota(jnp.int32, (qb, nsel), 0)
    cur = (nsel - 2) + (rr >= SLC_BLOCK).astype(jnp.int32)
    s_abs = sp + (2 * i + 2 - nsel)
    valid = s_abs >= 0
    forced = ((s_abs == 0) | (sp == cur) | (sp == cur - 1)) & valid
    excluded = forced | (sp > cur) | (~valid)
    cand_t = jnp.where(excluded, NEG, imp).T
    v = cand_t
    tau = None
    n_rank = SLC_TOPN - N_FORCED
    p_w = []
    for it in range(n_rank):
        tau = jnp.max(v, axis=0, keepdims=True)
        v = jnp.where(v >= tau, 3.0 * NEG, v)
        if it % (n_rank // NSA_REP) == 0 and len(p_w) < NSA_REP:
            p_w.append(window_probs(len(p_w)))
    acc_w = _dot(jnp.concatenate(p_w, axis=0), vw_ref[0, 0, pl.ds(wstart, WIN_KEYS), :])
    picked = jnp.where((cand_t >= tau) & (cand_t > 0.5 * NEG), 1.0, 0.0).T
    sel_neg = jnp.where((picked > 0.5) | forced, 0.0, NEG)
    sel_neg = pltpu.roll(sel_neg, shift=(2 * i + 2) % nsel, axis=1).astype(qa_ref.dtype)
    for r in range(NSA_REP):
        qa_ref[qb * r:qb * (r + 1), LANES:2 * LANES] = sel_neg
    q_ext = qa_ref[...]

    kt = SEL_TILE

    def tile_start(dd):
        return pl.multiple_of(KEY_PAD + qb * (i + 1) - kt * (dd + 1), qb)

    def tile_scores(dd):
        return _dot_nt(q_ext, ks_ref[0, 0, pl.ds(tile_start(dd), kt), :])

    def tile_softmax(s, dd):
        m_t = jnp.max(s, axis=-1, keepdims=True)
        return m_t, _dot(jnp.exp2(s - m_t), vs_ref[0, 0, pl.ds(tile_start(dd), kt), :])

    def merge(parts):
        m_new = functools.reduce(jnp.maximum, [m for m, _ in parts])
        acc = sum(jnp.exp2(m - m_new) * pv for m, pv in parts)
        return m_new, acc

    n_tiles = (i * qb + qb + kt - 1) // kt

    near = []
    for dd in range(NEAR_TILES):
        s = tile_scores(dd)
        near.append(jnp.concatenate([s[qb * r:qb * (r + 1)] + tbm_ref[0, r, dd]
                                     for r in range(NSA_REP)], axis=0))
    carry = merge([tile_softmax(s, dd) for dd, s in enumerate(near)])

    def far_group(first, width, carry):
        dds = [jnp.minimum(first + k, n_tiles) for k in range(width)]
        scores = [tile_scores(dd) for dd in dds]
        return merge([carry] + [tile_softmax(s, dd) for s, dd in zip(scores, dds)])

    n_far = jnp.maximum(n_tiles - NEAR_TILES, 0)
    n_groups = n_far // FAR_GROUP
    carry = lax.fori_loop(
        0, n_groups, lambda u, c: far_group(NEAR_TILES + FAR_GROUP * u, FAR_GROUP, c), carry)
    rest = NEAR_TILES + FAR_GROUP * n_groups
    carry = lax.fori_loop(
        0, (n_far - FAR_GROUP * n_groups + 1) // 2, lambda u, c: far_group(rest + 2 * u, 2, c), carry)
    acc_s = carry[1]

    gt = gate_ref[0]
    for r in range(NSA_REP):
        rs = slice(qb * r, qb * (r + 1))
        o_s = acc_s[rs, 0:HEAD_DIM] / acc_s[rs, HEAD_DIM:HEAD_DIM + 1]
        o_w = acc_w[rs, 0:HEAD_DIM] / acc_w[rs, HEAD_DIM:HEAD_DIM + 1]
        o = (gt[:, 3 * r + 0:3 * r + 1] * o_c[r] + gt[:, 3 * r + 1:3 * r + 2] * o_s
             + gt[:, 3 * r + 2:3 * r + 3] * o_w)
        o_ref[0, :, HEAD_DIM * r:HEAD_DIM * (r + 1)] = o.astype(o_ref.dtype)


def _nsa_attention(qn, gate, kcv, ksa, vsp, kwa, vwp, tbm, wbm, cbm, qmid, ov):
    b_, t_, _ = qn.shape
    tp = ksa.shape[2]
    ncp = kcv.shape[3]
    gw = NSA_REP * HEAD_DIM
    assert t_ // SLC_BLOCK <= LANES
    kvspec = lambda wd: pl.BlockSpec((1, 1, tp, wd), lambda b, g, i: (b, g, 0, 0))
    return pl.pallas_call(
        _nsa_kernel,
        grid=(b_, NSA_KV, t_ // Q_BLOCK),
        in_specs=[
            pl.BlockSpec((1, Q_BLOCK, gw), lambda b, g, i: (b, i, g)),
            pl.BlockSpec((1, Q_BLOCK, LANES), lambda b, g, i: (b, i, g)),
            pl.BlockSpec((1, 1, 1, ncp, LANES), lambda b, g, i: (b, 0, g, 0, 0)),
            pl.BlockSpec((1, 1, 1, ncp, LANES), lambda b, g, i: (b, 1, g, 0, 0)),
            kvspec(2 * LANES), kvspec(LANES), kvspec(LANES), kvspec(LANES),
            pl.BlockSpec((1, NSA_REP, NEAR_TILES, Q_BLOCK, SEL_TILE), lambda b, g, i: (g, 0, 0, 0, 0)),
            pl.BlockSpec((1, NSA_REP, Q_BLOCK, WIN_KEYS), lambda b, g, i: (g, 0, 0, 0)),
            pl.BlockSpec((1, NSA_REP, Q_BLOCK, CMP_PAD), lambda b, g, i: (g, 0, 0, 0)),
            pl.BlockSpec((1, NSA_REP, 1, LANES), lambda b, g, i: (g, 0, 0, 0)),
            pl.BlockSpec((CMP_PAD, LANES), lambda b, g, i: (0, 0)),
        ],
        out_specs=pl.BlockSpec((1, Q_BLOCK, gw), lambda b, g, i: (b, i, g)),
        out_shape=jax.ShapeDtypeStruct((b_, t_, D_NSA), F32),
        scratch_shapes=[pltpu.VMEM((NSA_REP * Q_BLOCK, 2 * LANES), MXU_DTYPE)],
        compiler_params=_params(("parallel", "parallel", "arbitrary")),
        name="nsa_attention",
    )(qn, gate, kcv, kcv, ksa, vsp, kwa, vwp, tbm, wbm, cbm, qmid, ov)


def _mix_kernel(x_ref, on_ref, gu_ref, vn_ref, sgw_ref, sgb_ref, gn_ref, gs_ref, wo_ref,
                fg_ref, wq_ref, x1_out, h2t_out, qp_out):
    tm = x_ref.shape[0]
    tri = (lax.broadcasted_iota(jnp.int32, (SG_CHUNK, SG_CHUNK), 0)
           >= lax.broadcasted_iota(jnp.int32, (SG_CHUNK, SG_CHUNK), 1))
    sgb = sgb_ref[...]
    y = x_ref[...] + _dot(_rms(on_ref[...], gn_ref[...]), wo_ref[0:D_NSA, :])
    parts = []
    for c in range(tm // SG_CHUNK):
        cs = slice(SG_CHUNK * c, SG_CHUNK * (c + 1))
        heads = []
        for hh in range(SG_HEADS):
            sl = slice(SG_DIM * hh, SG_DIM * (hh + 1))
            w = jnp.where(tri, sgw_ref[hh], 0.0)
            mixed = _dot(w, vn_ref[cs, sl]) + sgb[:, hh:hh + 1]
            heads.append(gu_ref[cs, sl].astype(F32) * mixed)
        parts.append(jnp.concatenate(heads, axis=1))
    o_sg = jnp.concatenate(parts, axis=0)
    y = y + _dot(_rms(o_sg, gs_ref[...]), wo_ref[D_NSA:D_NSA + D_SG, :])
    x1_out[...] = y
    h2 = _rms(y, fg_ref[...])
    h2t_out[...] = h2.T.astype(h2t_out.dtype)
    qp_out[...] = _dot(h2, wq_ref[...])


def _mix(x2, onsa, gu, vn, sgw, sgb_t, gn, gs, wo, fg, wq, tm):
    n_, d_ = x2.shape
    row = lambda wd: pl.BlockSpec((tm, wd), lambda i: (i, 0))
    const = lambda shape, **kw: pl.BlockSpec(shape, lambda i: (0,) * len(shape), **kw)
    return pl.pallas_call(
        _mix_kernel,
        grid=(n_ // tm,),
        in_specs=[
            row(d_), row(D_NSA), row(D_SG), row(D_SG),
            const((SG_HEADS, SG_CHUNK, SG_CHUNK)), const((SG_CHUNK, SG_HEADS)),
            const((1, D_NSA)), const((1, D_SG)),
            const((D_NSA + D_SG, d_), pipeline_mode=pl.Buffered(1)),
            const((1, d_)),
            const((d_, wq.shape[1]), pipeline_mode=pl.Buffered(1)),
        ],
        out_specs=[row(d_), pl.BlockSpec((d_, tm), lambda i: (0, i)), row(wq.shape[1])],
        out_shape=[
            jax.ShapeDtypeStruct((n_, d_), F32),
            jax.ShapeDtypeStruct((d_, n_), MXU_DTYPE),
            jax.ShapeDtypeStruct((n_, wq.shape[1]), F32),
        ],
        compiler_params=_params(("parallel",)),
        name="mix_out_proj",
    )(x2, onsa, gu, vn, sgw, sgb_t, gn, gs, wo, fg, wq)


def _top_rows(v, k):
    tops = []
    for _ in range(k):
        m = jnp.max(v, axis=0, keepdims=True)
        tops.append(m)
        v = jnp.where(v >= m, 0.0, v)
    return jnp.concatenate(tops, axis=0)


def _pair_products(a, b):
    cand = [a[0:1] * b]
    for ra in range(1, 8):
        cand.append(a[ra:ra + 1] * b[0:8])
    cand.append(a[8:16] * b[0:1])
    return jnp.concatenate(cand, axis=0)


def _peer_topk_kernel(qp_ref, sk_ref, e1_out, e2_out, eps_out):
    for hh in range(PEER_HEADS):
        base = 2 * PEER_HALF * hh
        s1 = _dot_nt(sk_ref[0], qp_ref[:, base:base + PEER_HALF])
        s2 = _dot_nt(sk_ref[1], qp_ref[:, base + PEER_HALF:base + 2 * PEER_HALF])
        e1 = jnp.exp(s1 - jnp.max(s1, axis=0, keepdims=True))
        e2 = jnp.exp(s2 - jnp.max(s2, axis=0, keepdims=True))
        a = _top_rows(e1, PEER_TOPK)
        b = _top_rows(e2, PEER_TOPK)
        z = jnp.sum(_top_rows(_pair_products(a, b), PEER_TOPK), axis=0, keepdims=True)
        zinv = 1.0 / z
        gates = _top_rows(_pair_products(a * zinv, b), PEER_TOPK)
        e1_out[hh] = e1 * zinv
        for j in range(e2.shape[1] // LANES):
            e2_out[hh, j] = e2[:, LANES * j:LANES * (j + 1)]
        eps_out[hh:hh + 1, :] = gates[PEER_TOPK - 1:PEER_TOPK]


def _peer_topk(qp, sub_keys, tm):
    n_, qd = qp.shape
    return pl.pallas_call(
        _peer_topk_kernel,
        grid=(n_ // tm,),
        in_specs=[
            pl.BlockSpec((tm, qd), lambda i: (i, 0)),
            pl.BlockSpec((2, PEER_NKEYS, PEER_HALF), lambda i: (0, 0, 0)),
        ],
        out_specs=[
            pl.BlockSpec((PEER_HEADS, PEER_NKEYS, tm), lambda i: (0, 0, i)),
            pl.BlockSpec((PEER_HEADS, tm // LANES, PEER_NKEYS, LANES), lambda i: (0, i, 0, 0)),
            pl.BlockSpec((PEER_HEADS, tm), lambda i: (0, i)),
        ],
        out_shape=[
            jax.ShapeDtypeStruct((PEER_HEADS, PEER_NKEYS, n_), F32),
            jax.ShapeDtypeStruct((PEER_HEADS, n_ // LANES, PEER_NKEYS, LANES), F32),
            jax.ShapeDtypeStruct((PEER_HEADS, n_), F32),
        ],
        compiler_params=_params(("parallel",)),
        name="peer_topk",
    )(qp, sub_keys)


def _peer_mix_kernel(ht_ref, u_ref, vt_ref, e1_ref, e2_ref, eps_ref, x1_ref, out_ref,
                     acc_ref, crow_ref, a0_ref, a1_ref, z0_ref, z1_ref):
    ie = pl.program_id(1)
    te = u_ref.shape[0]
    tm = ht_ref.shape[1]
    n_tiles = pl.num_programs(1) - 2
    per_tile = te // PEER_NKEYS

    @pl.when(ie == 0)
    def _():
        acc_ref[...] = jnp.zeros_like(acc_ref)
        a1_ref[...] = jnp.zeros_like(a1_ref)
        z0_ref[...] = jnp.zeros_like(z0_ref)

    d_ = vt_ref.shape[0]
    n_lane = tm // LANES
    n_blocks = per_tile * n_lane
    a_units = [(r, c, kc) for r in range(te // MXU_ROWS) for c in range(tm // MXU_ROWS)
               for kc in range(d_ // A_KCHUNK)]
    c_units = [(r, c) for r in range(d_ // MXU_ROWS) for c in range(tm // MXU_ROWS)]

    def step(a_w, a_r, z_w, z_r):
        tile = jnp.clip(ie - 1, 0, n_tiles - 1)

        def a_unit(r, c, kc):
            rs = slice(MXU_ROWS * r, MXU_ROWS * (r + 1))
            cs = slice(MXU_ROWS * c, MXU_ROWS * (c + 1))
            ds = slice(A_KCHUNK * kc, A_KCHUNK * (kc + 1))
            res = jnp.dot(u_ref[rs, ds], ht_ref[ds, cs], preferred_element_type=F32)
            for jj in range(MXU_ROWS // LANES):
                j = c * (MXU_ROWS // LANES) + jj
                part = res[:, LANES * jj:LANES * (jj + 1)]
                if kc == 0:
                    a_w[j, rs, :] = part
                else:
                    a_w[j, rs, :] += part

        def c_unit(r, c):
            rs = slice(MXU_ROWS * r, MXU_ROWS * (r + 1))
            cs = slice(MXU_ROWS * c, MXU_ROWS * (c + 1))
            acc_ref[rs, cs] += jnp.dot(vt_ref[rs, :], z_r[:, cs], preferred_element_type=F32)

        def b_block(k, j):
            ls = slice(LANES * j, LANES * (j + 1))
            for part in range(PEER_NKEYS // GATE_ROWS):
                ks = slice(GATE_ROWS * part, GATE_ROWS * (part + 1))
                rs = slice(PEER_NKEYS * k + GATE_ROWS * part, PEER_NKEYS * k + GATE_ROWS * (part + 1))
                g = jnp.zeros((GATE_ROWS, LANES), F32)
                for hh in range(PEER_HEADS):
                    row = PEER_HEADS * k + hh
                    gate = e2_ref[hh, j, ks, :] * crow_ref[row:row + 1, ls]
                    g = g + jnp.where(gate >= eps_ref[hh:hh + 1, ls], gate, 0.0)
                z_w[rs, ls] = (_gelu(a_r[j, rs, :]) * g).astype(z_w.dtype)

        for k in range(per_tile):
            i1 = tile * per_tile + k
            for hh in range(PEER_HEADS):
                row = PEER_HEADS * k + hh
                crow_ref[row:row + 1, :] = e1_ref[hh, pl.ds(i1, 1), :]

        a_iter = iter(a_units)
        c_iter = iter(c_units)
        a_every = max(n_blocks // len(a_units), 1)
        c_per = -(-len(c_units) // n_blocks)
        for blk in range(n_blocks):
            k, j = divmod(blk, n_lane)
            if blk % a_every == 0:
                unit = next(a_iter, None)
                if unit is not None:
                    a_unit(*unit)
            for _ in range(c_per):
                unit = next(c_iter, None)
                if unit is not None:
                    c_unit(*unit)
            b_block(k, j)
        for unit in a_iter:
            a_unit(*unit)
        for unit in c_iter:
            c_unit(*unit)

    @pl.when(ie % 2 == 0)
    def _():
        step(a0_ref, a1_ref, z1_ref, z0_ref)

    @pl.when(ie % 2 == 1)
    def _():
        step(a1_ref, a0_ref, z0_ref, z1_ref)

    @pl.when(ie == pl.num_programs(1) - 1)
    def _():
        out_ref[...] = x1_ref[...] + acc_ref[...].T


def _peer_mix(h2t, u, vt, e1t, e2t, eps, x1, tm, te):
    d_, n_ = h2t.shape
    n_tiles = u.shape[0] // te
    return pl.pallas_call(
        _peer_mix_kernel,
        grid=(n_ // tm, n_tiles + 2),
        in_specs=[
            pl.BlockSpec((d_, tm), lambda it, ie: (0, it)),
            pl.BlockSpec((te, d_), lambda it, ie: (jnp.minimum(ie, n_tiles - 1), 0)),
            pl.BlockSpec((d_, te), lambda it, ie: (0, jnp.maximum(ie - 2, 0))),
            pl.BlockSpec((PEER_HEADS, PEER_NKEYS, tm), lambda it, ie: (0, 0, it)),
            pl.BlockSpec((PEER_HEADS, tm // LANES, PEER_NKEYS, LANES), lambda it, ie: (0, it, 0, 0)),
            pl.BlockSpec((PEER_HEADS, tm), lambda it, ie: (0, it)),
            pl.BlockSpec((tm, d_), lambda it, ie: (it, 0)),
        ],
        out_specs=pl.BlockSpec((tm, d_), lambda it, ie: (it, 0)),
        out_shape=jax.ShapeDtypeStruct((n_, d_), F32),
        scratch_shapes=[
            pltpu.VMEM((d_, tm), F32),
            pltpu.VMEM((PEER_HEADS * (te // PEER_NKEYS), tm), F32),
            pltpu.VMEM((tm // LANES, te, LANES), F32), pltpu.VMEM((tm // LANES, te, LANES), F32),
            pltpu.VMEM((te, tm), MXU_DTYPE), pltpu.VMEM((te, tm), MXU_DTYPE),
        ],
        compiler_params=_params(("parallel", "arbitrary")),
        name="peer_mix",
    )(h2t, u, vt, e1t, e2t, eps, x1)


def _rel_bucket(dist):
    n = jnp.maximum(dist, 0)
    max_exact = REL_BUCKETS // 2
    nf = jnp.maximum(n, 1).astype(F32)
    large = max_exact + (jnp.log(nf / max_exact) / math.log(REL_MAX_DIST / max_exact)
                         * (REL_BUCKETS - max_exact)).astype(jnp.int32)
    large = jnp.minimum(large, REL_BUCKETS - 1)
    return jnp.where(n < max_exact, n, large)


def _toeplitz(btab, off, rows, cols):
    per = rows + cols
    k = np.concatenate([np.arange(cols), np.zeros(1, np.int64), np.arange(-(rows - 1), 0)])
    idx = np.clip(off - k, 0, btab.shape[1] - 1)
    w = jnp.take(btab, jnp.asarray(idx, jnp.int32), axis=1)
    x = jnp.tile(w, (1, rows))[:, :rows * (per - 1)]
    return x.reshape(btab.shape[0], rows, per - 1)[:, :, :cols]


def _bias_tiles(rel_bias):
    dist = jnp.arange(BIAS_TABLE, dtype=jnp.int32)
    rel_bias = rel_bias.astype(F32) * LOG2E
    btab = rel_bias[_rel_bucket(dist)].T
    tbw = jnp.stack([_toeplitz(btab, SEL_TILE * (dd + 1) - Q_BLOCK, Q_BLOCK, SEL_TILE)
                     for dd in range(NEAR_TILES)], axis=1)
    wb = _toeplitz(btab, WINDOW, Q_BLOCK, WIN_KEYS)
    off = CMP_STRIDE * (LANES - 8) - (CMP_BLOCK - 1)
    front = CMP_STRIDE * LANES - off
    length = CMP_STRIDE * (LANES + 8 + 2)
    padded = jnp.pad(btab, ((0, 0), (front, length - front - BIAS_TABLE)))
    ch = padded.reshape(btab.shape[0], length // CMP_STRIDE, CMP_STRIDE)
    nwin = LANES + 1
    win = jnp.stack([ch[:, a:a + nwin] for a in range(Q_BLOCK // CMP_STRIDE)], axis=2)
    win = win.reshape(btab.shape[0], nwin, Q_BLOCK)
    cb = jnp.swapaxes(win[:, 1:nwin][:, ::-1], 1, 2)
    cfar = rel_bias.astype(F32)[REL_BUCKETS - 1]
    grp = lambda a: a.reshape((NSA_KV, NSA_REP) + a.shape[1:])

    r = np.arange(Q_BLOCK)[:, None]
    near_ok = np.stack([r - np.arange(SEL_TILE)[None, :] + SEL_TILE * (dd + 1) - Q_BLOCK >= 0
                        for dd in range(NEAR_TILES)])
    tbm = jnp.where(near_ok, tbw - cfar[:, None, None, None], NEG)
    dist_w = r - np.arange(WIN_KEYS)[None, :] + WINDOW
    wbm = jnp.where((dist_w >= 0) & (dist_w < WINDOW), wb, NEG)
    dist_c = r - CMP_STRIDE * np.arange(CMP_PAD)[None, :] + (CMP_STRIDE * CMP_PAD - Q_BLOCK - (CMP_BLOCK - 1))
    cbias = jnp.concatenate(
        [jnp.broadcast_to(cfar[:, None, None], (NSA_HEADS, Q_BLOCK, CMP_PAD - LANES)), cb], axis=2)
    cbm = jnp.where(dist_c >= 0, cbias, NEG)

    hi = cfar.astype(BF16).astype(F32)
    lo = (cfar - hi).astype(BF16).astype(F32)
    qmid = jnp.zeros((NSA_HEADS, 1, LANES), F32)
    qmid = qmid.at[:, 0, FAR_HI_LANE - HEAD_DIM].set(hi).at[:, 0, FAR_LO_LANE - HEAD_DIM].set(lo)
    qmid = qmid.at[:, 0, PAD_FLAG_LANE - HEAD_DIM].set(NEG)
    return grp(tbm), grp(wbm), grp(cbm), grp(qmid)


def _extend_keys(ks, kw):
    tp = ks.shape[2] + KEY_PAD
    pos = np.arange(tp) - KEY_PAD
    real = pos >= 0
    mid_s = np.zeros((tp, LANES - HEAD_DIM), np.float32)
    mid_s[:, FAR_HI_LANE - HEAD_DIM] = real
    mid_s[:, FAR_LO_LANE - HEAD_DIM] = real
    mid_s[:, PAD_FLAG_LANE - HEAD_DIM] = ~real
    onehot = ((pos[:, None] // SLC_BLOCK) == np.arange(LANES)[None, :]) & real[:, None]
    mid_w = np.zeros((tp, LANES - HEAD_DIM), np.float32)
    mid_w[:, PAD_FLAG_LANE - HEAD_DIM] = ~real
    padk = lambda a: jnp.pad(a, ((0, 0), (0, 0), (KEY_PAD, 0), (0, 0)))
    bc = lambda a: jnp.broadcast_to(jnp.asarray(a, ks.dtype), ks.shape[:2] + a.shape)
    ksa = jnp.concatenate([padk(ks), bc(mid_s), bc(onehot.astype(np.float32))], axis=-1)
    kwa = jnp.concatenate([padk(kw), bc(mid_w)], axis=-1)
    return ksa, kwa


def _overlap_matrix():
    c = np.arange(CMP_PAD)[:, None]
    s = np.arange(LANES)[None, :]
    per = SLC_BLOCK // CMP_STRIDE
    ov = (s == c // per) | ((c % per == per - 1) & (s == c // per + 1))
    return jnp.asarray(ov.astype(np.float32), BF16)


def kernel(x, attn_norm_g, w_in, q_norm_g, k_norm_g, cmp_pe_k, cmp_w1_k, cmp_w2_k, cmp_pe_v,
           cmp_w1_v, cmp_w2_v, rel_bias, sg_ln_g, sg_ln_b, sg_w, sg_b, out_norm_nsa, out_norm_sg,
           w_out, ffn_norm_g, peer_w_query, peer_sub_keys, peer_u, peer_v):
    b_, t_, d_ = x.shape
    n_ = b_ * t_
    assert t_ % Q_BLOCK == 0 and t_ >= WIN_KEYS and d_ == D_NSA + D_SG

    c_kv = D_NSA + 6 * D_KV
    c_gate = c_kv + 3 * NSA_HEADS
    wg = w_in[:, c_kv:c_gate].reshape(d_, NSA_KV, 3 * NSA_REP)
    wg = jnp.pad(wg, ((0, 0), (0, 0), (0, LANES - 3 * NSA_REP))).reshape(d_, NSA_KV * LANES)
    w_attn = jnp.concatenate([w_in[:, :c_kv], wg], axis=1).astype(MXU_DTYPE)
    w_sgu = w_in[:, c_gate:].astype(MXU_DTYPE)
    ng = attn_norm_g.reshape(1, d_)

    qn, kcv, ks, vs, kw, vw, gate = _attn_proj(x, ng, w_attn, q_norm_g.reshape(1, HEAD_DIM),
                                               k_norm_g, tm=256)
    gu, vn = _sgu_proj(x, ng, w_sgu, sg_ln_g, sg_ln_b, tm=256)

    nch = t_ // CMP_STRIDE
    xc = kcv.reshape(b_, nch, CMP_STRIDE, 2, NSA_KV, HEAD_DIM).transpose(0, 3, 4, 1, 2, 5)
    xc = xc.reshape(b_, 2, NSA_KV, nch, CMP_STRIDE * HEAD_DIM)
    w1 = jnp.stack([cmp_w1_k, cmp_w1_v]).astype(MXU_DTYPE)
    w2 = jnp.stack([cmp_w2_k, cmp_w2_v]).astype(MXU_DTYPE)
    pe = jnp.stack([cmp_pe_k, cmp_pe_v]).transpose(0, 2, 1, 3).reshape(2, NSA_KV, 1, CMP_BLOCK * HEAD_DIM)
    kcv_c = _compress(xc, w1, w2, pe.astype(MXU_DTYPE), k_norm_g)

    padk = lambda a: jnp.pad(a, ((0, 0), (0, 0), (KEY_PAD, 0), (0, 0)))
    tbm, wbm, cbm, qmid = _bias_tiles(rel_bias)
    ksa, kwa = _extend_keys(ks, kw)
    onsa = _nsa_attention(qn, gate, kcv_c, ksa, padk(vs), kwa, padk(vw),
                          tbm, wbm, cbm, qmid, _overlap_matrix())

    x1, h2t, qp = _mix(x.reshape(n_, d_), onsa.reshape(n_, D_NSA), gu.reshape(n_, D_SG),
                      vn.reshape(n_, D_SG), sg_w, sg_b.T, out_norm_nsa.reshape(1, D_NSA),
                      out_norm_sg.reshape(1, D_SG), w_out.astype(MXU_DTYPE),
                      ffn_norm_g.reshape(1, d_), peer_w_query.astype(MXU_DTYPE), tm=256)

    e1t, e2t, eps = _peer_topk(qp, peer_sub_keys, tm=256)
    out = _peer_mix(h2t, peer_u.astype(MXU_DTYPE), peer_v.T.astype(MXU_DTYPE), e1t, e2t, eps, x1,
                    tm=min(512, n_), te=512)
    return out.reshape(b_, t_, d_)
```

```python
import functools
import math

import jax
import jax.numpy as jnp
import numpy as np
from jax import lax
from jax.experimental import pallas as pl
from jax.experimental.pallas import tpu as pltpu

F32 = jnp.float32
BF16 = jnp.bfloat16
MXU_DTYPE = BF16

EPS = 1e-6
NEG = -1e30
SQRT_HALF = 0.7071067811865476
LOG2E = 1.4426950408889634

NSA_HEADS = 16
NSA_KV = 4
NSA_REP = 4
HEAD_DIM = 64
D_NSA = NSA_HEADS * HEAD_DIM
D_KV = NSA_KV * HEAD_DIM
CMP_STRIDE = 16
CMP_BLOCK = 32
CMP_HIDDEN = 128
SLC_BLOCK = 64
SLC_TOPN = 16
WINDOW = 512
Q_BLOCK = 128
SG_HEADS = 8
SG_DIM = 128
SG_CHUNK = 128
D_SG = SG_HEADS * SG_DIM
REL_BUCKETS = 32
REL_MAX_DIST = 1024
PEER_HEADS = 8
PEER_NKEYS = 128
PEER_HALF = 128
PEER_TOPK = 16

LANES = 128
MXU_ROWS = 256
A_KCHUNK = 1024
GATE_ROWS = 64
KEY_PAD = 1024
CMP_PAD = 512
SEL_TILE = 512
WIN_KEYS = WINDOW + Q_BLOCK
NEAR_TILES = 2
FAR_GROUP = 4
BIAS_TABLE = 2048
FAR_DIST = 897
FAR_HI_LANE = 64
FAR_LO_LANE = 65
PAD_FLAG_LANE = 66
N_FORCED = 3
VMEM_LIMIT = 56 * 1024 * 1024


def _gelu(x):
    return 0.5 * x * (1.0 + lax.erf(x * SQRT_HALF))


def _dot(a, b):
    return jnp.dot(a.astype(MXU_DTYPE), b.astype(MXU_DTYPE), preferred_element_type=F32)


def _dot_nt(a, b):
    return lax.dot_general(a.astype(MXU_DTYPE), b.astype(MXU_DTYPE),
                           (((1,), (1,)), ((), ())), preferred_element_type=F32)


def _rms(x, g):
    ms = jnp.mean(x * x, axis=-1, keepdims=True)
    return x * lax.rsqrt(ms + EPS) * g


def _params(sem):
    return pltpu.CompilerParams(dimension_semantics=sem, vmem_limit_bytes=VMEM_LIMIT)


def _attn_proj_kernel(x_ref, ng_ref, w_ref, qg_ref, kg_ref,
                      q_out, kcv_out, ks_out, vs_out, kw_out, vw_out, gate_out):
    h = _rms(x_ref[0], ng_ref[...]).astype(MXU_DTYPE)
    tm = h.shape[0]

    pq = _dot(h, w_ref[:, 0:D_NSA])
    qg = qg_ref[...]
    for hh in range(NSA_HEADS):
        sl = slice(HEAD_DIM * hh, HEAD_DIM * (hh + 1))
        q_out[0, :, sl] = (_rms(pq[:, sl], qg) * (HEAD_DIM ** -0.5 * LOG2E)).astype(q_out.dtype)

    kcv_out[0] = _dot(h, w_ref[:, D_NSA:D_NSA + 2 * D_KV]).astype(kcv_out.dtype)

    base = D_NSA + 2 * D_KV
    p4 = _dot(h, w_ref[:, base:base + 4 * D_KV])
    ones_col = (lax.broadcasted_iota(jnp.int32, (tm, HEAD_DIM), 1) == 0).astype(F32)
    for g in range(NSA_KV):
        sl = slice(HEAD_DIM * g, HEAD_DIM * (g + 1))
        ks_out[0, g] = _rms(p4[:, 0 * D_KV:1 * D_KV][:, sl], kg_ref[1:2, :]).astype(ks_out.dtype)
        kw_out[0, g] = _rms(p4[:, 2 * D_KV:3 * D_KV][:, sl], kg_ref[2:3, :]).astype(kw_out.dtype)
        vs_out[0, g, :, 0:HEAD_DIM] = p4[:, 1 * D_KV:2 * D_KV][:, sl].astype(vs_out.dtype)
        vs_out[0, g, :, HEAD_DIM:2 * HEAD_DIM] = ones_col.astype(vs_out.dtype)
        vw_out[0, g, :, 0:HEAD_DIM] = p4[:, 3 * D_KV:4 * D_KV][:, sl].astype(vw_out.dtype)
        vw_out[0, g, :, HEAD_DIM:2 * HEAD_DIM] = ones_col.astype(vw_out.dtype)

    base = base + 4 * D_KV
    gate_out[0] = jax.nn.sigmoid(_dot(h, w_ref[:, base:base + NSA_KV * LANES]))


def _attn_proj(x, ng, w, qg, kg, tm):
    b_, t_, d_ = x.shape
    nw = w.shape[1]
    kv4 = lambda wd: pl.BlockSpec((1, NSA_KV, tm, wd), lambda b, i: (b, 0, i, 0))
    return pl.pallas_call(
        _attn_proj_kernel,
        grid=(b_, t_ // tm),
        in_specs=[
            pl.BlockSpec((1, tm, d_), lambda b, i: (b, i, 0)),
            pl.BlockSpec((1, d_), lambda b, i: (0, 0)),
            pl.BlockSpec((d_, nw), lambda b, i: (0, 0), pipeline_mode=pl.Buffered(1)),
            pl.BlockSpec((1, HEAD_DIM), lambda b, i: (0, 0)),
            pl.BlockSpec((3, HEAD_DIM), lambda b, i: (0, 0)),
        ],
        out_specs=[
            pl.BlockSpec((1, tm, D_NSA), lambda b, i: (b, i, 0)),
            pl.BlockSpec((1, tm, 2 * D_KV), lambda b, i: (b, i, 0)),
            kv4(HEAD_DIM), kv4(2 * HEAD_DIM), kv4(HEAD_DIM), kv4(2 * HEAD_DIM),
            pl.BlockSpec((1, tm, NSA_KV * LANES), lambda b, i: (b, i, 0)),
        ],
        out_shape=[
            jax.ShapeDtypeStruct((b_, t_, D_NSA), MXU_DTYPE),
            jax.ShapeDtypeStruct((b_, t_, 2 * D_KV), MXU_DTYPE),
            jax.ShapeDtypeStruct((b_, NSA_KV, t_, HEAD_DIM), MXU_DTYPE),
            jax.ShapeDtypeStruct((b_, NSA_KV, t_, 2 * HEAD_DIM), MXU_DTYPE),
            jax.ShapeDtypeStruct((b_, NSA_KV, t_, HEAD_DIM), MXU_DTYPE),
            jax.ShapeDtypeStruct((b_, NSA_KV, t_, 2 * HEAD_DIM), MXU_DTYPE),
            jax.ShapeDtypeStruct((b_, t_, NSA_KV * LANES), F32),
        ],
        compiler_params=_params(("parallel", "parallel")),
        name="attn_proj",
    )(x, ng, w, qg, kg)


def _sgu_proj_kernel(x_ref, ng_ref, w_ref, lng_ref, lnb_ref, gu_out, vn_out):
    h = _rms(x_ref[0], ng_ref[...]).astype(MXU_DTYPE)
    gu_out[0] = _gelu(_dot(h, w_ref[:, 0:D_SG])).astype(gu_out.dtype)
    gv = _gelu(_dot(h, w_ref[:, D_SG:2 * D_SG]))
    for hh in range(SG_HEADS):
        sl = slice(SG_DIM * hh, SG_DIM * (hh + 1))
        v = gv[:, sl]
        mu = jnp.mean(v, axis=-1, keepdims=True)
        var = jnp.mean(jnp.square(v - mu), axis=-1, keepdims=True)
        y = (v - mu) * lax.rsqrt(var + EPS) * lng_ref[hh:hh + 1, :] + lnb_ref[hh:hh + 1, :]
        vn_out[0, :, sl] = y.astype(vn_out.dtype)


def _sgu_proj(x, ng, w, lng, lnb, tm):
    b_, t_, d_ = x.shape
    return pl.pallas_call(
        _sgu_proj_kernel,
        grid=(b_, t_ // tm),
        in_specs=[
            pl.BlockSpec((1, tm, d_), lambda b, i: (b, i, 0)),
            pl.BlockSpec((1, d_), lambda b, i: (0, 0)),
            pl.BlockSpec((d_, 2 * D_SG), lambda b, i: (0, 0), pipeline_mode=pl.Buffered(1)),
            pl.BlockSpec((SG_HEADS, SG_DIM), lambda b, i: (0, 0)),
            pl.BlockSpec((SG_HEADS, SG_DIM), lambda b, i: (0, 0)),
        ],
        out_specs=[
            pl.BlockSpec((1, tm, D_SG), lambda b, i: (b, i, 0)),
            pl.BlockSpec((1, tm, D_SG), lambda b, i: (b, i, 0)),
        ],
        out_shape=[
            jax.ShapeDtypeStruct((b_, t_, D_SG), MXU_DTYPE),
            jax.ShapeDtypeStruct((b_, t_, D_SG), MXU_DTYPE),
        ],
        compiler_params=_params(("parallel", "parallel")),
        name="sgu_proj",
    )(x, ng, w, lng, lnb)


def _compress_kernel(x_ref, w1_ref, w2_ref, pe_ref, kg_ref, out_ref):
    which = pl.program_id(1)
    x = x_ref[0, 0, 0]
    nch = x.shape[0]
    half = CMP_STRIDE * HEAD_DIM
    a = _dot(x, w1_ref[0, 0, 0:half, :])
    bm = _dot(x, w1_ref[0, 0, half:2 * half, :])
    pec = _dot(pe_ref[0, 0], w1_ref[0, 0])
    hid = _gelu(a + pltpu.roll(bm, shift=nch - 1, axis=0) + pec)
    y = _dot(hid, w2_ref[0, 0])
    y = jnp.where(which == 0, _rms(y, kg_ref[0:1, :]), y)
    lane = lax.broadcasted_iota(jnp.int32, (CMP_PAD, LANES), 1)
    flag = ((lane == PAD_FLAG_LANE) & (which == 0)).astype(out_ref.dtype)
    out_ref[0, 0, 0, 0:CMP_PAD, :] = flag
    out_ref[0, 0, 0, CMP_PAD:CMP_PAD + nch, 0:HEAD_DIM] = y.astype(out_ref.dtype)
    out_ref[0, 0, 0, CMP_PAD:CMP_PAD + nch, HEAD_DIM:LANES] = jnp.zeros((nch, LANES - HEAD_DIM), out_ref.dtype)


def _compress(xc, w1, w2, pe, kg):
    b_, _, g_, nch, fl = xc.shape
    return pl.pallas_call(
        _compress_kernel,
        grid=(b_, 2, g_),
        in_specs=[
            pl.BlockSpec((1, 1, 1, nch, fl), lambda b, w, g: (b, w, g, 0, 0)),
            pl.BlockSpec((1, 1, 2 * fl, CMP_HIDDEN), lambda b, w, g: (w, g, 0, 0)),
            pl.BlockSpec((1, 1, CMP_HIDDEN, HEAD_DIM), lambda b, w, g: (w, g, 0, 0)),
            pl.BlockSpec((1, 1, 1, 2 * fl), lambda b, w, g: (w, g, 0, 0)),
            pl.BlockSpec((3, HEAD_DIM), lambda b, w, g: (0, 0)),
        ],
        out_specs=pl.BlockSpec((1, 1, 1, CMP_PAD + nch, LANES), lambda b, w, g: (b, w, g, 0, 0)),
        out_shape=jax.ShapeDtypeStruct((b_, 2, g_, CMP_PAD + nch, LANES), F32),
        compiler_params=_params(("parallel", "parallel", "parallel")),
        name="nsa_compress",
    )(xc, w1, w2, pe, kg)


def _nsa_kernel(q_ref, gate_ref, kc_ref, vc_ref, ks_ref, vs_ref, kw_ref, vw_ref,
                tbm_ref, wbm_ref, cbm_ref, qmid_ref, ov_ref, o_ref, qa_ref,
                accs_ref, accw_ref, oc_ref):
    i = jnp.minimum(pl.program_id(2), pl.num_programs(2) - 2)
    qb = Q_BLOCK
    rows = NSA_REP * qb
    ncmp = CMP_PAD

    @pl.when((pl.program_id(0) == 0) & (pl.program_id(1) == 0) & (pl.program_id(2) == 0))
    def _():
        accs_ref[...] = jnp.ones_like(accs_ref)
        accw_ref[...] = jnp.ones_like(accw_ref)
        oc_ref[...] = jnp.zeros_like(oc_ref)

    gt = gate_ref[0]
    for r in range(NSA_REP):
        rs = slice(qb * r, qb * (r + 1))
        o_s = accs_ref[rs, 0:HEAD_DIM] / accs_ref[rs, HEAD_DIM:HEAD_DIM + 1]
        o_w = accw_ref[rs, 0:HEAD_DIM] / accw_ref[rs, HEAD_DIM:HEAD_DIM + 1]
        o = (gt[:, 3 * r + 0:3 * r + 1] * oc_ref[rs, :] + gt[:, 3 * r + 1:3 * r + 2] * o_s
             + gt[:, 3 * r + 2:3 * r + 3] * o_w)
        o_ref[0, :, HEAD_DIM * r:HEAD_DIM * (r + 1)] = o.astype(o_ref.dtype)

    qblk = q_ref[0]
    for r in range(NSA_REP):
        rs = slice(qb * r, qb * (r + 1))
        qa_ref[rs, 0:HEAD_DIM] = qblk[:, HEAD_DIM * r:HEAD_DIM * (r + 1)]
        qa_ref[rs, HEAD_DIM:LANES] = jnp.broadcast_to(
            qmid_ref[0, r][:, 0:LANES - HEAD_DIM], (qb, LANES - HEAD_DIM)).astype(qa_ref.dtype)
    q_lo = qa_ref[:, 0:LANES]

    wstart = pl.multiple_of(KEY_PAD + qb * (i + 1) - WIN_KEYS, qb)
    s_w = _dot_nt(q_lo, kw_ref[0, 0, pl.ds(wstart, WIN_KEYS), :])

    def window_probs(r):
        s = s_w[qb * r:qb * (r + 1)] + wbm_ref[0, r]
        return jnp.exp2(s - jnp.max(s, axis=-1, keepdims=True))

    cstart = pl.multiple_of(8 * i + 8, 8)
    kc = kc_ref[0, 0, 0, pl.ds(cstart, ncmp), :]
    vc = vc_ref[0, 0, 0, pl.ds(cstart, ncmp), :]
    s_c = _dot_nt(q_lo, kc)

    kt = SEL_TILE
    n_tiles = (i * qb + qb + kt - 1) // kt

    def tile_start(dd):
        return pl.multiple_of(KEY_PAD + qb * (i + 1) - kt * (dd + 1), qb)

    near_base = [_dot_nt(q_lo, ks_ref[0, 0, pl.ds(tile_start(dd), kt), 0:LANES])
                 for dd in range(NEAR_TILES)]
    psum = jnp.zeros((qb, ncmp), F32)
    for r in range(NSA_REP):
        s = s_c[qb * r:qb * (r + 1)] + cbm_ref[0, r]
        m = jnp.max(s, axis=-1, keepdims=True)
        p = jnp.exp2(s - m)
        l = jnp.sum(p, axis=-1, keepdims=True)
        p = p * jnp.where(m > 0.5 * NEG, 1.0 / l, 0.0)
        oc_ref[qb * r:qb * (r + 1), :] = _dot(p, vc)[:, 0:HEAD_DIM]
        psum = psum + p
    p_hi = psum.astype(BF16)
    p_lo = (psum - p_hi.astype(F32)).astype(BF16)
    ov = ov_ref[...]
    imp = (jnp.dot(p_hi, ov, preferred_element_type=F32)
           + jnp.dot(p_lo, ov, preferred_element_type=F32))

    nsel = LANES
    sp = lax.broadcasted_iota(jnp.int32, (qb, nsel), 1)
    rr = lax.broadcasted_iota(jnp.int32, (qb, nsel), 0)
    cur = (nsel - 2) + (rr >= SLC_BLOCK).astype(jnp.int32)
    s_abs = sp + (2 * i + 2 - nsel)
    valid = s_abs >= 0
    forced = ((s_abs == 0) | (sp == cur) | (sp == cur - 1)) & valid
    excluded = forced | (sp > cur) | (~valid)
    cand_t = jnp.where(excluded, NEG, imp).T
    v = cand_t
    tau = None
    n_rank = SLC_TOPN - N_FORCED
    p_w = []
    for it in range(n_rank):
        tau = jnp.max(v, axis=0, keepdims=True)
        v = jnp.where(v >= tau, 3.0 * NEG, v)
        if it % (n_rank // NSA_REP) == 0 and len(p_w) < NSA_REP:
            p_w.append(window_probs(len(p_w)))
    accw_ref[...] = _dot(jnp.concatenate(p_w, axis=0), vw_ref[0, 0, pl.ds(wstart, WIN_KEYS), :])
    picked = jnp.where((cand_t >= tau) & (cand_t > 0.5 * NEG), 1.0, 0.0).T
    sel_neg = jnp.where((picked > 0.5) | forced, 0.0, NEG)
    sel_neg = pltpu.roll(sel_neg, shift=(2 * i + 2) % nsel, axis=1).astype(qa_ref.dtype)
    for r in range(NSA_REP):
        qa_ref[qb * r:qb * (r + 1), LANES:2 * LANES] = sel_neg
    q_ext = qa_ref[...]

    def tile_scores(dd):
        return _dot_nt(q_ext, ks_ref[0, 0, pl.ds(tile_start(dd), kt), :])

    def tile_softmax(s, dd):
        m_t = jnp.max(s, axis=-1, keepdims=True)
        return m_t, _dot(jnp.exp2(s - m_t), vs_ref[0, 0, pl.ds(tile_start(dd), kt), :])

    def merge(parts):
        m_new = functools.reduce(jnp.maximum, [m for m, _ in parts])
        acc = sum(jnp.exp2(m - m_new) * pv for m, pv in parts)
        return m_new, acc

    near = []
    for dd in range(NEAR_TILES):
        msk = _dot_nt(sel_neg, ks_ref[0, 0, pl.ds(tile_start(dd), kt), LANES:2 * LANES])
        near.append(jnp.concatenate([near_base[dd][qb * r:qb * (r + 1)] + (tbm_ref[0, r, dd] + msk)
                                     for r in range(NSA_REP)], axis=0))
    carry = merge([tile_softmax(s, dd) for dd, s in enumerate(near)])

    def far_group(first, width, carry):
        dds = [jnp.minimum(first + k, n_tiles) for k in range(width)]
        scores = [tile_scores(dd) for dd in dds]
        return merge([carry] + [tile_softmax(s, dd) for s, dd in zip(scores, dds)])

    n_far = jnp.maximum(n_tiles - NEAR_TILES, 0)
    n_groups = n_far // FAR_GROUP
    carry = lax.fori_loop(
        0, n_groups, lambda u, c: far_group(NEAR_TILES + FAR_GROUP * u, FAR_GROUP, c), carry)
    rest = NEAR_TILES + FAR_GROUP * n_groups
    carry = lax.fori_loop(
        0, (n_far - FAR_GROUP * n_groups + 1) // 2, lambda u, c: far_group(rest + 2 * u, 2, c), carry)
    accs_ref[...] = carry[1]


def _nsa_attention(qn, gate, kcv, ksa, vsp, kwa, vwp, tbm, wbm, cbm, qmid, ov):
    b_, t_, _ = qn.shape
    tp = ksa.shape[2]
    ncp = kcv.shape[3]
    gw = NSA_REP * HEAD_DIM
    assert t_ // SLC_BLOCK <= LANES
    kvspec = lambda wd: pl.BlockSpec((1, 1, tp, wd), lambda b, g, i: (b, g, 0, 0))
    nq = t_ // Q_BLOCK
    behind = lambda b, g, i: (b, jnp.maximum(i - 1, 0), g)
    return pl.pallas_call(
        _nsa_kernel,
        grid=(b_, NSA_KV, nq + 1),
        in_specs=[
            pl.BlockSpec((1, Q_BLOCK, gw), lambda b, g, i: (b, jnp.minimum(i, nq - 1), g)),
            pl.BlockSpec((1, Q_BLOCK, LANES), behind),
            pl.BlockSpec((1, 1, 1, ncp, LANES), lambda b, g, i: (b, 0, g, 0, 0)),
            pl.BlockSpec((1, 1, 1, ncp, LANES), lambda b, g, i: (b, 1, g, 0, 0)),
            kvspec(2 * LANES), kvspec(LANES), kvspec(LANES), kvspec(LANES),
            pl.BlockSpec((1, NSA_REP, NEAR_TILES, Q_BLOCK, SEL_TILE), lambda b, g, i: (g, 0, 0, 0, 0)),
            pl.BlockSpec((1, NSA_REP, Q_BLOCK, WIN_KEYS), lambda b, g, i: (g, 0, 0, 0)),
            pl.BlockSpec((1, NSA_REP, Q_BLOCK, CMP_PAD), lambda b, g, i: (g, 0, 0, 0)),
            pl.BlockSpec((1, NSA_REP, 1, LANES), lambda b, g, i: (g, 0, 0, 0)),
            pl.BlockSpec((CMP_PAD, LANES), lambda b, g, i: (0, 0)),
        ],
        out_specs=pl.BlockSpec((1, Q_BLOCK, gw), behind),
        out_shape=jax.ShapeDtypeStruct((b_, t_, D_NSA), F32),
        scratch_shapes=[
            pltpu.VMEM((NSA_REP * Q_BLOCK, 2 * LANES), MXU_DTYPE),
            pltpu.VMEM((NSA_REP * Q_BLOCK, LANES), F32),
            pltpu.VMEM((NSA_REP * Q_BLOCK, LANES), F32),
            pltpu.VMEM((NSA_REP * Q_BLOCK, HEAD_DIM), F32),
        ],
        compiler_params=_params(("parallel", "parallel", "arbitrary")),
        name="nsa_attention",
    )(qn, gate, kcv, kcv, ksa, vsp, kwa, vwp, tbm, wbm, cbm, qmid, ov)


def _mix_kernel(x_ref, on_ref, gu_ref, vn_ref, sgw_ref, sgb_ref, gn_ref, gs_ref, wo_ref,
                fg_ref, wq_ref, x1_out, h2t_out, qp_out):
    tm = x_ref.shape[0]
    tri = (lax.broadcasted_iota(jnp.int32, (SG_CHUNK, SG_CHUNK), 0)
           >= lax.broadcasted_iota(jnp.int32, (SG_CHUNK, SG_CHUNK), 1))
    sgb = sgb_ref[...]
    y = x_ref[...] + _dot(_rms(on_ref[...], gn_ref[...]), wo_ref[0:D_NSA, :])
    parts = []
    for c in range(tm // SG_CHUNK):
        cs = slice(SG_CHUNK * c, SG_CHUNK * (c + 1))
        heads = []
        for hh in range(SG_HEADS):
            sl = slice(SG_DIM * hh, SG_DIM * (hh + 1))
            w = jnp.where(tri, sgw_ref[hh], 0.0)
            mixed = _dot(w, vn_ref[cs, sl]) + sgb[:, hh:hh + 1]
            heads.append(gu_ref[cs, sl].astype(F32) * mixed)
        parts.append(jnp.concatenate(heads, axis=1))
    o_sg = jnp.concatenate(parts, axis=0)
    y = y + _dot(_rms(o_sg, gs_ref[...]), wo_ref[D_NSA:D_NSA + D_SG, :])
    x1_out[...] = y
    h2 = _rms(y, fg_ref[...])
    h2t_out[...] = h2.T.astype(h2t_out.dtype)
    qp_out[...] = _dot(h2, wq_ref[...])


def _mix(x2, onsa, gu, vn, sgw, sgb_t, gn, gs, wo, fg, wq, tm):
    n_, d_ = x2.shape
    row = lambda wd: pl.BlockSpec((tm, wd), lambda i: (i, 0))
    const = lambda shape, **kw: pl.BlockSpec(shape, lambda i: (0,) * len(shape), **kw)
    return pl.pallas_call(
        _mix_kernel,
        grid=(n_ // tm,),
        in_specs=[
            row(d_), row(D_NSA), row(D_SG), row(D_SG),
            const((SG_HEADS, SG_CHUNK, SG_CHUNK)), const((SG_CHUNK, SG_HEADS)),
            const((1, D_NSA)), const((1, D_SG)),
            const((D_NSA + D_SG, d_), pipeline_mode=pl.Buffered(1)),
            const((1, d_)),
            const((d_, wq.shape[1]), pipeline_mode=pl.Buffered(1)),
        ],
        out_specs=[row(d_), pl.BlockSpec((d_, tm), lambda i: (0, i)), row(wq.shape[1])],
        out_shape=[
            jax.ShapeDtypeStruct((n_, d_), F32),
            jax.ShapeDtypeStruct((d_, n_), MXU_DTYPE),
            jax.ShapeDtypeStruct((n_, wq.shape[1]), F32),
        ],
        compiler_params=_params(("parallel",)),
        name="mix_out_proj",
    )(x2, onsa, gu, vn, sgw, sgb_t, gn, gs, wo, fg, wq)


def _top_rows(v, k):
    tops = []
    for _ in range(k):
        m = jnp.max(v, axis=0, keepdims=True)
        tops.append(m)
        v = jnp.where(v >= m, 0.0, v)
    return jnp.concatenate(tops, axis=0)


def _pair_products(a, b):
    cand = [a[0:1] * b]
    for ra in range(1, 8):
        cand.append(a[ra:ra + 1] * b[0:8])
    cand.append(a[8:16] * b[0:1])
    return jnp.concatenate(cand, axis=0)


def _peer_topk_kernel(qp_ref, sk_ref, e1_out, e2_out, eps_out):
    for hh in range(PEER_HEADS):
        base = 2 * PEER_HALF * hh
        s1 = _dot_nt(sk_ref[0], qp_ref[:, base:base + PEER_HALF])
        s2 = _dot_nt(sk_ref[1], qp_ref[:, base + PEER_HALF:base + 2 * PEER_HALF])
        e1 = jnp.exp(s1 - jnp.max(s1, axis=0, keepdims=True))
        e2 = jnp.exp(s2 - jnp.max(s2, axis=0, keepdims=True))
        a = _top_rows(e1, PEER_TOPK)
        b = _top_rows(e2, PEER_TOPK)
        z = jnp.sum(_top_rows(_pair_products(a, b), PEER_TOPK), axis=0, keepdims=True)
        zinv = 1.0 / z
        gates = _top_rows(_pair_products(a * zinv, b), PEER_TOPK)
        e1_out[hh] = e1 * zinv
        for j in range(e2.shape[1] // LANES):
            e2_out[hh, j] = e2[:, LANES * j:LANES * (j + 1)]
        eps_out[hh:hh + 1, :] = gates[PEER_TOPK - 1:PEER_TOPK]


def _peer_topk(qp, sub_keys, tm):
    n_, qd = qp.shape
    return pl.pallas_call(
        _peer_topk_kernel,
        grid=(n_ // tm,),
        in_specs=[
            pl.BlockSpec((tm, qd), lambda i: (i, 0)),
            pl.BlockSpec((2, PEER_NKEYS, PEER_HALF), lambda i: (0, 0, 0)),
        ],
        out_specs=[
            pl.BlockSpec((PEER_HEADS, PEER_NKEYS, tm), lambda i: (0, 0, i)),
            pl.BlockSpec((PEER_HEADS, tm // LANES, PEER_NKEYS, LANES), lambda i: (0, i, 0, 0)),
            pl.BlockSpec((PEER_HEADS, tm), lambda i: (0, i)),
        ],
        out_shape=[
            jax.ShapeDtypeStruct((PEER_HEADS, PEER_NKEYS, n_), F32),
            jax.ShapeDtypeStruct((PEER_HEADS, n_ // LANES, PEER_NKEYS, LANES), F32),
            jax.ShapeDtypeStruct((PEER_HEADS, n_), F32),
        ],
        compiler_params=_params(("parallel",)),
        name="peer_topk",
    )(qp, sub_keys)


def _peer_mix_kernel(n_tiles, ht_ref, u_ref, vt_ref, e1_ref, e2_ref, eps_ref, x1_ref, out_ref,
                     acc_ref, crow_ref, a0_ref, a1_ref, z0_ref, z1_ref):
    ie = pl.program_id(1)
    te = u_ref.shape[0]
    tm = ht_ref.shape[1]
    per_tile = te // PEER_NKEYS

    @pl.when(ie == 0)
    def _():
        acc_ref[...] = jnp.zeros_like(acc_ref)

    d_ = vt_ref.shape[0]
    n_lane = tm // LANES
    n_blocks = per_tile * n_lane
    a_units = [(r, c, kc) for r in range(te // MXU_ROWS) for c in range(tm // MXU_ROWS)
               for kc in range(d_ // A_KCHUNK)]
    c_units = [(r, c) for r in range(d_ // MXU_ROWS) for c in range(tm // MXU_ROWS)]

    def step(a_w, a_r, z_w, z_r):
        tile = ie - 1

        def a_unit(r, c, kc):
            rs = slice(MXU_ROWS * r, MXU_ROWS * (r + 1))
            cs = slice(MXU_ROWS * c, MXU_ROWS * (c + 1))
            ds = slice(A_KCHUNK * kc, A_KCHUNK * (kc + 1))
            res = jnp.dot(u_ref[rs, ds], ht_ref[ds, cs], preferred_element_type=F32)
            for jj in range(MXU_ROWS // LANES):
                j = c * (MXU_ROWS // LANES) + jj
                part = res[:, LANES * jj:LANES * (jj + 1)]
                if kc == 0:
                    a_w[j, rs, :] = part
                else:
                    a_w[j, rs, :] += part

        def c_unit(r, c):
            rs = slice(MXU_ROWS * r, MXU_ROWS * (r + 1))
            cs = slice(MXU_ROWS * c, MXU_ROWS * (c + 1))
            acc_ref[rs, cs] += jnp.dot(vt_ref[rs, :], z_r[:, cs], preferred_element_type=F32)

        def b_block(k, j):
            ls = slice(LANES * j, LANES * (j + 1))
            for part in range(PEER_NKEYS // GATE_ROWS):
                ks = slice(GATE_ROWS * part, GATE_ROWS * (part + 1))
                rs = slice(PEER_NKEYS * k + GATE_ROWS * part, PEER_NKEYS * k + GATE_ROWS * (part + 1))
                g = jnp.zeros((GATE_ROWS, LANES), F32)
                for hh in range(PEER_HEADS):
                    row = PEER_HEADS * k + hh
                    gate = e2_ref[hh, j, ks, :] * crow_ref[row:row + 1, ls]
                    g = g + jnp.where(gate >= eps_ref[hh:hh + 1, ls], gate, 0.0)
                z_w[rs, ls] = (_gelu(a_r[j, rs, :]) * g).astype(z_w.dtype)

        do_a, do_b, do_c = a_w is not None, z_w is not None, z_r is not None
        if do_b:
            for k in range(per_tile):
                i1 = tile * per_tile + k
                for hh in range(PEER_HEADS):
                    row = PEER_HEADS * k + hh
                    crow_ref[row:row + 1, :] = e1_ref[hh, pl.ds(i1, 1), :]

        a_iter = iter(a_units if do_a else [])
        c_iter = iter(c_units if do_c else [])
        a_every = max(n_blocks // len(a_units), 1)
        c_per = -(-len(c_units) // n_blocks)
        for blk in range(n_blocks):
            k, j = divmod(blk, n_lane)
            if blk % a_every == 0:
                unit = next(a_iter, None)
                if unit is not None:
                    a_unit(*unit)
            for _ in range(c_per):
                unit = next(c_iter, None)
                if unit is not None:
                    c_unit(*unit)
            if do_b:
                b_block(k, j)
        for unit in a_iter:
            a_unit(*unit)
        for unit in c_iter:
            c_unit(*unit)

    a_bufs, z_bufs = (a0_ref, a1_ref), (z0_ref, z1_ref)

    def run(par, do_a=True, do_b=True, do_c=True):
        step(a_bufs[par] if do_a else None, a_bufs[1 - par], z_bufs[1 - par] if do_b else None,
             z_bufs[par] if do_c else None)

    last = n_tiles + 1
    pl.when(ie == 0)(lambda: run(0, do_b=False, do_c=False))
    pl.when(ie == 1)(lambda: run(1, do_c=False))
    pl.when((ie >= 2) & (ie < n_tiles) & (ie % 2 == 0))(lambda: run(0))
    pl.when((ie >= 2) & (ie < n_tiles) & (ie % 2 == 1))(lambda: run(1))
    pl.when(ie == n_tiles)(lambda: run(n_tiles % 2, do_a=False))
    pl.when(ie == last)(lambda: run(last % 2, do_a=False, do_b=False))

    @pl.when(ie == last)
    def _():
        out_ref[...] = x1_ref[...] + acc_ref[...].T


def _peer_mix(h2t, u, vt, e1t, e2t, eps, x1, tm, te):
    d_, n_ = h2t.shape
    n_tiles = u.shape[0] // te
    assert n_tiles >= 3
    return pl.pallas_call(
        functools.partial(_peer_mix_kernel, n_tiles),
        grid=(n_ // tm, n_tiles + 2),
        in_specs=[
            pl.BlockSpec((d_, tm), lambda it, ie: (0, it)),
            pl.BlockSpec((te, d_), lambda it, ie: (jnp.minimum(ie, n_tiles - 1), 0)),
            pl.BlockSpec((d_, te), lambda it, ie: (0, jnp.maximum(ie - 2, 0))),
            pl.BlockSpec((PEER_HEADS, PEER_NKEYS, tm), lambda it, ie: (0, 0, it)),
            pl.BlockSpec((PEER_HEADS, tm // LANES, PEER_NKEYS, LANES), lambda it, ie: (0, it, 0, 0)),
            pl.BlockSpec((PEER_HEADS, tm), lambda it, ie: (0, it)),
            pl.BlockSpec((tm, d_), lambda it, ie: (it, 0)),
        ],
        out_specs=pl.BlockSpec((tm, d_), lambda it, ie: (it, 0)),
        out_shape=jax.ShapeDtypeStruct((n_, d_), F32),
        scratch_shapes=[
            pltpu.VMEM((d_, tm), F32),
            pltpu.VMEM((PEER_HEADS * (te // PEER_NKEYS), tm), F32),
            pltpu.VMEM((tm // LANES, te, LANES), F32), pltpu.VMEM((tm // LANES, te, LANES), F32),
            pltpu.VMEM((te, tm), MXU_DTYPE), pltpu.VMEM((te, tm), MXU_DTYPE),
        ],
        compiler_params=_params(("parallel", "arbitrary")),
        name="peer_mix",
    )(h2t, u, vt, e1t, e2t, eps, x1)


def _rel_bucket(dist):
    n = jnp.maximum(dist, 0)
    max_exact = REL_BUCKETS // 2
    nf = jnp.maximum(n, 1).astype(F32)
    large = max_exact + (jnp.log(nf / max_exact) / math.log(REL_MAX_DIST / max_exact)
                         * (REL_BUCKETS - max_exact)).astype(jnp.int32)
    large = jnp.minimum(large, REL_BUCKETS - 1)
    return jnp.where(n < max_exact, n, large)


def _toeplitz(btab, off, rows, cols):
    per = rows + cols
    k = np.concatenate([np.arange(cols), np.zeros(1, np.int64), np.arange(-(rows - 1), 0)])
    idx = np.clip(off - k, 0, btab.shape[1] - 1)
    w = jnp.take(btab, jnp.asarray(idx, jnp.int32), axis=1)
    x = jnp.tile(w, (1, rows))[:, :rows * (per - 1)]
    return x.reshape(btab.shape[0], rows, per - 1)[:, :, :cols]


def _bias_tiles(rel_bias):
    dist = jnp.arange(BIAS_TABLE, dtype=jnp.int32)
    rel_bias = rel_bias.astype(F32) * LOG2E
    btab = rel_bias[_rel_bucket(dist)].T
    tbw = jnp.stack([_toeplitz(btab, SEL_TILE * (dd + 1) - Q_BLOCK, Q_BLOCK, SEL_TILE)
                     for dd in range(NEAR_TILES)], axis=1)
    wb = _toeplitz(btab, WINDOW, Q_BLOCK, WIN_KEYS)
    off = CMP_STRIDE * (LANES - 8) - (CMP_BLOCK - 1)
    front = CMP_STRIDE * LANES - off
    length = CMP_STRIDE * (LANES + 8 + 2)
    padded = jnp.pad(btab, ((0, 0), (front, length - front - BIAS_TABLE)))
    ch = padded.reshape(btab.shape[0], length // CMP_STRIDE, CMP_STRIDE)
    nwin = LANES + 1
    win = jnp.stack([ch[:, a:a + nwin] for a in range(Q_BLOCK // CMP_STRIDE)], axis=2)
    win = win.reshape(btab.shape[0], nwin, Q_BLOCK)
    cb = jnp.swapaxes(win[:, 1:nwin][:, ::-1], 1, 2)
    cfar = rel_bias.astype(F32)[REL_BUCKETS - 1]
    grp = lambda a: a.reshape((NSA_KV, NSA_REP) + a.shape[1:])

    r = np.arange(Q_BLOCK)[:, None]
    near_ok = np.stack([r - np.arange(SEL_TILE)[None, :] + SEL_TILE * (dd + 1) - Q_BLOCK >= 0
                        for dd in range(NEAR_TILES)])
    tbm = jnp.where(near_ok, tbw - cfar[:, None, None, None], NEG)
    dist_w = r - np.arange(WIN_KEYS)[None, :] + WINDOW
    wbm = jnp.where((dist_w >= 0) & (dist_w < WINDOW), wb, NEG)
    dist_c = r - CMP_STRIDE * np.arange(CMP_PAD)[None, :] + (CMP_STRIDE * CMP_PAD - Q_BLOCK - (CMP_BLOCK - 1))
    cbias = jnp.concatenate(
        [jnp.broadcast_to(cfar[:, None, None], (NSA_HEADS, Q_BLOCK, CMP_PAD - LANES)), cb], axis=2)
    cbm = jnp.where(dist_c >= 0, cbias, NEG)

    hi = cfar.astype(BF16).astype(F32)
    lo = (cfar - hi).astype(BF16).astype(F32)
    qmid = jnp.zeros((NSA_HEADS, 1, LANES), F32)
    qmid = qmid.at[:, 0, FAR_HI_LANE - HEAD_DIM].set(hi).at[:, 0, FAR_LO_LANE - HEAD_DIM].set(lo)
    qmid = qmid.at[:, 0, PAD_FLAG_LANE - HEAD_DIM].set(NEG)
    return grp(tbm), grp(wbm), grp(cbm), grp(qmid)


def _extend_keys(ks, kw):
    tp = ks.shape[2] + KEY_PAD
    pos = np.arange(tp) - KEY_PAD
    real = pos >= 0
    mid_s = np.zeros((tp, LANES - HEAD_DIM), np.float32)
    mid_s[:, FAR_HI_LANE - HEAD_DIM] = real
    mid_s[:, FAR_LO_LANE - HEAD_DIM] = real
    mid_s[:, PAD_FLAG_LANE - HEAD_DIM] = ~real
    onehot = ((pos[:, None] // SLC_BLOCK) == np.arange(LANES)[None, :]) & real[:, None]
    mid_w = np.zeros((tp, LANES - HEAD_DIM), np.float32)
    mid_w[:, PAD_FLAG_LANE - HEAD_DIM] = ~real
    padk = lambda a: jnp.pad(a, ((0, 0), (0, 0), (KEY_PAD, 0), (0, 0)))
    bc = lambda a: jnp.broadcast_to(jnp.asarray(a, ks.dtype), ks.shape[:2] + a.shape)
    ksa = jnp.concatenate([padk(ks), bc(mid_s), bc(onehot.astype(np.float32))], axis=-1)
    kwa = jnp.concatenate([padk(kw), bc(mid_w)], axis=-1)
    return ksa, kwa


def _overlap_matrix():
    c = np.arange(CMP_PAD)[:, None]
    s = np.arange(LANES)[None, :]
    per = SLC_BLOCK // CMP_STRIDE
    ov = (s == c // per) | ((c % per == per - 1) & (s == c // per + 1))
    return jnp.asarray(ov.astype(np.float32), BF16)


def kernel(x, attn_norm_g, w_in, q_norm_g, k_norm_g, cmp_pe_k, cmp_w1_k, cmp_w2_k, cmp_pe_v,
           cmp_w1_v, cmp_w2_v, rel_bias, sg_ln_g, sg_ln_b, sg_w, sg_b, out_norm_nsa, out_norm_sg,
           w_out, ffn_norm_g, peer_w_query, peer_sub_keys, peer_u, peer_v):
    b_, t_, d_ = x.shape
    n_ = b_ * t_
    assert t_ % Q_BLOCK == 0 and t_ >= WIN_KEYS and d_ == D_NSA + D_SG

    c_kv = D_NSA + 6 * D_KV
    c_gate = c_kv + 3 * NSA_HEADS
    wg = w_in[:, c_kv:c_gate].reshape(d_, NSA_KV, 3 * NSA_REP)
    wg = jnp.pad(wg, ((0, 0), (0, 0), (0, LANES - 3 * NSA_REP))).reshape(d_, NSA_KV * LANES)
    w_attn = jnp.concatenate([w_in[:, :c_kv], wg], axis=1).astype(MXU_DTYPE)
    w_sgu = w_in[:, c_gate:].astype(MXU_DTYPE)
    ng = attn_norm_g.reshape(1, d_)

    qn, kcv, ks, vs, kw, vw, gate = _attn_proj(x, ng, w_attn, q_norm_g.reshape(1, HEAD_DIM),
                                               k_norm_g, tm=256)
    gu, vn = _sgu_proj(x, ng, w_sgu, sg_ln_g, sg_ln_b, tm=256)

    nch = t_ // CMP_STRIDE
    xc = kcv.reshape(b_, nch, CMP_STRIDE, 2, NSA_KV, HEAD_DIM).transpose(0, 3, 4, 1, 2, 5)
    xc = xc.reshape(b_, 2, NSA_KV, nch, CMP_STRIDE * HEAD_DIM)
    w1 = jnp.stack([cmp_w1_k, cmp_w1_v]).astype(MXU_DTYPE)
    w2 = jnp.stack([cmp_w2_k, cmp_w2_v]).astype(MXU_DTYPE)
    pe = jnp.stack([cmp_pe_k, cmp_pe_v]).transpose(0, 2, 1, 3).reshape(2, NSA_KV, 1, CMP_BLOCK * HEAD_DIM)
    kcv_c = _compress(xc, w1, w2, pe.astype(MXU_DTYPE), k_norm_g)

    padk = lambda a: jnp.pad(a, ((0, 0), (0, 0), (KEY_PAD, 0), (0, 0)))
    tbm, wbm, cbm, qmid = _bias_tiles(rel_bias)
    ksa, kwa = _extend_keys(ks, kw)
    onsa = _nsa_attention(qn, gate, kcv_c, ksa, padk(vs), kwa, padk(vw),
                          tbm, wbm, cbm, qmid, _overlap_matrix())

    x1, h2t, qp = _mix(x.reshape(n_, d_), onsa.reshape(n_, D_NSA), gu.reshape(n_, D_SG),
                      vn.reshape(n_, D_SG), sg_w, sg_b.T, out_norm_nsa.reshape(1, D_NSA),
                      out_norm_sg.reshape(1, D_SG), w_out.astype(MXU_DTYPE),
                      ffn_norm_g.reshape(1, d_), peer_w_query.astype(MXU_DTYPE), tm=256)

    e1t, e2t, eps = _peer_topk(qp, peer_sub_keys, tm=256)
    out = _peer_mix(h2t, peer_u.astype(MXU_DTYPE), peer_v.T.astype(MXU_DTYPE), e1t, e2t, eps, x1,
                    tm=min(512, n_), te=512)
    return out.reshape(b_, t_, d_)
```

```python
import functools
import math

import jax
import jax.numpy as jnp
import numpy as np
from jax import lax
from jax.experimental import pallas as pl
from jax.experimental.pallas import tpu as pltpu

F32 = jnp.float32
BF16 = jnp.bfloat16
MXU_DTYPE = BF16

EPS = 1e-6
NEG = -1e30
SQRT_HALF = 0.7071067811865476
LOG2E = 1.4426950408889634

NSA_HEADS = 16
NSA_KV = 4
NSA_REP = 4
HEAD_DIM = 64
D_NSA = NSA_HEADS * HEAD_DIM
D_KV = NSA_KV * HEAD_DIM
CMP_STRIDE = 16
CMP_BLOCK = 32
CMP_HIDDEN = 128
SLC_BLOCK = 64
SLC_TOPN = 16
WINDOW = 512
Q_BLOCK = 128
SG_HEADS = 8
SG_DIM = 128
SG_CHUNK = 128
D_SG = SG_HEADS * SG_DIM
REL_BUCKETS = 32
REL_MAX_DIST = 1024
PEER_HEADS = 8
PEER_NKEYS = 128
PEER_HALF = 128
PEER_TOPK = 16

LANES = 128
MXU_ROWS = 256
A_KCHUNK = 1024
GATE_ROWS = 64
KEY_PAD = 1024
CMP_PAD = 512
SEL_TILE = 512
WIN_KEYS = WINDOW + Q_BLOCK
NEAR_TILES = 2
FAR_GROUP = 4
BIAS_TABLE = 2048
FAR_DIST = 897
FAR_HI_LANE = 64
FAR_LO_LANE = 65
PAD_FLAG_LANE = 66
N_FORCED = 3
VMEM_LIMIT = 56 * 1024 * 1024


def _gelu(x):
    return 0.5 * x * (1.0 + lax.erf(x * SQRT_HALF))


def _dot(a, b):
    return jnp.dot(a.astype(MXU_DTYPE), b.astype(MXU_DTYPE), preferred_element_type=F32)


def _dot_nt(a, b):
    return lax.dot_general(a.astype(MXU_DTYPE), b.astype(MXU_DTYPE),
                           (((1,), (1,)), ((), ())), preferred_element_type=F32)


def _rms(x, g):
    ms = jnp.mean(x * x, axis=-1, keepdims=True)
    return x * lax.rsqrt(ms + EPS) * g


def _params(sem):
    return pltpu.CompilerParams(dimension_semantics=sem, vmem_limit_bytes=VMEM_LIMIT)


def _attn_proj_kernel(x_ref, ng_ref, w_ref, gq_ref, gkv_ref, hs_ref, he_ref,
                      q_out, kcv_out, kv_out, gate_out):
    h = _rms(x_ref[0], ng_ref[...]).astype(MXU_DTYPE)

    def head_rms(p):
        ms = jnp.dot((p * p).astype(MXU_DTYPE), hs_ref[...], preferred_element_type=F32)
        r = lax.rsqrt(ms + EPS)
        r_hi = r.astype(BF16)
        r_lo = (r - r_hi.astype(F32)).astype(BF16)
        he = he_ref[...]
        return p * (jnp.dot(r_hi, he, preferred_element_type=F32)
                    + jnp.dot(r_lo, he, preferred_element_type=F32))

    pq = _dot(h, w_ref[:, 0:D_NSA])
    q_out[0] = (head_rms(pq) * gq_ref[...]).astype(q_out.dtype)

    kcv_out[0] = _dot(h, w_ref[:, D_NSA:D_NSA + 2 * D_KV]).astype(kcv_out.dtype)

    base = D_NSA + 2 * D_KV
    p4 = _dot(h, w_ref[:, base:base + 4 * D_KV])
    col = lax.broadcasted_iota(jnp.int32, (1, 4 * D_KV), 1)
    is_key = (col // D_KV) % 2 == 0
    kv_out[0] = jnp.where(is_key, head_rms(p4) * gkv_ref[...], p4).astype(kv_out.dtype)

    base = base + 4 * D_KV
    gate_out[0] = jax.nn.sigmoid(_dot(h, w_ref[:, base:base + NSA_KV * LANES]))


def _attn_proj(x, ng, w, gq, gkv, tm):
    b_, t_, d_ = x.shape
    nw = w.shape[1]
    heads = np.arange(D_NSA) // HEAD_DIM
    hs = (heads[:, None] == np.arange(LANES)[None, :]).astype(np.float32)
    const = lambda shape: pl.BlockSpec(shape, lambda b, i: (0,) * len(shape))
    return pl.pallas_call(
        _attn_proj_kernel,
        grid=(b_, t_ // tm),
        in_specs=[
            pl.BlockSpec((1, tm, d_), lambda b, i: (b, i, 0)),
            const((1, d_)),
            pl.BlockSpec((d_, nw), lambda b, i: (0, 0), pipeline_mode=pl.Buffered(1)),
            const((1, D_NSA)), const((1, 4 * D_KV)),
            const((D_NSA, LANES)), const((LANES, D_NSA)),
        ],
        out_specs=[
            pl.BlockSpec((1, tm, D_NSA), lambda b, i: (b, i, 0)),
            pl.BlockSpec((1, tm, 2 * D_KV), lambda b, i: (b, i, 0)),
            pl.BlockSpec((1, tm, 4 * D_KV), lambda b, i: (b, i, 0)),
            pl.BlockSpec((1, tm, NSA_KV * LANES), lambda b, i: (b, i, 0)),
        ],
        out_shape=[
            jax.ShapeDtypeStruct((b_, t_, D_NSA), MXU_DTYPE),
            jax.ShapeDtypeStruct((b_, t_, 2 * D_KV), MXU_DTYPE),
            jax.ShapeDtypeStruct((b_, t_, 4 * D_KV), MXU_DTYPE),
            jax.ShapeDtypeStruct((b_, t_, NSA_KV * LANES), F32),
        ],
        compiler_params=_params(("parallel", "parallel")),
        name="attn_proj",
    )(x, ng, w, gq, gkv, jnp.asarray(hs / HEAD_DIM, MXU_DTYPE), jnp.asarray(hs.T, BF16))


def _sgu_proj_kernel(x_ref, ng_ref, w_ref, lng_ref, lnb_ref, gu_out, vn_out):
    h = _rms(x_ref[0], ng_ref[...]).astype(MXU_DTYPE)
    gu_out[0] = _gelu(_dot(h, w_ref[:, 0:D_SG])).astype(gu_out.dtype)
    gv = _gelu(_dot(h, w_ref[:, D_SG:2 * D_SG]))
    for hh in range(SG_HEADS):
        sl = slice(SG_DIM * hh, SG_DIM * (hh + 1))
        v = gv[:, sl]
        mu = jnp.mean(v, axis=-1, keepdims=True)
        var = jnp.mean(jnp.square(v - mu), axis=-1, keepdims=True)
        y = (v - mu) * lax.rsqrt(var + EPS) * lng_ref[hh:hh + 1, :] + lnb_ref[hh:hh + 1, :]
        vn_out[0, :, sl] = y.astype(vn_out.dtype)


def _sgu_proj(x, ng, w, lng, lnb, tm):
    b_, t_, d_ = x.shape
    return pl.pallas_call(
        _sgu_proj_kernel,
        grid=(b_, t_ // tm),
        in_specs=[
            pl.BlockSpec((1, tm, d_), lambda b, i: (b, i, 0)),
            pl.BlockSpec((1, d_), lambda b, i: (0, 0)),
            pl.BlockSpec((d_, 2 * D_SG), lambda b, i: (0, 0), pipeline_mode=pl.Buffered(1)),
            pl.BlockSpec((SG_HEADS, SG_DIM), lambda b, i: (0, 0)),
            pl.BlockSpec((SG_HEADS, SG_DIM), lambda b, i: (0, 0)),
        ],
        out_specs=[
            pl.BlockSpec((1, tm, D_SG), lambda b, i: (b, i, 0)),
            pl.BlockSpec((1, tm, D_SG), lambda b, i: (b, i, 0)),
        ],
        out_shape=[
            jax.ShapeDtypeStruct((b_, t_, D_SG), MXU_DTYPE),
            jax.ShapeDtypeStruct((b_, t_, D_SG), MXU_DTYPE),
        ],
        compiler_params=_params(("parallel", "parallel")),
        name="sgu_proj",
    )(x, ng, w, lng, lnb)


def _compress_kernel(x_ref, w1_ref, w2_ref, pe_ref, kg_ref, out_ref):
    which = pl.program_id(1)
    x = x_ref[0, 0, 0]
    nch = x.shape[0]
    half = CMP_STRIDE * HEAD_DIM
    a = _dot(x, w1_ref[0, 0, 0:half, :])
    bm = _dot(x, w1_ref[0, 0, half:2 * half, :])
    pec = _dot(pe_ref[0, 0], w1_ref[0, 0])
    hid = _gelu(a + pltpu.roll(bm, shift=nch - 1, axis=0) + pec)
    y = _dot(hid, w2_ref[0, 0])
    y = jnp.where(which == 0, _rms(y, kg_ref[0:1, :]), y)
    lane = lax.broadcasted_iota(jnp.int32, (CMP_PAD, LANES), 1)
    flag = ((lane == PAD_FLAG_LANE) & (which == 0)).astype(out_ref.dtype)
    out_ref[0, 0, 0, 0:CMP_PAD, :] = flag
    out_ref[0, 0, 0, CMP_PAD:CMP_PAD + nch, 0:HEAD_DIM] = y.astype(out_ref.dtype)
    out_ref[0, 0, 0, CMP_PAD:CMP_PAD + nch, HEAD_DIM:LANES] = jnp.zeros((nch, LANES - HEAD_DIM), out_ref.dtype)


def _compress(xc, w1, w2, pe, kg):
    b_, _, g_, nch, fl = xc.shape
    return pl.pallas_call(
        _compress_kernel,
        grid=(b_, 2, g_),
        in_specs=[
            pl.BlockSpec((1, 1, 1, nch, fl), lambda b, w, g: (b, w, g, 0, 0)),
            pl.BlockSpec((1, 1, 2 * fl, CMP_HIDDEN), lambda b, w, g: (w, g, 0, 0)),
            pl.BlockSpec((1, 1, CMP_HIDDEN, HEAD_DIM), lambda b, w, g: (w, g, 0, 0)),
            pl.BlockSpec((1, 1, 1, 2 * fl), lambda b, w, g: (w, g, 0, 0)),
            pl.BlockSpec((3, HEAD_DIM), lambda b, w, g: (0, 0)),
        ],
        out_specs=pl.BlockSpec((1, 1, 1, CMP_PAD + nch, LANES), lambda b, w, g: (b, w, g, 0, 0)),
        out_shape=jax.ShapeDtypeStruct((b_, 2, g_, CMP_PAD + nch, LANES), F32),
        compiler_params=_params(("parallel", "parallel", "parallel")),
        name="nsa_compress",
    )(xc, w1, w2, pe, kg)


def _nsa_kernel(q_ref, gate_ref, kc_ref, vc_ref, ks_ref, vs_ref, kw_ref, vw_ref,
                tbm_ref, wbm_ref, cbm_ref, qmid_ref, ov_ref, o_ref, qa_ref,
                accs_ref, accw_ref, oc_ref):
    i = jnp.minimum(pl.program_id(2), pl.num_programs(2) - 2)
    qb = Q_BLOCK
    rows = NSA_REP * qb
    ncmp = CMP_PAD

    @pl.when((pl.program_id(0) == 0) & (pl.program_id(1) == 0) & (pl.program_id(2) == 0))
    def _():
        accs_ref[...] = jnp.ones_like(accs_ref)
        accw_ref[...] = jnp.ones_like(accw_ref)
        oc_ref[...] = jnp.zeros_like(oc_ref)

    gt = gate_ref[0]
    for r in range(NSA_REP):
        rs = slice(qb * r, qb * (r + 1))
        o_s = accs_ref[rs, 0:HEAD_DIM] / accs_ref[rs, HEAD_DIM:HEAD_DIM + 1]
        o_w = accw_ref[rs, 0:HEAD_DIM] / accw_ref[rs, HEAD_DIM:HEAD_DIM + 1]
        o = (gt[:, 3 * r + 0:3 * r + 1] * oc_ref[rs, :] + gt[:, 3 * r + 1:3 * r + 2] * o_s
             + gt[:, 3 * r + 2:3 * r + 3] * o_w)
        o_ref[0, :, HEAD_DIM * r:HEAD_DIM * (r + 1)] = o.astype(o_ref.dtype)

    qblk = q_ref[0]
    for r in range(NSA_REP):
        rs = slice(qb * r, qb * (r + 1))
        qa_ref[rs, 0:HEAD_DIM] = qblk[:, HEAD_DIM * r:HEAD_DIM * (r + 1)]
        qa_ref[rs, HEAD_DIM:LANES] = jnp.broadcast_to(
            qmid_ref[0, r][:, 0:LANES - HEAD_DIM], (qb, LANES - HEAD_DIM)).astype(qa_ref.dtype)
    q_lo = qa_ref[:, 0:LANES]

    wstart = pl.multiple_of(KEY_PAD + qb * (i + 1) - WIN_KEYS, qb)
    s_w = _dot_nt(q_lo, kw_ref[0, 0, pl.ds(wstart, WIN_KEYS), :])

    def window_probs(r):
        s = s_w[qb * r:qb * (r + 1)] + wbm_ref[0, r]
        return jnp.exp2(s - jnp.max(s, axis=-1, keepdims=True))

    cstart = pl.multiple_of(8 * i + 8, 8)
    kc = kc_ref[0, 0, 0, pl.ds(cstart, ncmp), :]
    vc = vc_ref[0, 0, 0, pl.ds(cstart, ncmp), :]
    s_c = _dot_nt(q_lo, kc)

    kt = SEL_TILE
    n_tiles = (i * qb + qb + kt - 1) // kt

    def tile_start(dd):
        return pl.multiple_of(KEY_PAD + qb * (i + 1) - kt * (dd + 1), qb)

    near_base = [_dot_nt(q_lo, ks_ref[0, 0, pl.ds(tile_start(dd), kt), 0:LANES])
                 for dd in range(NEAR_TILES)]
    psum = jnp.zeros((qb, ncmp), F32)
    for r in range(NSA_REP):
        s = s_c[qb * r:qb * (r + 1)] + cbm_ref[0, r]
        m = jnp.max(s, axis=-1, keepdims=True)
        p = jnp.exp2(s - m)
        l = jnp.sum(p, axis=-1, keepdims=True)
        p = p * jnp.where(m > 0.5 * NEG, 1.0 / l, 0.0)
        oc_ref[qb * r:qb * (r + 1), :] = _dot(p, vc)[:, 0:HEAD_DIM]
        psum = psum + p
    p_hi = psum.astype(BF16)
    p_lo = (psum - p_hi.astype(F32)).astype(BF16)
    ov = ov_ref[...]
    imp = (jnp.dot(p_hi, ov, preferred_element_type=F32)
           + jnp.dot(p_lo, ov, preferred_element_type=F32))

    nsel = LANES
    sp = lax.broadcasted_iota(jnp.int32, (qb, nsel), 1)
    rr = lax.broadcasted_iota(jnp.int32, (qb, nsel), 0)
    cur = (nsel - 2) + (rr >= SLC_BLOCK).astype(jnp.int32)
    s_abs = sp + (2 * i + 2 - nsel)
    valid = s_abs >= 0
    forced = ((s_abs == 0) | (sp == cur) | (sp == cur - 1)) & valid
    excluded = forced | (sp > cur) | (~valid)
    cand_t = jnp.where(excluded, NEG, imp).T
    v = cand_t
    tau = None
    n_rank = SLC_TOPN - N_FORCED
    p_w = []
    for it in range(n_rank):
        tau = jnp.max(v, axis=0, keepdims=True)
        v = jnp.where(v >= tau, 3.0 * NEG, v)
        if it % (n_rank // NSA_REP) == 0 and len(p_w) < NSA_REP:
            p_w.append(window_probs(len(p_w)))
    accw_ref[...] = _dot(jnp.concatenate(p_w, axis=0), vw_ref[0, 0, pl.ds(wstart, WIN_KEYS), :])
    picked = jnp.where((cand_t >= tau) & (cand_t > 0.5 * NEG), 1.0, 0.0).T
    sel_neg = jnp.where((picked > 0.5) | forced, 0.0, NEG)
    sel_neg = pltpu.roll(sel_neg, shift=(2 * i + 2) % nsel, axis=1).astype(qa_ref.dtype)
    for r in range(NSA_REP):
        qa_ref[qb * r:qb * (r + 1), LANES:2 * LANES] = sel_neg
    q_ext = qa_ref[...]

    def tile_scores(dd):
        return _dot_nt(q_ext, ks_ref[0, 0, pl.ds(tile_start(dd), kt), :])

    def tile_softmax(s, dd):
        m_t = jnp.max(s, axis=-1, keepdims=True)
        return m_t, _dot(jnp.exp2(s - m_t), vs_ref[0, 0, pl.ds(tile_start(dd), kt), :])

    def merge(parts):
        m_new = functools.reduce(jnp.maximum, [m for m, _ in parts])
        acc = sum(jnp.exp2(m - m_new) * pv for m, pv in parts)
        return m_new, acc

    near = []
    for dd in range(NEAR_TILES):
        msk = _dot_nt(sel_neg, ks_ref[0, 0, pl.ds(tile_start(dd), kt), LANES:2 * LANES])
        near.append(jnp.concatenate([near_base[dd][qb * r:qb * (r + 1)] + (tbm_ref[0, r, dd] + msk)
                                     for r in range(NSA_REP)], axis=0))
    carry = merge([tile_softmax(s, dd) for dd, s in enumerate(near)])

    def far_group(first, width, carry):
        dds = [jnp.minimum(first + k, n_tiles) for k in range(width)]
        scores = [tile_scores(dd) for dd in dds]
        return merge([carry] + [tile_softmax(s, dd) for s, dd in zip(scores, dds)])

    n_far = jnp.maximum(n_tiles - NEAR_TILES, 0)
    n_groups = n_far // FAR_GROUP
    carry = lax.fori_loop(
        0, n_groups, lambda u, c: far_group(NEAR_TILES + FAR_GROUP * u, FAR_GROUP, c), carry)
    rest = NEAR_TILES + FAR_GROUP * n_groups
    carry = lax.fori_loop(
        0, (n_far - FAR_GROUP * n_groups + 1) // 2, lambda u, c: far_group(rest + 2 * u, 2, c), carry)
    accs_ref[...] = carry[1]


def _nsa_attention(qn, gate, kcv, ksa, vsp, kwa, vwp, tbm, wbm, cbm, qmid, ov):
    b_, t_, _ = qn.shape
    tp = ksa.shape[2]
    ncp = kcv.shape[3]
    gw = NSA_REP * HEAD_DIM
    assert t_ // SLC_BLOCK <= LANES
    kvspec = lambda wd: pl.BlockSpec((1, 1, tp, wd), lambda b, g, i: (b, g, 0, 0))
    nq = t_ // Q_BLOCK
    behind = lambda b, g, i: (b, jnp.maximum(i - 1, 0), g)
    return pl.pallas_call(
        _nsa_kernel,
        grid=(b_, NSA_KV, nq + 1),
        in_specs=[
            pl.BlockSpec((1, Q_BLOCK, gw), lambda b, g, i: (b, jnp.minimum(i, nq - 1), g)),
            pl.BlockSpec((1, Q_BLOCK, LANES), behind),
            pl.BlockSpec((1, 1, 1, ncp, LANES), lambda b, g, i: (b, 0, g, 0, 0)),
            pl.BlockSpec((1, 1, 1, ncp, LANES), lambda b, g, i: (b, 1, g, 0, 0)),
            kvspec(2 * LANES), kvspec(LANES), kvspec(LANES), kvspec(LANES),
            pl.BlockSpec((1, NSA_REP, NEAR_TILES, Q_BLOCK, SEL_TILE), lambda b, g, i: (g, 0, 0, 0, 0)),
            pl.BlockSpec((1, NSA_REP, Q_BLOCK, WIN_KEYS), lambda b, g, i: (g, 0, 0, 0)),
            pl.BlockSpec((1, NSA_REP, Q_BLOCK, CMP_PAD), lambda b, g, i: (g, 0, 0, 0)),
            pl.BlockSpec((1, NSA_REP, 1, LANES), lambda b, g, i: (g, 0, 0, 0)),
            pl.BlockSpec((CMP_PAD, LANES), lambda b, g, i: (0, 0)),
        ],
        out_specs=pl.BlockSpec((1, Q_BLOCK, gw), behind),
        out_shape=jax.ShapeDtypeStruct((b_, t_, D_NSA), F32),
        scratch_shapes=[
            pltpu.VMEM((NSA_REP * Q_BLOCK, 2 * LANES), MXU_DTYPE),
            pltpu.VMEM((NSA_REP * Q_BLOCK, LANES), F32),
            pltpu.VMEM((NSA_REP * Q_BLOCK, LANES), F32),
            pltpu.VMEM((NSA_REP * Q_BLOCK, HEAD_DIM), F32),
        ],
        compiler_params=_params(("parallel", "parallel", "arbitrary")),
        name="nsa_attention",
    )(qn, gate, kcv, kcv, ksa, vsp, kwa, vwp, tbm, wbm, cbm, qmid, ov)


def _mix_kernel(x_ref, on_ref, gu_ref, vn_ref, sgw_ref, sgb_ref, gn_ref, gs_ref, wo_ref,
                fg_ref, wq_ref, x1_out, h2t_out, qp_out):
    tm = x_ref.shape[0]
    tri = (lax.broadcasted_iota(jnp.int32, (SG_CHUNK, SG_CHUNK), 0)
           >= lax.broadcasted_iota(jnp.int32, (SG_CHUNK, SG_CHUNK), 1))
    sgb = sgb_ref[...]
    y = x_ref[...] + _dot(_rms(on_ref[...], gn_ref[...]), wo_ref[0:D_NSA, :])
    parts = []
    for c in range(tm // SG_CHUNK):
        cs = slice(SG_CHUNK * c, SG_CHUNK * (c + 1))
        heads = []
        for hh in range(SG_HEADS):
            sl = slice(SG_DIM * hh, SG_DIM * (hh + 1))
            w = jnp.where(tri, sgw_ref[hh], 0.0)
            mixed = _dot(w, vn_ref[cs, sl]) + sgb[:, hh:hh + 1]
            heads.append(gu_ref[cs, sl].astype(F32) * mixed)
        parts.append(jnp.concatenate(heads, axis=1))
    o_sg = jnp.concatenate(parts, axis=0)
    y = y + _dot(_rms(o_sg, gs_ref[...]), wo_ref[D_NSA:D_NSA + D_SG, :])
    x1_out[...] = y
    h2 = _rms(y, fg_ref[...])
    h2t_out[...] = h2.T.astype(h2t_out.dtype)
    qp_out[...] = _dot(h2, wq_ref[...])


def _mix(x2, onsa, gu, vn, sgw, sgb_t, gn, gs, wo, fg, wq, tm):
    n_, d_ = x2.shape
    row = lambda wd: pl.BlockSpec((tm, wd), lambda i: (i, 0))
    const = lambda shape, **kw: pl.BlockSpec(shape, lambda i: (0,) * len(shape), **kw)
    return pl.pallas_call(
        _mix_kernel,
        grid=(n_ // tm,),
        in_specs=[
            row(d_), row(D_NSA), row(D_SG), row(D_SG),
            const((SG_HEADS, SG_CHUNK, SG_CHUNK)), const((SG_CHUNK, SG_HEADS)),
            const((1, D_NSA)), const((1, D_SG)),
            const((D_NSA + D_SG, d_), pipeline_mode=pl.Buffered(1)),
            const((1, d_)),
            const((d_, wq.shape[1]), pipeline_mode=pl.Buffered(1)),
        ],
        out_specs=[row(d_), pl.BlockSpec((d_, tm), lambda i: (0, i)), row(wq.shape[1])],
        out_shape=[
            jax.ShapeDtypeStruct((n_, d_), F32),
            jax.ShapeDtypeStruct((d_, n_), MXU_DTYPE),
            jax.ShapeDtypeStruct((n_, wq.shape[1]), F32),
        ],
        compiler_params=_params(("parallel",)),
        name="mix_out_proj",
    )(x2, onsa, gu, vn, sgw, sgb_t, gn, gs, wo, fg, wq)


def _top_rows(v, k):
    tops = []
    for _ in range(k):
        m = jnp.max(v, axis=0, keepdims=True)
        tops.append(m)
        v = jnp.where(v >= m, 0.0, v)
    return jnp.concatenate(tops, axis=0)


def _pair_products(a, b):
    cand = [a[0:1] * b]
    for ra in range(1, 8):
        cand.append(a[ra:ra + 1] * b[0:8])
    cand.append(a[8:16] * b[0:1])
    return jnp.concatenate(cand, axis=0)


def _peer_topk_kernel(qp_ref, sk_ref, e1_out, e2_out, eps_out):
    for hh in range(PEER_HEADS):
        base = 2 * PEER_HALF * hh
        s1 = _dot_nt(sk_ref[0], qp_ref[:, base:base + PEER_HALF])
        s2 = _dot_nt(sk_ref[1], qp_ref[:, base + PEER_HALF:base + 2 * PEER_HALF])
        e1 = jnp.exp(s1 - jnp.max(s1, axis=0, keepdims=True))
        e2 = jnp.exp(s2 - jnp.max(s2, axis=0, keepdims=True))
        a = _top_rows(e1, PEER_TOPK)
        b = _top_rows(e2, PEER_TOPK)
        z = jnp.sum(_top_rows(_pair_products(a, b), PEER_TOPK), axis=0, keepdims=True)
        zinv = 1.0 / z
        gates = _top_rows(_pair_products(a * zinv, b), PEER_TOPK)
        e1_out[hh] = e1 * zinv
        for j in range(e2.shape[1] // LANES):
            e2_out[hh, j] = e2[:, LANES * j:LANES * (j + 1)]
        eps_out[hh:hh + 1, :] = gates[PEER_TOPK - 1:PEER_TOPK]


def _peer_topk(qp, sub_keys, tm):
    n_, qd = qp.shape
    return pl.pallas_call(
        _peer_topk_kernel,
        grid=(n_ // tm,),
        in_specs=[
            pl.BlockSpec((tm, qd), lambda i: (i, 0)),
            pl.BlockSpec((2, PEER_NKEYS, PEER_HALF), lambda i: (0, 0, 0)),
        ],
        out_specs=[
            pl.BlockSpec((PEER_HEADS, PEER_NKEYS, tm), lambda i: (0, 0, i)),
            pl.BlockSpec((PEER_HEADS, tm // LANES, PEER_NKEYS, LANES), lambda i: (0, i, 0, 0)),
            pl.BlockSpec((PEER_HEADS, tm), lambda i: (0, i)),
        ],
        out_shape=[
            jax.ShapeDtypeStruct((PEER_HEADS, PEER_NKEYS, n_), F32),
            jax.ShapeDtypeStruct((PEER_HEADS, n_ // LANES, PEER_NKEYS, LANES), F32),
            jax.ShapeDtypeStruct((PEER_HEADS, n_), F32),
        ],
        compiler_params=_params(("parallel",)),
        name="peer_topk",
    )(qp, sub_keys)


def _peer_mix_kernel(n_tiles, ht_ref, u_ref, vt_ref, e1_ref, e2_ref, eps_ref, x1_ref, out_ref,
                     acc_ref, crow_ref, a0_ref, a1_ref, z0_ref, z1_ref):
    ie = pl.program_id(1)
    te = u_ref.shape[0]
    tm = ht_ref.shape[1]
    per_tile = te // PEER_NKEYS

    @pl.when(ie == 0)
    def _():
        acc_ref[...] = jnp.zeros_like(acc_ref)

    d_ = vt_ref.shape[1]
    n_lane = tm // LANES
    n_blocks = per_tile * n_lane
    a_units = [(r, c, kc) for r in range(te // MXU_ROWS) for c in range(tm // MXU_ROWS)
               for kc in range(d_ // A_KCHUNK)]
    c_units = [(r, c) for r in range(d_ // MXU_ROWS) for c in range(tm // MXU_ROWS)]

    def step(a_w, a_r, z_w, z_r):
        tile = ie - 1

        def a_unit(r, c, kc):
            rs = slice(MXU_ROWS * r, MXU_ROWS * (r + 1))
            cs = slice(MXU_ROWS * c, MXU_ROWS * (c + 1))
            ds = slice(A_KCHUNK * kc, A_KCHUNK * (kc + 1))
            res = jnp.dot(u_ref[rs, ds], ht_ref[ds, cs], preferred_element_type=F32)
            for jj in range(MXU_ROWS // LANES):
                j = c * (MXU_ROWS // LANES) + jj
                part = res[:, LANES * jj:LANES * (jj + 1)]
                if kc == 0:
                    a_w[j, rs, :] = part
                else:
                    a_w[j, rs, :] += part

        def c_unit(r, c):
            rs = slice(MXU_ROWS * r, MXU_ROWS * (r + 1))
            cs = slice(MXU_ROWS * c, MXU_ROWS * (c + 1))
            acc_ref[rs, cs] += jnp.dot(vt_ref[0, rs, :], z_r[:, cs], preferred_element_type=F32)

        def b_block(k, j):
            ls = slice(LANES * j, LANES * (j + 1))
            for part in range(PEER_NKEYS // GATE_ROWS):
                ks = slice(GATE_ROWS * part, GATE_ROWS * (part + 1))
                rs = slice(PEER_NKEYS * k + GATE_ROWS * part, PEER_NKEYS * k + GATE_ROWS * (part + 1))
                g = jnp.zeros((GATE_ROWS, LANES), F32)
                for hh in range(PEER_HEADS):
                    row = PEER_HEADS * k + hh
                    gate = e2_ref[hh, j, ks, :] * crow_ref[row:row + 1, ls]
                    g = g + jnp.where(gate >= eps_ref[hh:hh + 1, ls], gate, 0.0)
                z_w[rs, ls] = (_gelu(a_r[j, rs, :]) * g).astype(z_w.dtype)

        do_a, do_b, do_c = a_w is not None, z_w is not None, z_r is not None
        if do_b:
            for k in range(per_tile):
                i1 = tile * per_tile + k
                for hh in range(PEER_HEADS):
                    row = PEER_HEADS * k + hh
                    crow_ref[row:row + 1, :] = e1_ref[hh, pl.ds(i1, 1), :]

        a_iter = iter(a_units if do_a else [])
        c_iter = iter(c_units if do_c else [])
        a_every = max(n_blocks // len(a_units), 1)
        c_per = -(-len(c_units) // n_blocks)
        for blk in range(n_blocks):
            k, j = divmod(blk, n_lane)
            if blk % a_every == 0:
                unit = next(a_iter, None)
                if unit is not None:
                    a_unit(*unit)
            for _ in range(c_per):
                unit = next(c_iter, None)
                if unit is not None:
                    c_unit(*unit)
            if do_b:
                b_block(k, j)
        for unit in a_iter:
            a_unit(*unit)
        for unit in c_iter:
            c_unit(*unit)

    a_bufs, z_bufs = (a0_ref, a1_ref), (z0_ref, z1_ref)

    def run(par, do_a=True, do_b=True, do_c=True):
        step(a_bufs[par] if do_a else None, a_bufs[1 - par], z_bufs[1 - par] if do_b else None,
             z_bufs[par] if do_c else None)

    last = n_tiles + 1
    pl.when(ie == 0)(lambda: run(0, do_b=False, do_c=False))
    pl.when(ie == 1)(lambda: run(1, do_c=False))
    pl.when((ie >= 2) & (ie < n_tiles) & (ie % 2 == 0))(lambda: run(0))
    pl.when((ie >= 2) & (ie < n_tiles) & (ie % 2 == 1))(lambda: run(1))
    pl.when(ie == n_tiles)(lambda: run(n_tiles % 2, do_a=False))
    pl.when(ie == last)(lambda: run(last % 2, do_a=False, do_b=False))

    @pl.when(ie == last)
    def _():
        out_ref[...] = x1_ref[...] + acc_ref[...].T


def _peer_mix(h2t, u, vt, e1t, e2t, eps, x1, tm, te):
    d_, n_ = h2t.shape
    n_tiles = u.shape[0] // te
    assert n_tiles >= 3
    return pl.pallas_call(
        functools.partial(_peer_mix_kernel, n_tiles),
        grid=(n_ // tm, n_tiles + 2),
        in_specs=[
            pl.BlockSpec((d_, tm), lambda it, ie: (0, it)),
            pl.BlockSpec((te, d_), lambda it, ie: (jnp.minimum(ie, n_tiles - 1), 0)),
            pl.BlockSpec((1, d_, te), lambda it, ie: (jnp.maximum(ie - 2, 0), 0, 0)),
            pl.BlockSpec((PEER_HEADS, PEER_NKEYS, tm), lambda it, ie: (0, 0, it)),
            pl.BlockSpec((PEER_HEADS, tm // LANES, PEER_NKEYS, LANES), lambda it, ie: (0, it, 0, 0)),
            pl.BlockSpec((PEER_HEADS, tm), lambda it, ie: (0, it)),
            pl.BlockSpec((tm, d_), lambda it, ie: (it, 0)),
        ],
        out_specs=pl.BlockSpec((tm, d_), lambda it, ie: (it, 0)),
        out_shape=jax.ShapeDtypeStruct((n_, d_), F32),
        scratch_shapes=[
            pltpu.VMEM((d_, tm), F32),
            pltpu.VMEM((PEER_HEADS * (te // PEER_NKEYS), tm), F32),
            pltpu.VMEM((tm // LANES, te, LANES), F32), pltpu.VMEM((tm // LANES, te, LANES), F32),
            pltpu.VMEM((te, tm), MXU_DTYPE), pltpu.VMEM((te, tm), MXU_DTYPE),
        ],
        compiler_params=_params(("parallel", "arbitrary")),
        name="peer_mix",
    )(h2t, u, vt, e1t, e2t, eps, x1)


def _rel_bucket(dist):
    n = jnp.maximum(dist, 0)
    max_exact = REL_BUCKETS // 2
    nf = jnp.maximum(n, 1).astype(F32)
    large = max_exact + (jnp.log(nf / max_exact) / math.log(REL_MAX_DIST / max_exact)
                         * (REL_BUCKETS - max_exact)).astype(jnp.int32)
    large = jnp.minimum(large, REL_BUCKETS - 1)
    return jnp.where(n < max_exact, n, large)


def _toeplitz(btab, off, rows, cols):
    per = rows + cols
    k = np.concatenate([np.arange(cols), np.zeros(1, np.int64), np.arange(-(rows - 1), 0)])
    idx = np.clip(off - k, 0, btab.shape[1] - 1)
    w = jnp.take(btab, jnp.asarray(idx, jnp.int32), axis=1)
    x = jnp.tile(w, (1, rows))[:, :rows * (per - 1)]
    return x.reshape(btab.shape[0], rows, per - 1)[:, :, :cols]


def _bias_tiles(rel_bias):
    dist = jnp.arange(BIAS_TABLE, dtype=jnp.int32)
    rel_bias = rel_bias.astype(F32) * LOG2E
    btab = rel_bias[_rel_bucket(dist)].T
    tbw = jnp.stack([_toeplitz(btab, SEL_TILE * (dd + 1) - Q_BLOCK, Q_BLOCK, SEL_TILE)
                     for dd in range(NEAR_TILES)], axis=1)
    wb = _toeplitz(btab, WINDOW, Q_BLOCK, WIN_KEYS)
    off = CMP_STRIDE * (LANES - 8) - (CMP_BLOCK - 1)
    front = CMP_STRIDE * LANES - off
    length = CMP_STRIDE * (LANES + 8 + 2)
    padded = jnp.pad(btab, ((0, 0), (front, length - front - BIAS_TABLE)))
    ch = padded.reshape(btab.shape[0], length // CMP_STRIDE, CMP_STRIDE)
    nwin = LANES + 1
    win = jnp.stack([ch[:, a:a + nwin] for a in range(Q_BLOCK // CMP_STRIDE)], axis=2)
    win = win.reshape(btab.shape[0], nwin, Q_BLOCK)
    cb = jnp.swapaxes(win[:, 1:nwin][:, ::-1], 1, 2)
    cfar = rel_bias.astype(F32)[REL_BUCKETS - 1]
    grp = lambda a: a.reshape((NSA_KV, NSA_REP) + a.shape[1:])

    r = np.arange(Q_BLOCK)[:, None]
    near_ok = np.stack([r - np.arange(SEL_TILE)[None, :] + SEL_TILE * (dd + 1) - Q_BLOCK >= 0
                        for dd in range(NEAR_TILES)])
    tbm = jnp.where(near_ok, tbw - cfar[:, None, None, None], NEG)
    dist_w = r - np.arange(WIN_KEYS)[None, :] + WINDOW
    wbm = jnp.where((dist_w >= 0) & (dist_w < WINDOW), wb, NEG)
    dist_c = r - CMP_STRIDE * np.arange(CMP_PAD)[None, :] + (CMP_STRIDE * CMP_PAD - Q_BLOCK - (CMP_BLOCK - 1))
    cbias = jnp.concatenate(
        [jnp.broadcast_to(cfar[:, None, None], (NSA_HEADS, Q_BLOCK, CMP_PAD - LANES)), cb], axis=2)
    cbm = jnp.where(dist_c >= 0, cbias, NEG)

    hi = cfar.astype(BF16).astype(F32)
    lo = (cfar - hi).astype(BF16).astype(F32)
    qmid = jnp.zeros((NSA_HEADS, 1, LANES), F32)
    qmid = qmid.at[:, 0, FAR_HI_LANE - HEAD_DIM].set(hi).at[:, 0, FAR_LO_LANE - HEAD_DIM].set(lo)
    qmid = qmid.at[:, 0, PAD_FLAG_LANE - HEAD_DIM].set(NEG)
    return grp(tbm), grp(wbm), grp(cbm), grp(qmid)


def _extend_keys(ks, kw):
    tp = ks.shape[2] + KEY_PAD
    pos = np.arange(tp) - KEY_PAD
    real = pos >= 0
    mid_s = np.zeros((tp, LANES - HEAD_DIM), np.float32)
    mid_s[:, FAR_HI_LANE - HEAD_DIM] = real
    mid_s[:, FAR_LO_LANE - HEAD_DIM] = real
    mid_s[:, PAD_FLAG_LANE - HEAD_DIM] = ~real
    onehot = ((pos[:, None] // SLC_BLOCK) == np.arange(LANES)[None, :]) & real[:, None]
    mid_w = np.zeros((tp, LANES - HEAD_DIM), np.float32)
    mid_w[:, PAD_FLAG_LANE - HEAD_DIM] = ~real
    padk = lambda a: jnp.pad(a, ((0, 0), (0, 0), (KEY_PAD, 0), (0, 0)))
    bc = lambda a: jnp.broadcast_to(jnp.asarray(a, ks.dtype), ks.shape[:2] + a.shape)
    ksa = jnp.concatenate([padk(ks), bc(mid_s), bc(onehot.astype(np.float32))], axis=-1)
    kwa = jnp.concatenate([padk(kw), bc(mid_w)], axis=-1)
    return ksa, kwa


def _overlap_matrix():
    c = np.arange(CMP_PAD)[:, None]
    s = np.arange(LANES)[None, :]
    per = SLC_BLOCK // CMP_STRIDE
    ov = (s == c // per) | ((c % per == per - 1) & (s == c // per + 1))
    return jnp.asarray(ov.astype(np.float32), BF16)


def kernel(x, attn_norm_g, w_in, q_norm_g, k_norm_g, cmp_pe_k, cmp_w1_k, cmp_w2_k, cmp_pe_v,
           cmp_w1_v, cmp_w2_v, rel_bias, sg_ln_g, sg_ln_b, sg_w, sg_b, out_norm_nsa, out_norm_sg,
           w_out, ffn_norm_g, peer_w_query, peer_sub_keys, peer_u, peer_v):
    b_, t_, d_ = x.shape
    n_ = b_ * t_
    assert t_ % Q_BLOCK == 0 and t_ >= WIN_KEYS and d_ == D_NSA + D_SG

    c_kv = D_NSA + 6 * D_KV
    c_gate = c_kv + 3 * NSA_HEADS
    wg = w_in[:, c_kv:c_gate].reshape(d_, NSA_KV, 3 * NSA_REP)
    wg = jnp.pad(wg, ((0, 0), (0, 0), (0, LANES - 3 * NSA_REP))).reshape(d_, NSA_KV * LANES)
    w_attn = jnp.concatenate([w_in[:, :c_kv], wg], axis=1).astype(MXU_DTYPE)
    w_sgu = w_in[:, c_gate:].astype(MXU_DTYPE)
    ng = attn_norm_g.reshape(1, d_)

    gq = jnp.tile(q_norm_g, NSA_HEADS).reshape(1, D_NSA) * (HEAD_DIM ** -0.5 * LOG2E)
    no_gain = jnp.zeros((D_KV,), F32)
    gkv = jnp.concatenate([jnp.tile(k_norm_g[1], NSA_KV), no_gain,
                           jnp.tile(k_norm_g[2], NSA_KV), no_gain]).reshape(1, 4 * D_KV)
    qn, kcv, kv, gate = _attn_proj(x, ng, w_attn, gq, gkv, tm=256)
    heads = lambda a: a.reshape(b_, t_, NSA_KV, HEAD_DIM).transpose(0, 2, 1, 3)
    ones_col = jnp.zeros((b_, NSA_KV, t_, HEAD_DIM), kv.dtype).at[..., 0].set(1)
    ks, kw = heads(kv[..., 0:D_KV]), heads(kv[..., 2 * D_KV:3 * D_KV])
    vs = jnp.concatenate([heads(kv[..., D_KV:2 * D_KV]), ones_col], axis=-1)
    vw = jnp.concatenate([heads(kv[..., 3 * D_KV:4 * D_KV]), ones_col], axis=-1)
    gu, vn = _sgu_proj(x, ng, w_sgu, sg_ln_g, sg_ln_b, tm=256)

    nch = t_ // CMP_STRIDE
    xc = kcv.reshape(b_, nch, CMP_STRIDE, 2, NSA_KV, HEAD_DIM).transpose(0, 3, 4, 1, 2, 5)
    xc = xc.reshape(b_, 2, NSA_KV, nch, CMP_STRIDE * HEAD_DIM)
    w1 = jnp.stack([cmp_w1_k, cmp_w1_v]).astype(MXU_DTYPE)
    w2 = jnp.stack([cmp_w2_k, cmp_w2_v]).astype(MXU_DTYPE)
    pe = jnp.stack([cmp_pe_k, cmp_pe_v]).transpose(0, 2, 1, 3).reshape(2, NSA_KV, 1, CMP_BLOCK * HEAD_DIM)
    kcv_c = _compress(xc, w1, w2, pe.astype(MXU_DTYPE), k_norm_g)

    padk = lambda a: jnp.pad(a, ((0, 0), (0, 0), (KEY_PAD, 0), (0, 0)))
    tbm, wbm, cbm, qmid = _bias_tiles(rel_bias)
    ksa, kwa = _extend_keys(ks, kw)
    onsa = _nsa_attention(qn, gate, kcv_c, ksa, padk(vs), kwa, padk(vw),
                          tbm, wbm, cbm, qmid, _overlap_matrix())

    x1, h2t, qp = _mix(x.reshape(n_, d_), onsa.reshape(n_, D_NSA), gu.reshape(n_, D_SG),
                      vn.reshape(n_, D_SG), sg_w, sg_b.T, out_norm_nsa.reshape(1, D_NSA),
                      out_norm_sg.reshape(1, D_SG), w_out.astype(MXU_DTYPE),
                      ffn_norm_g.reshape(1, d_), peer_w_query.astype(MXU_DTYPE), tm=256)

    e1t, e2t, eps = _peer_topk(qp, peer_sub_keys, tm=256)
    te = 512
    vt = peer_v.reshape(-1, te, d_).transpose(0, 2, 1).astype(MXU_DTYPE)
    out = _peer_mix(h2t, peer_u.astype(MXU_DTYPE), vt, e1t, e2t, eps, x1, tm=min(512, n_), te=te)
    return out.reshape(b_, t_, d_)
```

```python
import functools
import math

import jax
import jax.numpy as jnp
import numpy as np
from jax import lax
from jax.experimental import pallas as pl
from jax.experimental.pallas import tpu as pltpu

F32 = jnp.float32
BF16 = jnp.bfloat16
MXU_DTYPE = BF16

EPS = 1e-6
NEG = -1e30
SQRT_HALF = 0.7071067811865476
LOG2E = 1.4426950408889634

NSA_HEADS = 16
NSA_KV = 4
NSA_REP = 4
HEAD_DIM = 64
D_NSA = NSA_HEADS * HEAD_DIM
D_KV = NSA_KV * HEAD_DIM
CMP_STRIDE = 16
CMP_BLOCK = 32
CMP_HIDDEN = 128
SLC_BLOCK = 64
SLC_TOPN = 16
WINDOW = 512
Q_BLOCK = 128
SG_HEADS = 8
SG_DIM = 128
SG_CHUNK = 128
D_SG = SG_HEADS * SG_DIM
REL_BUCKETS = 32
REL_MAX_DIST = 1024
PEER_HEADS = 8
PEER_NKEYS = 128
PEER_HALF = 128
PEER_TOPK = 16

LANES = 128
MXU_ROWS = 256
A_KCHUNK = 1024
GATE_ROWS = 64
KEY_PAD = 1024
CMP_PAD = 512
SEL_TILE = 512
WIN_KEYS = WINDOW + Q_BLOCK
NEAR_TILES = 2
FAR_GROUP = 4
BIAS_TABLE = 2048
FAR_DIST = 897
FAR_HI_LANE = 64
FAR_LO_LANE = 65
PAD_FLAG_LANE = 66
N_FORCED = 3
VMEM_LIMIT = 56 * 1024 * 1024


def _gelu(x):
    return 0.5 * x * (1.0 + lax.erf(x * SQRT_HALF))


def _dot(a, b):
    return jnp.dot(a.astype(MXU_DTYPE), b.astype(MXU_DTYPE), preferred_element_type=F32)


def _dot_nt(a, b):
    return lax.dot_general(a.astype(MXU_DTYPE), b.astype(MXU_DTYPE),
                           (((1,), (1,)), ((), ())), preferred_element_type=F32)


def _rms(x, g):
    ms = jnp.mean(x * x, axis=-1, keepdims=True)
    return x * lax.rsqrt(ms + EPS) * g


def _params(sem):
    return pltpu.CompilerParams(dimension_semantics=sem, vmem_limit_bytes=VMEM_LIMIT)


def _attn_proj_kernel(x_ref, ng_ref, w_ref, gq_ref, gkv_ref, hs_ref, he_ref,
                      q_out, kcv_out, kv_out, gate_out):
    h = _rms(x_ref[0], ng_ref[...]).astype(MXU_DTYPE)

    def head_rms(p):
        ms = jnp.dot((p * p).astype(MXU_DTYPE), hs_ref[...], preferred_element_type=F32)
        r = lax.rsqrt(ms + EPS)
        r_hi = r.astype(BF16)
        r_lo = (r - r_hi.astype(F32)).astype(BF16)
        he = he_ref[...]
        return p * (jnp.dot(r_hi, he, preferred_element_type=F32)
                    + jnp.dot(r_lo, he, preferred_element_type=F32))

    pq = _dot(h, w_ref[:, 0:D_NSA])
    q_out[0] = (head_rms(pq) * gq_ref[...]).astype(q_out.dtype)

    kcv_out[0] = _dot(h, w_ref[:, D_NSA:D_NSA + 2 * D_KV]).astype(kcv_out.dtype)

    base = D_NSA + 2 * D_KV
    p4 = _dot(h, w_ref[:, base:base + 4 * D_KV])
    col = lax.broadcasted_iota(jnp.int32, (1, 4 * D_KV), 1)
    is_key = (col // D_KV) % 2 == 0
    kv_out[0] = jnp.where(is_key, head_rms(p4) * gkv_ref[...], p4).astype(kv_out.dtype)

    base = base + 4 * D_KV
    gate_out[0] = jax.nn.sigmoid(_dot(h, w_ref[:, base:base + NSA_KV * LANES]))


def _attn_proj(x, ng, w, gq, gkv, tm):
    b_, t_, d_ = x.shape
    nw = w.shape[1]
    heads = np.arange(D_NSA) // HEAD_DIM
    hs = (heads[:, None] == np.arange(LANES)[None, :]).astype(np.float32)
    const = lambda shape: pl.BlockSpec(shape, lambda b, i: (0,) * len(shape))
    return pl.pallas_call(
        _attn_proj_kernel,
        grid=(b_, t_ // tm),
        in_specs=[
            pl.BlockSpec((1, tm, d_), lambda b, i: (b, i, 0)),
            const((1, d_)),
            pl.BlockSpec((d_, nw), lambda b, i: (0, 0), pipeline_mode=pl.Buffered(1)),
            const((1, D_NSA)), const((1, 4 * D_KV)),
            const((D_NSA, LANES)), const((LANES, D_NSA)),
        ],
        out_specs=[
            pl.BlockSpec((1, tm, D_NSA), lambda b, i: (b, i, 0)),
            pl.BlockSpec((1, tm, 2 * D_KV), lambda b, i: (b, i, 0)),
            pl.BlockSpec((1, tm, 4 * D_KV), lambda b, i: (b, i, 0)),
            pl.BlockSpec((1, tm, NSA_KV * LANES), lambda b, i: (b, i, 0)),
        ],
        out_shape=[
            jax.ShapeDtypeStruct((b_, t_, D_NSA), MXU_DTYPE),
            jax.ShapeDtypeStruct((b_, t_, 2 * D_KV), MXU_DTYPE),
            jax.ShapeDtypeStruct((b_, t_, 4 * D_KV), MXU_DTYPE),
            jax.ShapeDtypeStruct((b_, t_, NSA_KV * LANES), F32),
        ],
        compiler_params=_params(("parallel", "parallel")),
        name="attn_proj",
    )(x, ng, w, gq, gkv, jnp.asarray(hs / HEAD_DIM, MXU_DTYPE), jnp.asarray(hs.T, BF16))


def _sgu_proj_kernel(x_ref, ng_ref, w_ref, lng_ref, lnb_ref, gu_out, vn_out):
    h = _rms(x_ref[0], ng_ref[...]).astype(MXU_DTYPE)
    gu_out[0] = _gelu(_dot(h, w_ref[:, 0:D_SG])).astype(gu_out.dtype)
    gv = _gelu(_dot(h, w_ref[:, D_SG:2 * D_SG]))
    for hh in range(SG_HEADS):
        sl = slice(SG_DIM * hh, SG_DIM * (hh + 1))
        v = gv[:, sl]
        mu = jnp.mean(v, axis=-1, keepdims=True)
        var = jnp.mean(jnp.square(v - mu), axis=-1, keepdims=True)
        y = (v - mu) * lax.rsqrt(var + EPS) * lng_ref[hh:hh + 1, :] + lnb_ref[hh:hh + 1, :]
        vn_out[0, :, sl] = y.astype(vn_out.dtype)


def _sgu_proj(x, ng, w, lng, lnb, tm):
    b_, t_, d_ = x.shape
    return pl.pallas_call(
        _sgu_proj_kernel,
        grid=(b_, t_ // tm),
        in_specs=[
            pl.BlockSpec((1, tm, d_), lambda b, i: (b, i, 0)),
            pl.BlockSpec((1, d_), lambda b, i: (0, 0)),
            pl.BlockSpec((d_, 2 * D_SG), lambda b, i: (0, 0), pipeline_mode=pl.Buffered(1)),
            pl.BlockSpec((SG_HEADS, SG_DIM), lambda b, i: (0, 0)),
            pl.BlockSpec((SG_HEADS, SG_DIM), lambda b, i: (0, 0)),
        ],
        out_specs=[
            pl.BlockSpec((1, tm, D_SG), lambda b, i: (b, i, 0)),
            pl.BlockSpec((1, tm, D_SG), lambda b, i: (b, i, 0)),
        ],
        out_shape=[
            jax.ShapeDtypeStruct((b_, t_, D_SG), MXU_DTYPE),
            jax.ShapeDtypeStruct((b_, t_, D_SG), MXU_DTYPE),
        ],
        compiler_params=_params(("parallel", "parallel")),
        name="sgu_proj",
    )(x, ng, w, lng, lnb)


def _compress_kernel(x_ref, w1_ref, w2_ref, pe_ref, kg_ref, out_ref):
    which = pl.program_id(1)
    x = x_ref[0, 0, 0]
    nch = x.shape[0]
    half = CMP_STRIDE * HEAD_DIM
    a = _dot(x, w1_ref[0, 0, 0:half, :])
    bm = _dot(x, w1_ref[0, 0, half:2 * half, :])
    pec = _dot(pe_ref[0, 0], w1_ref[0, 0])
    hid = _gelu(a + pltpu.roll(bm, shift=nch - 1, axis=0) + pec)
    y = _dot(hid, w2_ref[0, 0])
    y = jnp.where(which == 0, _rms(y, kg_ref[0:1, :]), y)
    lane = lax.broadcasted_iota(jnp.int32, (CMP_PAD, LANES), 1)
    flag = ((lane == PAD_FLAG_LANE) & (which == 0)).astype(out_ref.dtype)
    out_ref[0, 0, 0, 0:CMP_PAD, :] = flag
    out_ref[0, 0, 0, CMP_PAD:CMP_PAD + nch, 0:HEAD_DIM] = y.astype(out_ref.dtype)
    out_ref[0, 0, 0, CMP_PAD:CMP_PAD + nch, HEAD_DIM:LANES] = jnp.zeros((nch, LANES - HEAD_DIM), out_ref.dtype)


def _compress(xc, w1, w2, pe, kg):
    b_, _, g_, nch, fl = xc.shape
    return pl.pallas_call(
        _compress_kernel,
        grid=(b_, 2, g_),
        in_specs=[
            pl.BlockSpec((1, 1, 1, nch, fl), lambda b, w, g: (b, w, g, 0, 0)),
            pl.BlockSpec((1, 1, 2 * fl, CMP_HIDDEN), lambda b, w, g: (w, g, 0, 0)),
            pl.BlockSpec((1, 1, CMP_HIDDEN, HEAD_DIM), lambda b, w, g: (w, g, 0, 0)),
            pl.BlockSpec((1, 1, 1, 2 * fl), lambda b, w, g: (w, g, 0, 0)),
            pl.BlockSpec((3, HEAD_DIM), lambda b, w, g: (0, 0)),
        ],
        out_specs=pl.BlockSpec((1, 1, 1, CMP_PAD + nch, LANES), lambda b, w, g: (b, w, g, 0, 0)),
        out_shape=jax.ShapeDtypeStruct((b_, 2, g_, CMP_PAD + nch, LANES), F32),
        compiler_params=_params(("parallel", "parallel", "parallel")),
        name="nsa_compress",
    )(xc, w1, w2, pe, kg)


def _nsa_kernel(q_ref, gate_ref, kc_ref, vc_ref, ks_ref, vs_ref, kw_ref, vw_ref,
                tbm_ref, wbm_ref, cbm_ref, qmid_ref, ov_ref, o_ref, qa_ref,
                accs_ref, accw_ref, oc_ref):
    i = jnp.minimum(pl.program_id(2), pl.num_programs(2) - 2)
    qb = Q_BLOCK
    rows = NSA_REP * qb
    ncmp = CMP_PAD

    @pl.when((pl.program_id(0) == 0) & (pl.program_id(1) == 0) & (pl.program_id(2) == 0))
    def _():
        accs_ref[...] = jnp.ones_like(accs_ref)
        accw_ref[...] = jnp.ones_like(accw_ref)
        oc_ref[...] = jnp.zeros_like(oc_ref)

    gt = gate_ref[0]
    for r in range(NSA_REP):
        rs = slice(qb * r, qb * (r + 1))
        o_s = accs_ref[rs, 0:HEAD_DIM] / accs_ref[rs, HEAD_DIM:HEAD_DIM + 1]
        o_w = accw_ref[rs, 0:HEAD_DIM] / accw_ref[rs, HEAD_DIM:HEAD_DIM + 1]
        o = (gt[:, 3 * r + 0:3 * r + 1] * oc_ref[rs, :] + gt[:, 3 * r + 1:3 * r + 2] * o_s
             + gt[:, 3 * r + 2:3 * r + 3] * o_w)
        o_ref[0, :, HEAD_DIM * r:HEAD_DIM * (r + 1)] = o.astype(o_ref.dtype)

    qblk = q_ref[0]
    for r in range(NSA_REP):
        rs = slice(qb * r, qb * (r + 1))
        qa_ref[rs, 0:HEAD_DIM] = qblk[:, HEAD_DIM * r:HEAD_DIM * (r + 1)]
        qa_ref[rs, HEAD_DIM:LANES] = jnp.broadcast_to(
            qmid_ref[0, r][:, 0:LANES - HEAD_DIM], (qb, LANES - HEAD_DIM)).astype(qa_ref.dtype)
    q_lo = qa_ref[:, 0:LANES]

    wstart = pl.multiple_of(KEY_PAD + qb * (i + 1) - WIN_KEYS, qb)
    s_w = _dot_nt(q_lo, kw_ref[0, 0, pl.ds(wstart, WIN_KEYS), :])

    def window_probs(r):
        s = s_w[qb * r:qb * (r + 1)] + wbm_ref[0, r]
        return jnp.exp2(s - jnp.max(s, axis=-1, keepdims=True))

    cstart = pl.multiple_of(8 * i + 8, 8)
    kc = kc_ref[0, 0, 0, pl.ds(cstart, ncmp), :]
    vc = vc_ref[0, 0, 0, pl.ds(cstart, ncmp), :]
    s_c = _dot_nt(q_lo, kc)

    kt = SEL_TILE
    n_tiles = (i * qb + qb + kt - 1) // kt

    def tile_start(dd):
        return pl.multiple_of(KEY_PAD + qb * (i + 1) - kt * (dd + 1), qb)

    near_base = [_dot_nt(q_lo, ks_ref[0, 0, pl.ds(tile_start(dd), kt), 0:LANES])
                 for dd in range(NEAR_TILES)]
    psum = jnp.zeros((qb, ncmp), F32)
    for r in range(NSA_REP):
        s = s_c[qb * r:qb * (r + 1)] + cbm_ref[0, r]
        m = jnp.max(s, axis=-1, keepdims=True)
        p = jnp.exp2(s - m)
        l = jnp.sum(p, axis=-1, keepdims=True)
        p = p * jnp.where(m > 0.5 * NEG, 1.0 / l, 0.0)
        oc_ref[qb * r:qb * (r + 1), :] = _dot(p, vc)[:, 0:HEAD_DIM]
        psum = psum + p
    p_hi = psum.astype(BF16)
    p_lo = (psum - p_hi.astype(F32)).astype(BF16)
    ov = ov_ref[...]
    imp = (jnp.dot(p_hi, ov, preferred_element_type=F32)
           + jnp.dot(p_lo, ov, preferred_element_type=F32))

    nsel = LANES
    sp = lax.broadcasted_iota(jnp.int32, (qb, nsel), 1)
    rr = lax.broadcasted_iota(jnp.int32, (qb, nsel), 0)
    cur = (nsel - 2) + (rr >= SLC_BLOCK).astype(jnp.int32)
    s_abs = sp + (2 * i + 2 - nsel)
    valid = s_abs >= 0
    forced = ((s_abs == 0) | (sp == cur) | (sp == cur - 1)) & valid
    excluded = forced | (sp > cur) | (~valid)
    cand_t = jnp.where(excluded, NEG, imp).T
    v = cand_t
    tau = None
    n_rank = SLC_TOPN - N_FORCED
    p_w = []
    for it in range(n_rank):
        tau = jnp.max(v, axis=0, keepdims=True)
        v = jnp.where(v >= tau, 3.0 * NEG, v)
        if it % (n_rank // NSA_REP) == 0 and len(p_w) < NSA_REP:
            p_w.append(window_probs(len(p_w)))
    accw_ref[...] = _dot(jnp.concatenate(p_w, axis=0), vw_ref[0, 0, pl.ds(wstart, WIN_KEYS), :])
    picked = jnp.where((cand_t >= tau) & (cand_t > 0.5 * NEG), 1.0, 0.0).T
    sel_neg = jnp.where((picked > 0.5) | forced, 0.0, NEG)
    sel_neg = pltpu.roll(sel_neg, shift=(2 * i + 2) % nsel, axis=1).astype(qa_ref.dtype)
    for r in range(NSA_REP):
        qa_ref[qb * r:qb * (r + 1), LANES:2 * LANES] = sel_neg
    q_ext = qa_ref[...]

    def tile_scores(dd):
        return _dot_nt(q_ext, ks_ref[0, 0, pl.ds(tile_start(dd), kt), :])

    def tile_softmax(s, dd):
        m_t = jnp.max(s, axis=-1, keepdims=True)
        return m_t, _dot(jnp.exp2(s - m_t), vs_ref[0, 0, pl.ds(tile_start(dd), kt), :])

    def merge(parts):
        m_new = functools.reduce(jnp.maximum, [m for m, _ in parts])
        acc = sum(jnp.exp2(m - m_new) * pv for m, pv in parts)
        return m_new, acc

    near = []
    for dd in range(NEAR_TILES):
        msk = _dot_nt(sel_neg, ks_ref[0, 0, pl.ds(tile_start(dd), kt), LANES:2 * LANES])
        near.append(jnp.concatenate([near_base[dd][qb * r:qb * (r + 1)] + (tbm_ref[0, r, dd] + msk)
                                     for r in range(NSA_REP)], axis=0))
    carry = merge([tile_softmax(s, dd) for dd, s in enumerate(near)])

    def far_group(first, width, carry):
        dds = [jnp.minimum(first + k, n_tiles) for k in range(width)]
        scores = [tile_scores(dd) for dd in dds]
        return merge([carry] + [tile_softmax(s, dd) for s, dd in zip(scores, dds)])

    n_far = jnp.maximum(n_tiles - NEAR_TILES, 0)
    n_groups = n_far // FAR_GROUP
    carry = lax.fori_loop(
        0, n_groups, lambda u, c: far_group(NEAR_TILES + FAR_GROUP * u, FAR_GROUP, c), carry)
    rest = NEAR_TILES + FAR_GROUP * n_groups
    carry = lax.fori_loop(
        0, (n_far - FAR_GROUP * n_groups + 1) // 2, lambda u, c: far_group(rest + 2 * u, 2, c), carry)
    accs_ref[...] = carry[1]


def _nsa_attention(qn, gate, kcv, ksa, vsp, kwa, vwp, tbm, wbm, cbm, qmid, ov):
    b_, t_, _ = qn.shape
    tp = ksa.shape[2]
    ncp = kcv.shape[3]
    gw = NSA_REP * HEAD_DIM
    assert t_ // SLC_BLOCK <= LANES
    kvspec = lambda wd: pl.BlockSpec((1, 1, tp, wd), lambda b, g, i: (b, g, 0, 0))
    nq = t_ // Q_BLOCK
    behind = lambda b, g, i: (b, jnp.maximum(i - 1, 0), g)
    return pl.pallas_call(
        _nsa_kernel,
        grid=(b_, NSA_KV, nq + 1),
        in_specs=[
            pl.BlockSpec((1, Q_BLOCK, gw), lambda b, g, i: (b, jnp.minimum(i, nq - 1), g)),
            pl.BlockSpec((1, Q_BLOCK, LANES), behind),
            pl.BlockSpec((1, 1, 1, ncp, LANES), lambda b, g, i: (b, 0, g, 0, 0)),
            pl.BlockSpec((1, 1, 1, ncp, LANES), lambda b, g, i: (b, 1, g, 0, 0)),
            kvspec(2 * LANES), kvspec(LANES), kvspec(LANES), kvspec(LANES),
            pl.BlockSpec((1, NSA_REP, NEAR_TILES, Q_BLOCK, SEL_TILE), lambda b, g, i: (g, 0, 0, 0, 0)),
            pl.BlockSpec((1, NSA_REP, Q_BLOCK, WIN_KEYS), lambda b, g, i: (g, 0, 0, 0)),
            pl.BlockSpec((1, NSA_REP, Q_BLOCK, CMP_PAD), lambda b, g, i: (g, 0, 0, 0)),
            pl.BlockSpec((1, NSA_REP, 1, LANES), lambda b, g, i: (g, 0, 0, 0)),
            pl.BlockSpec((CMP_PAD, LANES), lambda b, g, i: (0, 0)),
        ],
        out_specs=pl.BlockSpec((1, Q_BLOCK, gw), behind),
        out_shape=jax.ShapeDtypeStruct((b_, t_, D_NSA), F32),
        scratch_shapes=[
            pltpu.VMEM((NSA_REP * Q_BLOCK, 2 * LANES), MXU_DTYPE),
            pltpu.VMEM((NSA_REP * Q_BLOCK, LANES), F32),
            pltpu.VMEM((NSA_REP * Q_BLOCK, LANES), F32),
            pltpu.VMEM((NSA_REP * Q_BLOCK, HEAD_DIM), F32),
        ],
        compiler_params=_params(("parallel", "parallel", "arbitrary")),
        name="nsa_attention",
    )(qn, gate, kcv, kcv, ksa, vsp, kwa, vwp, tbm, wbm, cbm, qmid, ov)


def _mix_kernel(x_ref, on_ref, gu_ref, vn_ref, sgw_ref, sgb_ref, gn_ref, gs_ref, wo_ref,
                fg_ref, wq_ref, x1_out, h2t_out, qp_out):
    tm = x_ref.shape[0]
    tri = (lax.broadcasted_iota(jnp.int32, (SG_CHUNK, SG_CHUNK), 0)
           >= lax.broadcasted_iota(jnp.int32, (SG_CHUNK, SG_CHUNK), 1))
    sgb = sgb_ref[...]
    y = x_ref[...] + _dot(_rms(on_ref[...], gn_ref[...]), wo_ref[0:D_NSA, :])
    parts = []
    for c in range(tm // SG_CHUNK):
        cs = slice(SG_CHUNK * c, SG_CHUNK * (c + 1))
        heads = []
        for hh in range(SG_HEADS):
            sl = slice(SG_DIM * hh, SG_DIM * (hh + 1))
            w = jnp.where(tri, sgw_ref[hh], 0.0)
            mixed = _dot(w, vn_ref[cs, sl]) + sgb[:, hh:hh + 1]
            heads.append(gu_ref[cs, sl].astype(F32) * mixed)
        parts.append(jnp.concatenate(heads, axis=1))
    o_sg = jnp.concatenate(parts, axis=0)
    y = y + _dot(_rms(o_sg, gs_ref[...]), wo_ref[D_NSA:D_NSA + D_SG, :])
    x1_out[...] = y
    h2 = _rms(y, fg_ref[...])
    h2t_out[...] = h2.T.astype(h2t_out.dtype)
    qp_out[...] = _dot(h2, wq_ref[...])


def _mix(x2, onsa, gu, vn, sgw, sgb_t, gn, gs, wo, fg, wq, tm):
    n_, d_ = x2.shape
    row = lambda wd: pl.BlockSpec((tm, wd), lambda i: (i, 0))
    const = lambda shape, **kw: pl.BlockSpec(shape, lambda i: (0,) * len(shape), **kw)
    return pl.pallas_call(
        _mix_kernel,
        grid=(n_ // tm,),
        in_specs=[
            row(d_), row(D_NSA), row(D_SG), row(D_SG),
            const((SG_HEADS, SG_CHUNK, SG_CHUNK)), const((SG_CHUNK, SG_HEADS)),
            const((1, D_NSA)), const((1, D_SG)),
            const((D_NSA + D_SG, d_), pipeline_mode=pl.Buffered(1)),
            const((1, d_)),
            const((d_, wq.shape[1]), pipeline_mode=pl.Buffered(1)),
        ],
        out_specs=[row(d_), pl.BlockSpec((d_, tm), lambda i: (0, i)), row(wq.shape[1])],
        out_shape=[
            jax.ShapeDtypeStruct((n_, d_), F32),
            jax.ShapeDtypeStruct((d_, n_), MXU_DTYPE),
            jax.ShapeDtypeStruct((n_, wq.shape[1]), F32),
        ],
        compiler_params=_params(("parallel",)),
        name="mix_out_proj",
    )(x2, onsa, gu, vn, sgw, sgb_t, gn, gs, wo, fg, wq)


def _top_rows(v, k):
    tops = []
    for _ in range(k):
        m = jnp.max(v, axis=0, keepdims=True)
        tops.append(m)
        v = jnp.where(v >= m, 0.0, v)
    return jnp.concatenate(tops, axis=0)


def _pair_products(a, b):
    cand = [a[0:1] * b]
    for ra in range(1, 8):
        cand.append(a[ra:ra + 1] * b[0:8])
    cand.append(a[8:16] * b[0:1])
    return jnp.concatenate(cand, axis=0)


def _peer_topk_kernel(qp_ref, sk_ref, e1_out, e2_out, eps_out):
    for hh in range(PEER_HEADS):
        base = 2 * PEER_HALF * hh
        s1 = _dot_nt(sk_ref[0], qp_ref[:, base:base + PEER_HALF])
        s2 = _dot_nt(sk_ref[1], qp_ref[:, base + PEER_HALF:base + 2 * PEER_HALF])
        e1 = jnp.exp(s1 - jnp.max(s1, axis=0, keepdims=True))
        e2 = jnp.exp(s2 - jnp.max(s2, axis=0, keepdims=True))
        a = _top_rows(e1, PEER_TOPK)
        b = _top_rows(e2, PEER_TOPK)
        z = jnp.sum(_top_rows(_pair_products(a, b), PEER_TOPK), axis=0, keepdims=True)
        zinv = 1.0 / z
        gates = _top_rows(_pair_products(a * zinv, b), PEER_TOPK)
        e1_out[hh] = e1 * zinv
        for j in range(e2.shape[1] // LANES):
            e2_out[hh, j] = e2[:, LANES * j:LANES * (j + 1)]
        eps_out[hh:hh + 1, :] = gates[PEER_TOPK - 1:PEER_TOPK]


def _peer_topk(qp, sub_keys, tm):
    n_, qd = qp.shape
    return pl.pallas_call(
        _peer_topk_kernel,
        grid=(n_ // tm,),
        in_specs=[
            pl.BlockSpec((tm, qd), lambda i: (i, 0)),
            pl.BlockSpec((2, PEER_NKEYS, PEER_HALF), lambda i: (0, 0, 0)),
        ],
        out_specs=[
            pl.BlockSpec((PEER_HEADS, PEER_NKEYS, tm), lambda i: (0, 0, i)),
            pl.BlockSpec((PEER_HEADS, tm // LANES, PEER_NKEYS, LANES), lambda i: (0, i, 0, 0)),
            pl.BlockSpec((PEER_HEADS, tm), lambda i: (0, i)),
        ],
        out_shape=[
            jax.ShapeDtypeStruct((PEER_HEADS, PEER_NKEYS, n_), F32),
            jax.ShapeDtypeStruct((PEER_HEADS, n_ // LANES, PEER_NKEYS, LANES), F32),
            jax.ShapeDtypeStruct((PEER_HEADS, n_), F32),
        ],
        compiler_params=_params(("parallel",)),
        name="peer_topk",
    )(qp, sub_keys)


def _peer_mix_kernel(n_tiles, ht_ref, u_ref, vt_ref, e1_ref, e2_ref, eps_ref, x1_ref, out_ref,
                     acc_ref, crow_ref, a0_ref, a1_ref, z0_ref, z1_ref):
    ie = pl.program_id(1)
    te = u_ref.shape[0]
    tm = ht_ref.shape[1]
    per_tile = te // PEER_NKEYS

    @pl.when(ie == 0)
    def _():
        acc_ref[...] = jnp.zeros_like(acc_ref)

    d_ = vt_ref.shape[1]
    n_lane = tm // LANES
    n_blocks = per_tile * n_lane
    a_units = [(r, c, kc) for r in range(te // MXU_ROWS) for c in range(tm // MXU_ROWS)
               for kc in range(d_ // A_KCHUNK)]
    c_units = [(r, c) for r in range(d_ // MXU_ROWS) for c in range(tm // MXU_ROWS)]

    def step(a_w, a_r, z_w, z_r):
        tile = ie - 1

        def a_unit(r, c, kc):
            rs = slice(MXU_ROWS * r, MXU_ROWS * (r + 1))
            cs = slice(MXU_ROWS * c, MXU_ROWS * (c + 1))
            ds = slice(A_KCHUNK * kc, A_KCHUNK * (kc + 1))
            res = jnp.dot(u_ref[rs, ds], ht_ref[ds, cs], preferred_element_type=F32)
            for jj in range(MXU_ROWS // LANES):
                j = c * (MXU_ROWS // LANES) + jj
                part = res[:, LANES * jj:LANES * (jj + 1)]
                if kc == 0:
                    a_w[j, rs, :] = part
                else:
                    a_w[j, rs, :] += part

        def c_unit(r, c):
            rs = slice(MXU_ROWS * r, MXU_ROWS * (r + 1))
            cs = slice(MXU_ROWS * c, MXU_ROWS * (c + 1))
            res = jnp.dot(vt_ref[0, rs, :], z_r[:, cs], preferred_element_type=F32)
            for jj in range(MXU_ROWS // LANES):
                acc_ref[c * (MXU_ROWS // LANES) + jj, rs, :] += res[:, LANES * jj:LANES * (jj + 1)]

        def b_block(k, j):
            ls = slice(LANES * j, LANES * (j + 1))
            for part in range(PEER_NKEYS // GATE_ROWS):
                ks = slice(GATE_ROWS * part, GATE_ROWS * (part + 1))
                rs = slice(PEER_NKEYS * k + GATE_ROWS * part, PEER_NKEYS * k + GATE_ROWS * (part + 1))
                g = jnp.zeros((GATE_ROWS, LANES), F32)
                for hh in range(PEER_HEADS):
                    row = PEER_HEADS * k + hh
                    gate = e2_ref[hh, j, ks, :] * crow_ref[row:row + 1, ls]
                    g = g + jnp.where(gate >= eps_ref[hh:hh + 1, ls], gate, 0.0)
                z_w[rs, ls] = (_gelu(a_r[j, rs, :]) * g).astype(z_w.dtype)

        do_a, do_b, do_c = a_w is not None, z_w is not None, z_r is not None
        if do_b:
            for k in range(per_tile):
                i1 = tile * per_tile + k
                for hh in range(PEER_HEADS):
                    row = PEER_HEADS * k + hh
                    crow_ref[row:row + 1, :] = e1_ref[hh, pl.ds(i1, 1), :]

        a_iter = iter(a_units if do_a else [])
        c_iter = iter(c_units if do_c else [])
        a_every = max(n_blocks // len(a_units), 1)
        c_per = -(-len(c_units) // n_blocks)
        for blk in range(n_blocks):
            k, j = divmod(blk, n_lane)
            if blk % a_every == 0:
                unit = next(a_iter, None)
                if unit is not None:
                    a_unit(*unit)
            for _ in range(c_per):
                unit = next(c_iter, None)
                if unit is not None:
                    c_unit(*unit)
            if do_b:
                b_block(k, j)
        for unit in a_iter:
            a_unit(*unit)
        for unit in c_iter:
            c_unit(*unit)

    a_bufs, z_bufs = (a0_ref, a1_ref), (z0_ref, z1_ref)

    def run(par, do_a=True, do_b=True, do_c=True):
        step(a_bufs[par] if do_a else None, a_bufs[1 - par], z_bufs[1 - par] if do_b else None,
             z_bufs[par] if do_c else None)

    last = n_tiles + 1
    pl.when(ie == 0)(lambda: run(0, do_b=False, do_c=False))
    pl.when(ie == 1)(lambda: run(1, do_c=False))
    pl.when((ie >= 2) & (ie < n_tiles) & (ie % 2 == 0))(lambda: run(0))
    pl.when((ie >= 2) & (ie < n_tiles) & (ie % 2 == 1))(lambda: run(1))
    pl.when(ie == n_tiles)(lambda: run(n_tiles % 2, do_a=False))
    pl.when(ie == last)(lambda: run(last % 2, do_a=False, do_b=False))

    @pl.when(ie == last)
    def _():
        for j in range(n_lane):
            ts = slice(LANES * j, LANES * (j + 1))
            out_ref[ts, :] = x1_ref[ts, :] + acc_ref[j].T


def _peer_mix(h2t, u, vt, e1t, e2t, eps, x1, tm, te):
    d_, n_ = h2t.shape
    n_tiles = u.shape[0] // te
    assert n_tiles >= 3
    return pl.pallas_call(
        functools.partial(_peer_mix_kernel, n_tiles),
        grid=(n_ // tm, n_tiles + 2),
        in_specs=[
            pl.BlockSpec((d_, tm), lambda it, ie: (0, it)),
            pl.BlockSpec((te, d_), lambda it, ie: (jnp.minimum(ie, n_tiles - 1), 0)),
            pl.BlockSpec((1, d_, te), lambda it, ie: (jnp.maximum(ie - 2, 0), 0, 0)),
            pl.BlockSpec((PEER_HEADS, PEER_NKEYS, tm), lambda it, ie: (0, 0, it)),
            pl.BlockSpec((PEER_HEADS, tm // LANES, PEER_NKEYS, LANES), lambda it, ie: (0, it, 0, 0)),
            pl.BlockSpec((PEER_HEADS, tm), lambda it, ie: (0, it)),
            pl.BlockSpec((tm, d_), lambda it, ie: (it, 0)),
        ],
        out_specs=pl.BlockSpec((tm, d_), lambda it, ie: (it, 0)),
        out_shape=jax.ShapeDtypeStruct((n_, d_), F32),
        scratch_shapes=[
            pltpu.VMEM((tm // LANES, d_, LANES), F32),
            pltpu.VMEM((PEER_HEADS * (te // PEER_NKEYS), tm), F32),
            pltpu.VMEM((tm // LANES, te, LANES), F32), pltpu.VMEM((tm // LANES, te, LANES), F32),
            pltpu.VMEM((te, tm), MXU_DTYPE), pltpu.VMEM((te, tm), MXU_DTYPE),
        ],
        compiler_params=_params(("parallel", "arbitrary")),
        name="peer_mix",
    )(h2t, u, vt, e1t, e2t, eps, x1)


def _rel_bucket(dist):
    n = jnp.maximum(dist, 0)
    max_exact = REL_BUCKETS // 2
    nf = jnp.maximum(n, 1).astype(F32)
    large = max_exact + (jnp.log(nf / max_exact) / math.log(REL_MAX_DIST / max_exact)
                         * (REL_BUCKETS - max_exact)).astype(jnp.int32)
    large = jnp.minimum(large, REL_BUCKETS - 1)
    return jnp.where(n < max_exact, n, large)


def _toeplitz(btab, off, rows, cols):
    per = rows + cols
    k = np.concatenate([np.arange(cols), np.zeros(1, np.int64), np.arange(-(rows - 1), 0)])
    idx = np.clip(off - k, 0, btab.shape[1] - 1)
    w = jnp.take(btab, jnp.asarray(idx, jnp.int32), axis=1)
    x = jnp.tile(w, (1, rows))[:, :rows * (per - 1)]
    return x.reshape(btab.shape[0], rows, per - 1)[:, :, :cols]


def _bias_tiles(rel_bias):
    dist = jnp.arange(BIAS_TABLE, dtype=jnp.int32)
    rel_bias = rel_bias.astype(F32) * LOG2E
    btab = rel_bias[_rel_bucket(dist)].T
    tbw = jnp.stack([_toeplitz(btab, SEL_TILE * (dd + 1) - Q_BLOCK, Q_BLOCK, SEL_TILE)
                     for dd in range(NEAR_TILES)], axis=1)
    wb = _toeplitz(btab, WINDOW, Q_BLOCK, WIN_KEYS)
    off = CMP_STRIDE * (LANES - 8) - (CMP_BLOCK - 1)
    front = CMP_STRIDE * LANES - off
    length = CMP_STRIDE * (LANES + 8 + 2)
    padded = jnp.pad(btab, ((0, 0), (front, length - front - BIAS_TABLE)))
    ch = padded.reshape(btab.shape[0], length // CMP_STRIDE, CMP_STRIDE)
    nwin = LANES + 1
    win = jnp.stack([ch[:, a:a + nwin] for a in range(Q_BLOCK // CMP_STRIDE)], axis=2)
    win = win.reshape(btab.shape[0], nwin, Q_BLOCK)
    cb = jnp.swapaxes(win[:, 1:nwin][:, ::-1], 1, 2)
    cfar = rel_bias.astype(F32)[REL_BUCKETS - 1]
    grp = lambda a: a.reshape((NSA_KV, NSA_REP) + a.shape[1:])

    r = np.arange(Q_BLOCK)[:, None]
    near_ok = np.stack([r - np.arange(SEL_TILE)[None, :] + SEL_TILE * (dd + 1) - Q_BLOCK >= 0
                        for dd in range(NEAR_TILES)])
    tbm = jnp.where(near_ok, tbw - cfar[:, None, None, None], NEG)
    dist_w = r - np.arange(WIN_KEYS)[None, :] + WINDOW
    wbm = jnp.where((dist_w >= 0) & (dist_w < WINDOW), wb, NEG)
    dist_c = r - CMP_STRIDE * np.arange(CMP_PAD)[None, :] + (CMP_STRIDE * CMP_PAD - Q_BLOCK - (CMP_BLOCK - 1))
    cbias = jnp.concatenate(
        [jnp.broadcast_to(cfar[:, None, None], (NSA_HEADS, Q_BLOCK, CMP_PAD - LANES)), cb], axis=2)
    cbm = jnp.where(dist_c >= 0, cbias, NEG)

    hi = cfar.astype(BF16).astype(F32)
    lo = (cfar - hi).astype(BF16).astype(F32)
    qmid = jnp.zeros((NSA_HEADS, 1, LANES), F32)
    qmid = qmid.at[:, 0, FAR_HI_LANE - HEAD_DIM].set(hi).at[:, 0, FAR_LO_LANE - HEAD_DIM].set(lo)
    qmid = qmid.at[:, 0, PAD_FLAG_LANE - HEAD_DIM].set(NEG)
    return grp(tbm), grp(wbm), grp(cbm), grp(qmid)


def _extend_keys(ks, kw):
    tp = ks.shape[2] + KEY_PAD
    pos = np.arange(tp) - KEY_PAD
    real = pos >= 0
    mid_s = np.zeros((tp, LANES - HEAD_DIM), np.float32)
    mid_s[:, FAR_HI_LANE - HEAD_DIM] = real
    mid_s[:, FAR_LO_LANE - HEAD_DIM] = real
    mid_s[:, PAD_FLAG_LANE - HEAD_DIM] = ~real
    onehot = ((pos[:, None] // SLC_BLOCK) == np.arange(LANES)[None, :]) & real[:, None]
    mid_w = np.zeros((tp, LANES - HEAD_DIM), np.float32)
    mid_w[:, PAD_FLAG_LANE - HEAD_DIM] = ~real
    padk = lambda a: jnp.pad(a, ((0, 0), (0, 0), (KEY_PAD, 0), (0, 0)))
    bc = lambda a: jnp.broadcast_to(jnp.asarray(a, ks.dtype), ks.shape[:2] + a.shape)
    ksa = jnp.concatenate([padk(ks), bc(mid_s), bc(onehot.astype(np.float32))], axis=-1)
    kwa = jnp.concatenate([padk(kw), bc(mid_w)], axis=-1)
    return ksa, kwa


def _overlap_matrix():
    c = np.arange(CMP_PAD)[:, None]
    s = np.arange(LANES)[None, :]
    per = SLC_BLOCK // CMP_STRIDE
    ov = (s == c // per) | ((c % per == per - 1) & (s == c // per + 1))
    return jnp.asarray(ov.astype(np.float32), BF16)


def kernel(x, attn_norm_g, w_in, q_norm_g, k_norm_g, cmp_pe_k, cmp_w1_k, cmp_w2_k, cmp_pe_v,
           cmp_w1_v, cmp_w2_v, rel_bias, sg_ln_g, sg_ln_b, sg_w, sg_b, out_norm_nsa, out_norm_sg,
           w_out, ffn_norm_g, peer_w_query, peer_sub_keys, peer_u, peer_v):
    b_, t_, d_ = x.shape
    n_ = b_ * t_
    assert t_ % Q_BLOCK == 0 and t_ >= WIN_KEYS and d_ == D_NSA + D_SG

    c_kv = D_NSA + 6 * D_KV
    c_gate = c_kv + 3 * NSA_HEADS
    wg = w_in[:, c_kv:c_gate].reshape(d_, NSA_KV, 3 * NSA_REP)
    wg = jnp.pad(wg, ((0, 0), (0, 0), (0, LANES - 3 * NSA_REP))).reshape(d_, NSA_KV * LANES)
    w_attn = jnp.concatenate([w_in[:, :c_kv], wg], axis=1).astype(MXU_DTYPE)
    w_sgu = w_in[:, c_gate:].astype(MXU_DTYPE)
    ng = attn_norm_g.reshape(1, d_)

    gq = jnp.tile(q_norm_g, NSA_HEADS).reshape(1, D_NSA) * (HEAD_DIM ** -0.5 * LOG2E)
    no_gain = jnp.zeros((D_KV,), F32)
    gkv = jnp.concatenate([jnp.tile(k_norm_g[1], NSA_KV), no_gain,
                           jnp.tile(k_norm_g[2], NSA_KV), no_gain]).reshape(1, 4 * D_KV)
    qn, kcv, kv, gate = _attn_proj(x, ng, w_attn, gq, gkv, tm=256)
    heads = lambda a: a.reshape(b_, t_, NSA_KV, HEAD_DIM).transpose(0, 2, 1, 3)
    ones_col = jnp.zeros((b_, NSA_KV, t_, HEAD_DIM), kv.dtype).at[..., 0].set(1)
    ks, kw = heads(kv[..., 0:D_KV]), heads(kv[..., 2 * D_KV:3 * D_KV])
    vs = jnp.concatenate([heads(kv[..., D_KV:2 * D_KV]), ones_col], axis=-1)
    vw = jnp.concatenate([heads(kv[..., 3 * D_KV:4 * D_KV]), ones_col], axis=-1)
    gu, vn = _sgu_proj(x, ng, w_sgu, sg_ln_g, sg_ln_b, tm=256)

    nch = t_ // CMP_STRIDE
    xc = kcv.reshape(b_, nch, CMP_STRIDE, 2, NSA_KV, HEAD_DIM).transpose(0, 3, 4, 1, 2, 5)
    xc = xc.reshape(b_, 2, NSA_KV, nch, CMP_STRIDE * HEAD_DIM)
    w1 = jnp.stack([cmp_w1_k, cmp_w1_v]).astype(MXU_DTYPE)
    w2 = jnp.stack([cmp_w2_k, cmp_w2_v]).astype(MXU_DTYPE)
    pe = jnp.stack([cmp_pe_k, cmp_pe_v]).transpose(0, 2, 1, 3).reshape(2, NSA_KV, 1, CMP_BLOCK * HEAD_DIM)
    kcv_c = _compress(xc, w1, w2, pe.astype(MXU_DTYPE), k_norm_g)

    padk = lambda a: jnp.pad(a, ((0, 0), (0, 0), (KEY_PAD, 0), (0, 0)))
    tbm, wbm, cbm, qmid = _bias_tiles(rel_bias)
    ksa, kwa = _extend_keys(ks, kw)
    onsa = _nsa_attention(qn, gate, kcv_c, ksa, padk(vs), kwa, padk(vw),
                          tbm, wbm, cbm, qmid, _overlap_matrix())

    x1, h2t, qp = _mix(x.reshape(n_, d_), onsa.reshape(n_, D_NSA), gu.reshape(n_, D_SG),
                      vn.reshape(n_, D_SG), sg_w, sg_b.T, out_norm_nsa.reshape(1, D_NSA),
                      out_norm_sg.reshape(1, D_SG), w_out.astype(MXU_DTYPE),
                      ffn_norm_g.reshape(1, d_), peer_w_query.astype(MXU_DTYPE), tm=256)

    e1t, e2t, eps = _peer_topk(qp, peer_sub_keys, tm=256)
    te = 512
    vt = peer_v.reshape(-1, te, d_).transpose(0, 2, 1).astype(MXU_DTYPE)
    out = _peer_mix(h2t, peer_u.astype(MXU_DTYPE), vt, e1t, e2t, eps, x1, tm=min(512, n_), te=te)
    return out.reshape(b_, t_, d_)
```

```python
import functools
import math

import jax
import jax.numpy as jnp
import numpy as np
from jax import lax
from jax.experimental import pallas as pl
from jax.experimental.pallas import tpu as pltpu

F32 = jnp.float32
BF16 = jnp.bfloat16
MXU_DTYPE = BF16

EPS = 1e-6
NEG = -1e30
SQRT_HALF = 0.7071067811865476
LOG2E = 1.4426950408889634

NSA_HEADS = 16
NSA_KV = 4
NSA_REP = 4
HEAD_DIM = 64
D_NSA = NSA_HEADS * HEAD_DIM
D_KV = NSA_KV * HEAD_DIM
CMP_STRIDE = 16
CMP_BLOCK = 32
CMP_HIDDEN = 128
SLC_BLOCK = 64
SLC_TOPN = 16
WINDOW = 512
Q_BLOCK = 128
SG_HEADS = 8
SG_DIM = 128
SG_CHUNK = 128
D_SG = SG_HEADS * SG_DIM
REL_BUCKETS = 32
REL_MAX_DIST = 1024
PEER_HEADS = 8
PEER_NKEYS = 128
PEER_HALF = 128
PEER_TOPK = 16

LANES = 128
MXU_ROWS = 256
A_KCHUNK = 1024
GATE_ROWS = 64
KEY_PAD = 1024
CMP_PAD = 512
SEL_TILE = 512
WIN_KEYS = WINDOW + Q_BLOCK
NEAR_TILES = 2
FAR_GROUPS = (8, 4, 2)
BIAS_TABLE = 2048
FAR_DIST = 897
FAR_HI_LANE = 64
FAR_LO_LANE = 65
PAD_FLAG_LANE = 66
N_FORCED = 3
VMEM_LIMIT = 56 * 1024 * 1024


def _gelu(x):
    return 0.5 * x * (1.0 + lax.erf(x * SQRT_HALF))


def _dot(a, b):
    return jnp.dot(a.astype(MXU_DTYPE), b.astype(MXU_DTYPE), preferred_element_type=F32)


def _dot_nt(a, b):
    return lax.dot_general(a.astype(MXU_DTYPE), b.astype(MXU_DTYPE),
                           (((1,), (1,)), ((), ())), preferred_element_type=F32)


def _rms(x, g):
    ms = jnp.mean(x * x, axis=-1, keepdims=True)
    return x * lax.rsqrt(ms + EPS) * g


def _params(sem):
    return pltpu.CompilerParams(dimension_semantics=sem, vmem_limit_bytes=VMEM_LIMIT)


def _attn_proj_kernel(x_ref, ng_ref, w_ref, gq_ref, gkv_ref, hs_ref, he_ref,
                      q_out, kcv_out, kv_out, gate_out):
    h = _rms(x_ref[0], ng_ref[...]).astype(MXU_DTYPE)

    def head_rms(p):
        ms = jnp.dot((p * p).astype(MXU_DTYPE), hs_ref[...], preferred_element_type=F32)
        r = lax.rsqrt(ms + EPS)
        r_hi = r.astype(BF16)
        r_lo = (r - r_hi.astype(F32)).astype(BF16)
        he = he_ref[...]
        return p * (jnp.dot(r_hi, he, preferred_element_type=F32)
                    + jnp.dot(r_lo, he, preferred_element_type=F32))

    pq = _dot(h, w_ref[:, 0:D_NSA])
    q_out[0] = (head_rms(pq) * gq_ref[...]).astype(q_out.dtype)

    kcv_out[0] = _dot(h, w_ref[:, D_NSA:D_NSA + 2 * D_KV]).astype(kcv_out.dtype)

    base = D_NSA + 2 * D_KV
    p4 = _dot(h, w_ref[:, base:base + 4 * D_KV])
    col = lax.broadcasted_iota(jnp.int32, (1, 4 * D_KV), 1)
    is_key = (col // D_KV) % 2 == 0
    kv_out[0] = jnp.where(is_key, head_rms(p4) * gkv_ref[...], p4).astype(kv_out.dtype)

    base = base + 4 * D_KV
    gate_out[0] = jax.nn.sigmoid(_dot(h, w_ref[:, base:base + NSA_KV * LANES]))


def _attn_proj(x, ng, w, gq, gkv, tm):
    b_, t_, d_ = x.shape
    nw = w.shape[1]
    heads = np.arange(D_NSA) // HEAD_DIM
    hs = (heads[:, None] == np.arange(LANES)[None, :]).astype(np.float32)
    const = lambda shape: pl.BlockSpec(shape, lambda b, i: (0,) * len(shape))
    return pl.pallas_call(
        _attn_proj_kernel,
        grid=(b_, t_ // tm),
        in_specs=[
            pl.BlockSpec((1, tm, d_), lambda b, i: (b, i, 0)),
            const((1, d_)),
            pl.BlockSpec((d_, nw), lambda b, i: (0, 0), pipeline_mode=pl.Buffered(1)),
            const((1, D_NSA)), const((1, 4 * D_KV)),
            const((D_NSA, LANES)), const((LANES, D_NSA)),
        ],
        out_specs=[
            pl.BlockSpec((1, tm, D_NSA), lambda b, i: (b, i, 0)),
            pl.BlockSpec((1, tm, 2 * D_KV), lambda b, i: (b, i, 0)),
            pl.BlockSpec((1, tm, 4 * D_KV), lambda b, i: (b, i, 0)),
            pl.BlockSpec((1, tm, NSA_KV * LANES), lambda b, i: (b, i, 0)),
        ],
        out_shape=[
            jax.ShapeDtypeStruct((b_, t_, D_NSA), MXU_DTYPE),
            jax.ShapeDtypeStruct((b_, t_, 2 * D_KV), MXU_DTYPE),
            jax.ShapeDtypeStruct((b_, t_, 4 * D_KV), MXU_DTYPE),
            jax.ShapeDtypeStruct((b_, t_, NSA_KV * LANES), F32),
        ],
        compiler_params=_params(("parallel", "parallel")),
        name="attn_proj",
    )(x, ng, w, gq, gkv, jnp.asarray(hs / HEAD_DIM, MXU_DTYPE), jnp.asarray(hs.T, BF16))


def _sgu_proj_kernel(x_ref, ng_ref, w_ref, lng_ref, lnb_ref, gu_out, vn_out):
    h = _rms(x_ref[0], ng_ref[...]).astype(MXU_DTYPE)
    gu_out[0] = _gelu(_dot(h, w_ref[:, 0:D_SG])).astype(gu_out.dtype)
    gv = _gelu(_dot(h, w_ref[:, D_SG:2 * D_SG]))
    for hh in range(SG_HEADS):
        sl = slice(SG_DIM * hh, SG_DIM * (hh + 1))
        v = gv[:, sl]
        mu = jnp.mean(v, axis=-1, keepdims=True)
        var = jnp.mean(jnp.square(v - mu), axis=-1, keepdims=True)
        y = (v - mu) * lax.rsqrt(var + EPS) * lng_ref[hh:hh + 1, :] + lnb_ref[hh:hh + 1, :]
        vn_out[0, :, sl] = y.astype(vn_out.dtype)


def _sgu_proj(x, ng, w, lng, lnb, tm):
    b_, t_, d_ = x.shape
    return pl.pallas_call(
        _sgu_proj_kernel,
        grid=(b_, t_ // tm),
        in_specs=[
            pl.BlockSpec((1, tm, d_), lambda b, i: (b, i, 0)),
            pl.BlockSpec((1, d_), lambda b, i: (0, 0)),
            pl.BlockSpec((d_, 2 * D_SG), lambda b, i: (0, 0), pipeline_mode=pl.Buffered(1)),
            pl.BlockSpec((SG_HEADS, SG_DIM), lambda b, i: (0, 0)),
            pl.BlockSpec((SG_HEADS, SG_DIM), lambda b, i: (0, 0)),
        ],
        out_specs=[
            pl.BlockSpec((1, tm, D_SG), lambda b, i: (b, i, 0)),
            pl.BlockSpec((1, tm, D_SG), lambda b, i: (b, i, 0)),
        ],
        out_shape=[
            jax.ShapeDtypeStruct((b_, t_, D_SG), MXU_DTYPE),
            jax.ShapeDtypeStruct((b_, t_, D_SG), MXU_DTYPE),
        ],
        compiler_params=_params(("parallel", "parallel")),
        name="sgu_proj",
    )(x, ng, w, lng, lnb)


def _compress_kernel(x_ref, w1_ref, w2_ref, pe_ref, kg_ref, out_ref):
    which = pl.program_id(1)
    x = x_ref[0, 0, 0]
    nch = x.shape[0]
    half = CMP_STRIDE * HEAD_DIM
    a = _dot(x, w1_ref[0, 0, 0:half, :])
    bm = _dot(x, w1_ref[0, 0, half:2 * half, :])
    pec = _dot(pe_ref[0, 0], w1_ref[0, 0])
    hid = _gelu(a + pltpu.roll(bm, shift=nch - 1, axis=0) + pec)
    y = _dot(hid, w2_ref[0, 0])
    y = jnp.where(which == 0, _rms(y, kg_ref[0:1, :]), y)
    lane = lax.broadcasted_iota(jnp.int32, (CMP_PAD, LANES), 1)
    flag = ((lane == PAD_FLAG_LANE) & (which == 0)).astype(out_ref.dtype)
    out_ref[0, 0, 0, 0:CMP_PAD, :] = flag
    out_ref[0, 0, 0, CMP_PAD:CMP_PAD + nch, 0:HEAD_DIM] = y.astype(out_ref.dtype)
    out_ref[0, 0, 0, CMP_PAD:CMP_PAD + nch, HEAD_DIM:LANES] = jnp.zeros((nch, LANES - HEAD_DIM), out_ref.dtype)


def _compress(xc, w1, w2, pe, kg):
    b_, _, g_, nch, fl = xc.shape
    return pl.pallas_call(
        _compress_kernel,
        grid=(b_, 2, g_),
        in_specs=[
            pl.BlockSpec((1, 1, 1, nch, fl), lambda b, w, g: (b, w, g, 0, 0)),
            pl.BlockSpec((1, 1, 2 * fl, CMP_HIDDEN), lambda b, w, g: (w, g, 0, 0)),
            pl.BlockSpec((1, 1, CMP_HIDDEN, HEAD_DIM), lambda b, w, g: (w, g, 0, 0)),
            pl.BlockSpec((1, 1, 1, 2 * fl), lambda b, w, g: (w, g, 0, 0)),
            pl.BlockSpec((3, HEAD_DIM), lambda b, w, g: (0, 0)),
        ],
        out_specs=pl.BlockSpec((1, 1, 1, CMP_PAD + nch, LANES), lambda b, w, g: (b, w, g, 0, 0)),
        out_shape=jax.ShapeDtypeStruct((b_, 2, g_, CMP_PAD + nch, LANES), F32),
        compiler_params=_params(("parallel", "parallel", "parallel")),
        name="nsa_compress",
    )(xc, w1, w2, pe, kg)


def _nsa_kernel(*refs):
    accs_ref, accw_ref, oc_ref = refs[-3:]
    step, last = pl.program_id(2), pl.num_programs(2) - 1

    @pl.when((pl.program_id(0) == 0) & (pl.program_id(1) == 0) & (step == 0))
    def _():
        accs_ref[...] = jnp.ones_like(accs_ref)
        accw_ref[...] = jnp.ones_like(accw_ref)
        oc_ref[...] = jnp.zeros_like(oc_ref)

    @pl.when(step < last)
    def _():
        _nsa_finish_previous(*refs)
        _nsa_attend(*refs)

    @pl.when(step == last)
    def _():
        _nsa_finish_previous(*refs)


def _nsa_finish_previous(q_ref, gate_ref, kc_ref, vc_ref, ks_ref, vs_ref, kw_ref, vw_ref,
                         tbm_ref, wbm_ref, cbm_ref, qmid_ref, ov_ref, o_ref, qa_ref,
                         accs_ref, accw_ref, oc_ref):
    qb = Q_BLOCK
    gt = gate_ref[0]
    for r in range(NSA_REP):
        rs = slice(qb * r, qb * (r + 1))
        o_s = accs_ref[rs, 0:HEAD_DIM] / accs_ref[rs, HEAD_DIM:HEAD_DIM + 1]
        o_w = accw_ref[rs, 0:HEAD_DIM] / accw_ref[rs, HEAD_DIM:HEAD_DIM + 1]
        o = (gt[:, 3 * r + 0:3 * r + 1] * oc_ref[rs, :] + gt[:, 3 * r + 1:3 * r + 2] * o_s
             + gt[:, 3 * r + 2:3 * r + 3] * o_w)
        o_ref[0, :, HEAD_DIM * r:HEAD_DIM * (r + 1)] = o.astype(o_ref.dtype)


def _nsa_attend(q_ref, gate_ref, kc_ref, vc_ref, ks_ref, vs_ref, kw_ref, vw_ref,
                tbm_ref, wbm_ref, cbm_ref, qmid_ref, ov_ref, o_ref, qa_ref,
                accs_ref, accw_ref, oc_ref):
    i = pl.program_id(2)
    qb = Q_BLOCK
    rows = NSA_REP * qb
    ncmp = CMP_PAD

    qblk = q_ref[0]
    for r in range(NSA_REP):
        rs = slice(qb * r, qb * (r + 1))
        qa_ref[rs, 0:HEAD_DIM] = qblk[:, HEAD_DIM * r:HEAD_DIM * (r + 1)]
        qa_ref[rs, HEAD_DIM:LANES] = jnp.broadcast_to(
            qmid_ref[0, r][:, 0:LANES - HEAD_DIM], (qb, LANES - HEAD_DIM)).astype(qa_ref.dtype)
    q_lo = qa_ref[:, 0:LANES]

    wstart = pl.multiple_of(KEY_PAD + qb * (i + 1) - WIN_KEYS, qb)
    s_w = _dot_nt(q_lo, kw_ref[0, 0, pl.ds(wstart, WIN_KEYS), :])

    def window_probs(r):
        s = s_w[qb * r:qb * (r + 1)] + wbm_ref[0, r]
        return jnp.exp2(s - jnp.max(s, axis=-1, keepdims=True))

    cstart = pl.multiple_of(8 * i + 8, 8)
    kc = kc_ref[0, 0, 0, pl.ds(cstart, ncmp), :]
    vc = vc_ref[0, 0, 0, pl.ds(cstart, ncmp), :]
    s_c = _dot_nt(q_lo, kc)

    kt = SEL_TILE
    n_tiles = (i * qb + qb + kt - 1) // kt

    def tile_start(dd):
        return pl.multiple_of(KEY_PAD + qb * (i + 1) - kt * (dd + 1), qb)

    near_base = [_dot_nt(q_lo, ks_ref[0, 0, pl.ds(tile_start(dd), kt), 0:LANES])
                 for dd in range(NEAR_TILES)]
    psum = jnp.zeros((qb, ncmp), F32)
    for r in range(NSA_REP):
        s = s_c[qb * r:qb * (r + 1)] + cbm_ref[0, r]
        m = jnp.max(s, axis=-1, keepdims=True)
        p = jnp.exp2(s - m)
        l = jnp.sum(p, axis=-1, keepdims=True)
        p = p * jnp.where(m > 0.5 * NEG, 1.0 / l, 0.0)
        oc_ref[qb * r:qb * (r + 1), :] = _dot(p, vc)[:, 0:HEAD_DIM]
        psum = psum + p
    p_hi = psum.astype(BF16)
    p_lo = (psum - p_hi.astype(F32)).astype(BF16)
    ov = ov_ref[...]
    imp = (jnp.dot(p_hi, ov, preferred_element_type=F32)
           + jnp.dot(p_lo, ov, preferred_element_type=F32))

    nsel = LANES
    sp = lax.broadcasted_iota(jnp.int32, (qb, nsel), 1)
    rr = lax.broadcasted_iota(jnp.int32, (qb, nsel), 0)
    cur = (nsel - 2) + (rr >= SLC_BLOCK).astype(jnp.int32)
    s_abs = sp + (2 * i + 2 - nsel)
    valid = s_abs >= 0
    forced = ((s_abs == 0) | (sp == cur) | (sp == cur - 1)) & valid
    excluded = forced | (sp > cur) | (~valid)
    cand_t = jnp.where(excluded, NEG, imp).T
    v = cand_t
    tau = None
    n_rank = SLC_TOPN - N_FORCED
    p_w = []
    for it in range(n_rank):
        tau = jnp.max(v, axis=0, keepdims=True)
        v = jnp.where(v >= tau, 3.0 * NEG, v)
        if it % (n_rank // NSA_REP) == 0 and len(p_w) < NSA_REP:
            p_w.append(window_probs(len(p_w)))
    accw_ref[...] = _dot(jnp.concatenate(p_w, axis=0), vw_ref[0, 0, pl.ds(wstart, WIN_KEYS), :])
    picked = jnp.where((cand_t >= tau) & (cand_t > 0.5 * NEG), 1.0, 0.0).T
    sel_neg = jnp.where((picked > 0.5) | forced, 0.0, NEG)
    sel_neg = pltpu.roll(sel_neg, shift=(2 * i + 2) % nsel, axis=1).astype(qa_ref.dtype)
    for r in range(NSA_REP):
        qa_ref[qb * r:qb * (r + 1), LANES:2 * LANES] = sel_neg
    q_ext = qa_ref[...]

    def tile_scores(dd):
        return _dot_nt(q_ext, ks_ref[0, 0, pl.ds(tile_start(dd), kt), :])

    def tile_softmax(s, dd):
        m_t = jnp.max(s, axis=-1, keepdims=True)
        return m_t, _dot(jnp.exp2(s - m_t), vs_ref[0, 0, pl.ds(tile_start(dd), kt), :])

    def merge(parts):
        m_new = functools.reduce(jnp.maximum, [m for m, _ in parts])
        acc = sum(jnp.exp2(m - m_new) * pv for m, pv in parts)
        return m_new, acc

    near = []
    for dd in range(NEAR_TILES):
        msk = _dot_nt(sel_neg, ks_ref[0, 0, pl.ds(tile_start(dd), kt), LANES:2 * LANES])
        near.append(jnp.concatenate([near_base[dd][qb * r:qb * (r + 1)] + (tbm_ref[0, r, dd] + msk)
                                     for r in range(NSA_REP)], axis=0))
    carry = merge([tile_softmax(s, dd) for dd, s in enumerate(near)])

    def far_group(first, width, carry):
        dds = [jnp.minimum(first + k, n_tiles) for k in range(width)]
        scores = [tile_scores(dd) for dd in dds]
        return merge([carry] + [tile_softmax(s, dd) for s, dd in zip(scores, dds)])

    first = NEAR_TILES
    left = jnp.maximum(n_tiles - NEAR_TILES, 0)
    for width in FAR_GROUPS:
        trips = (left + width - 1) // width if width == FAR_GROUPS[-1] else left // width
        carry = lax.fori_loop(
            0, trips, lambda u, c, first=first, width=width: far_group(first + width * u, width, c), carry)
        first = first + width * trips
        left = jnp.maximum(left - width * trips, 0)
    accs_ref[...] = carry[1]


def _nsa_attention(qn, gate, kcv, ksa, vsp, kwa, vwp, tbm, wbm, cbm, qmid, ov):
    b_, t_, _ = qn.shape
    tp = ksa.shape[2]
    ncp = kcv.shape[3]
    gw = NSA_REP * HEAD_DIM
    assert t_ // SLC_BLOCK <= LANES
    kvspec = lambda wd: pl.BlockSpec((1, 1, tp, wd), lambda b, g, i: (b, g, 0, 0))
    nq = t_ // Q_BLOCK
    behind = lambda b, g, i: (b, jnp.maximum(i - 1, 0), g)
    return pl.pallas_call(
        _nsa_kernel,
        grid=(b_, NSA_KV, nq + 1),
        in_specs=[
            pl.BlockSpec((1, Q_BLOCK, gw), lambda b, g, i: (b, jnp.minimum(i, nq - 1), g)),
            pl.BlockSpec((1, Q_BLOCK, LANES), behind),
            pl.BlockSpec((1, 1, 1, ncp, LANES), lambda b, g, i: (b, 0, g, 0, 0)),
            pl.BlockSpec((1, 1, 1, ncp, LANES), lambda b, g, i: (b, 1, g, 0, 0)),
            kvspec(2 * LANES), kvspec(LANES), kvspec(LANES), kvspec(LANES),
            pl.BlockSpec((1, NSA_REP, NEAR_TILES, Q_BLOCK, SEL_TILE), lambda b, g, i: (g, 0, 0, 0, 0)),
            pl.BlockSpec((1, NSA_REP, Q_BLOCK, WIN_KEYS), lambda b, g, i: (g, 0, 0, 0)),
            pl.BlockSpec((1, NSA_REP, Q_BLOCK, CMP_PAD), lambda b, g, i: (g, 0, 0, 0)),
            pl.BlockSpec((1, NSA_REP, 1, LANES), lambda b, g, i: (g, 0, 0, 0)),
            pl.BlockSpec((CMP_PAD, LANES), lambda b, g, i: (0, 0)),
        ],
        out_specs=pl.BlockSpec((1, Q_BLOCK, gw), behind),
        out_shape=jax.ShapeDtypeStruct((b_, t_, D_NSA), F32),
        scratch_shapes=[
            pltpu.VMEM((NSA_REP * Q_BLOCK, 2 * LANES), MXU_DTYPE),
            pltpu.VMEM((NSA_REP * Q_BLOCK, LANES), F32),
            pltpu.VMEM((NSA_REP * Q_BLOCK, LANES), F32),
            pltpu.VMEM((NSA_REP * Q_BLOCK, HEAD_DIM), F32),
        ],
        compiler_params=_params(("parallel", "parallel", "arbitrary")),
        name="nsa_attention",
    )(qn, gate, kcv, kcv, ksa, vsp, kwa, vwp, tbm, wbm, cbm, qmid, ov)


def _mix_kernel(x_ref, on_ref, gu_ref, vn_ref, sgw_ref, sgb_ref, gn_ref, gs_ref, wo_ref,
                fg_ref, wq_ref, x1_out, h2t_out, qp_out):
    tm = x_ref.shape[0]
    tri = (lax.broadcasted_iota(jnp.int32, (SG_CHUNK, SG_CHUNK), 0)
           >= lax.broadcasted_iota(jnp.int32, (SG_CHUNK, SG_CHUNK), 1))
    sgb = sgb_ref[...]
    y = x_ref[...] + _dot(_rms(on_ref[...], gn_ref[...]), wo_ref[0:D_NSA, :])
    parts = []
    for c in range(tm // SG_CHUNK):
        cs = slice(SG_CHUNK * c, SG_CHUNK * (c + 1))
        heads = []
        for hh in range(SG_HEADS):
            sl = slice(SG_DIM * hh, SG_DIM * (hh + 1))
            w = jnp.where(tri, sgw_ref[hh], 0.0)
            mixed = _dot(w, vn_ref[cs, sl]) + sgb[:, hh:hh + 1]
            heads.append(gu_ref[cs, sl].astype(F32) * mixed)
        parts.append(jnp.concatenate(heads, axis=1))
    o_sg = jnp.concatenate(parts, axis=0)
    y = y + _dot(_rms(o_sg, gs_ref[...]), wo_ref[D_NSA:D_NSA + D_SG, :])
    x1_out[...] = y
    h2 = _rms(y, fg_ref[...])
    h2t_out[...] = h2.T.astype(h2t_out.dtype)
    qp_out[...] = _dot(h2, wq_ref[...])


def _mix(x2, onsa, gu, vn, sgw, sgb_t, gn, gs, wo, fg, wq, tm):
    n_, d_ = x2.shape
    row = lambda wd: pl.BlockSpec((tm, wd), lambda i: (i, 0))
    const = lambda shape, **kw: pl.BlockSpec(shape, lambda i: (0,) * len(shape), **kw)
    return pl.pallas_call(
        _mix_kernel,
        grid=(n_ // tm,),
        in_specs=[
            row(d_), row(D_NSA), row(D_SG), row(D_SG),
            const((SG_HEADS, SG_CHUNK, SG_CHUNK)), const((SG_CHUNK, SG_HEADS)),
            const((1, D_NSA)), const((1, D_SG)),
            const((D_NSA + D_SG, d_), pipeline_mode=pl.Buffered(1)),
            const((1, d_)),
            const((d_, wq.shape[1]), pipeline_mode=pl.Buffered(1)),
        ],
        out_specs=[row(d_), pl.BlockSpec((d_, tm), lambda i: (0, i)), row(wq.shape[1])],
        out_shape=[
            jax.ShapeDtypeStruct((n_, d_), F32),
            jax.ShapeDtypeStruct((d_, n_), MXU_DTYPE),
            jax.ShapeDtypeStruct((n_, wq.shape[1]), F32),
        ],
        compiler_params=_params(("parallel",)),
        name="mix_out_proj",
    )(x2, onsa, gu, vn, sgw, sgb_t, gn, gs, wo, fg, wq)


def _top_rows(v, k):
    tops = []
    for _ in range(k):
        m = jnp.max(v, axis=0, keepdims=True)
        tops.append(m)
        v = jnp.where(v >= m, 0.0, v)
    return jnp.concatenate(tops, axis=0)


def _pair_products(a, b):
    cand = [a[0:1] * b]
    for ra in range(1, 8):
        cand.append(a[ra:ra + 1] * b[0:8])
    cand.append(a[8:16] * b[0:1])
    return jnp.concatenate(cand, axis=0)


def _peer_topk_kernel(qp_ref, sk_ref, e1_out, e2_out, eps_out):
    for hh in range(PEER_HEADS):
        base = 2 * PEER_HALF * hh
        s1 = _dot_nt(sk_ref[0], qp_ref[:, base:base + PEER_HALF])
        s2 = _dot_nt(sk_ref[1], qp_ref[:, base + PEER_HALF:base + 2 * PEER_HALF])
        e1 = jnp.exp(s1 - jnp.max(s1, axis=0, keepdims=True))
        e2 = jnp.exp(s2 - jnp.max(s2, axis=0, keepdims=True))
        a = _top_rows(e1, PEER_TOPK)
        b = _top_rows(e2, PEER_TOPK)
        z = jnp.sum(_top_rows(_pair_products(a, b), PEER_TOPK), axis=0, keepdims=True)
        zinv = 1.0 / z
        gates = _top_rows(_pair_products(a * zinv, b), PEER_TOPK)
        e1_out[hh] = e1 * zinv
        for j in range(e2.shape[1] // LANES):
            e2_out[hh, j] = e2[:, LANES * j:LANES * (j + 1)]
        eps_out[hh:hh + 1, :] = gates[PEER_TOPK - 1:PEER_TOPK]


def _peer_topk(qp, sub_keys, tm):
    n_, qd = qp.shape
    return pl.pallas_call(
        _peer_topk_kernel,
        grid=(n_ // tm,),
        in_specs=[
            pl.BlockSpec((tm, qd), lambda i: (i, 0)),
            pl.BlockSpec((2, PEER_NKEYS, PEER_HALF), lambda i: (0, 0, 0)),
        ],
        out_specs=[
            pl.BlockSpec((PEER_HEADS, PEER_NKEYS, tm), lambda i: (0, 0, i)),
            pl.BlockSpec((PEER_HEADS, tm // LANES, PEER_NKEYS, LANES), lambda i: (0, i, 0, 0)),
            pl.BlockSpec((PEER_HEADS, tm), lambda i: (0, i)),
        ],
        out_shape=[
            jax.ShapeDtypeStruct((PEER_HEADS, PEER_NKEYS, n_), F32),
            jax.ShapeDtypeStruct((PEER_HEADS, n_ // LANES, PEER_NKEYS, LANES), F32),
            jax.ShapeDtypeStruct((PEER_HEADS, n_), F32),
        ],
        compiler_params=_params(("parallel",)),
        name="peer_topk",
    )(qp, sub_keys)


def _peer_mix_kernel(n_tiles, ht_ref, u_ref, vt_ref, e1_ref, e2_ref, eps_ref, x1_ref, out_ref,
                     acc_ref, crow_ref, a0_ref, a1_ref, z0_ref, z1_ref):
    ie = pl.program_id(1)
    te = u_ref.shape[0]
    tm = ht_ref.shape[1]
    per_tile = te // PEER_NKEYS

    @pl.when(ie == 0)
    def _():
        acc_ref[...] = jnp.zeros_like(acc_ref)

    d_ = vt_ref.shape[1]
    n_lane = tm // LANES
    n_blocks = per_tile * n_lane
    a_units = [(r, c, kc) for r in range(te // MXU_ROWS) for c in range(tm // MXU_ROWS)
               for kc in range(d_ // A_KCHUNK)]
    c_units = [(r, c) for r in range(d_ // MXU_ROWS) for c in range(tm // MXU_ROWS)]

    def step(a_w, a_r, z_w, z_r):
        tile = ie - 1

        def a_unit(r, c, kc):
            rs = slice(MXU_ROWS * r, MXU_ROWS * (r + 1))
            cs = slice(MXU_ROWS * c, MXU_ROWS * (c + 1))
            ds = slice(A_KCHUNK * kc, A_KCHUNK * (kc + 1))
            res = jnp.dot(u_ref[rs, ds], ht_ref[ds, cs], preferred_element_type=F32)
            for jj in range(MXU_ROWS // LANES):
                j = c * (MXU_ROWS // LANES) + jj
                part = res[:, LANES * jj:LANES * (jj + 1)]
                if kc == 0:
                    a_w[j, rs, :] = part
                else:
                    a_w[j, rs, :] += part

        def c_unit(r, c):
            rs = slice(MXU_ROWS * r, MXU_ROWS * (r + 1))
            cs = slice(MXU_ROWS * c, MXU_ROWS * (c + 1))
            res = jnp.dot(vt_ref[0, rs, :], z_r[:, cs], preferred_element_type=F32)
            for jj in range(MXU_ROWS // LANES):
                acc_ref[c * (MXU_ROWS // LANES) + jj, rs, :] += res[:, LANES * jj:LANES * (jj + 1)]

        def b_block(k, j):
            ls = slice(LANES * j, LANES * (j + 1))
            for part in range(PEER_NKEYS // GATE_ROWS):
                ks = slice(GATE_ROWS * part, GATE_ROWS * (part + 1))
                rs = slice(PEER_NKEYS * k + GATE_ROWS * part, PEER_NKEYS * k + GATE_ROWS * (part + 1))
                g = jnp.zeros((GATE_ROWS, LANES), F32)
                for hh in range(PEER_HEADS):
                    row = PEER_HEADS * k + hh
                    gate = e2_ref[hh, j, ks, :] * crow_ref[row:row + 1, ls]
                    g = g + jnp.where(gate >= eps_ref[hh:hh + 1, ls], gate, 0.0)
                z_w[rs, ls] = (_gelu(a_r[j, rs, :]) * g).astype(z_w.dtype)

        do_a, do_b, do_c = a_w is not None, z_w is not None, z_r is not None
        if do_b:
            for k in range(per_tile):
                i1 = tile * per_tile + k
                for hh in range(PEER_HEADS):
                    row = PEER_HEADS * k + hh
                    crow_ref[row:row + 1, :] = e1_ref[hh, pl.ds(i1, 1), :]

        a_iter = iter(a_units if do_a else [])
        c_iter = iter(c_units if do_c else [])
        a_every = max(n_blocks // len(a_units), 1)
        c_per = -(-len(c_units) // n_blocks)
        for blk in range(n_blocks):
            k, j = divmod(blk, n_lane)
            if blk % a_every == 0:
                unit = next(a_iter, None)
                if unit is not None:
                    a_unit(*unit)
            for _ in range(c_per):
                unit = next(c_iter, None)
                if unit is not None:
                    c_unit(*unit)
            if do_b:
                b_block(k, j)
        for unit in a_iter:
            a_unit(*unit)
        for unit in c_iter:
            c_unit(*unit)

    a_bufs, z_bufs = (a0_ref, a1_ref), (z0_ref, z1_ref)

    def run(par, do_a=True, do_b=True, do_c=True):
        step(a_bufs[par] if do_a else None, a_bufs[1 - par], z_bufs[1 - par] if do_b else None,
             z_bufs[par] if do_c else None)

    last = n_tiles + 1
    pl.when(ie == 0)(lambda: run(0, do_b=False, do_c=False))
    pl.when(ie == 1)(lambda: run(1, do_c=False))
    pl.when((ie >= 2) & (ie < n_tiles) & (ie % 2 == 0))(lambda: run(0))
    pl.when((ie >= 2) & (ie < n_tiles) & (ie % 2 == 1))(lambda: run(1))
    pl.when(ie == n_tiles)(lambda: run(n_tiles % 2, do_a=False))
    pl.when(ie == last)(lambda: run(last % 2, do_a=False, do_b=False))

    @pl.when(ie == last)
    def _():
        for j in range(n_lane):
            ts = slice(LANES * j, LANES * (j + 1))
            out_ref[ts, :] = x1_ref[ts, :] + acc_ref[j].T


def _peer_mix(h2t, u, vt, e1t, e2t, eps, x1, tm, te):
    d_, n_ = h2t.shape
    n_tiles = u.shape[0] // te
    assert n_tiles >= 3
    return pl.pallas_call(
        functools.partial(_peer_mix_kernel, n_tiles),
        grid=(n_ // tm, n_tiles + 2),
        in_specs=[
            pl.BlockSpec((d_, tm), lambda it, ie: (0, it)),
            pl.BlockSpec((te, d_), lambda it, ie: (jnp.minimum(ie, n_tiles - 1), 0)),
            pl.BlockSpec((1, d_, te), lambda it, ie: (jnp.maximum(ie - 2, 0), 0, 0)),
            pl.BlockSpec((PEER_HEADS, PEER_NKEYS, tm), lambda it, ie: (0, 0, it)),
            pl.BlockSpec((PEER_HEADS, tm // LANES, PEER_NKEYS, LANES), lambda it, ie: (0, it, 0, 0)),
            pl.BlockSpec((PEER_HEADS, tm), lambda it, ie: (0, it)),
            pl.BlockSpec((tm, d_), lambda it, ie: (it, 0)),
        ],
        out_specs=pl.BlockSpec((tm, d_), lambda it, ie: (it, 0)),
        out_shape=jax.ShapeDtypeStruct((n_, d_), F32),
        scratch_shapes=[
            pltpu.VMEM((tm // LANES, d_, LANES), F32),
            pltpu.VMEM((PEER_HEADS * (te // PEER_NKEYS), tm), F32),
            pltpu.VMEM((tm // LANES, te, LANES), F32), pltpu.VMEM((tm // LANES, te, LANES), F32),
            pltpu.VMEM((te, tm), MXU_DTYPE), pltpu.VMEM((te, tm), MXU_DTYPE),
        ],
        compiler_params=_params(("parallel", "arbitrary")),
        name="peer_mix",
    )(h2t, u, vt, e1t, e2t, eps, x1)


def _rel_bucket(dist):
    n = jnp.maximum(dist, 0)
    max_exact = REL_BUCKETS // 2
    nf = jnp.maximum(n, 1).astype(F32)
    large = max_exact + (jnp.log(nf / max_exact) / math.log(REL_MAX_DIST / max_exact)
                         * (REL_BUCKETS - max_exact)).astype(jnp.int32)
    large = jnp.minimum(large, REL_BUCKETS - 1)
    return jnp.where(n < max_exact, n, large)


def _toeplitz(btab, off, rows, cols):
    per = rows + cols
    k = np.concatenate([np.arange(cols), np.zeros(1, np.int64), np.arange(-(rows - 1), 0)])
    idx = np.clip(off - k, 0, btab.shape[1] - 1)
    w = jnp.take(btab, jnp.asarray(idx, jnp.int32), axis=1)
    x = jnp.tile(w, (1, rows))[:, :rows * (per - 1)]
    return x.reshape(btab.shape[0], rows, per - 1)[:, :, :cols]


def _bias_tiles(rel_bias):
    dist = jnp.arange(BIAS_TABLE, dtype=jnp.int32)
    rel_bias = rel_bias.astype(F32) * LOG2E
    btab = rel_bias[_rel_bucket(dist)].T
    tbw = jnp.stack([_toeplitz(btab, SEL_TILE * (dd + 1) - Q_BLOCK, Q_BLOCK, SEL_TILE)
                     for dd in range(NEAR_TILES)], axis=1)
    wb = _toeplitz(btab, WINDOW, Q_BLOCK, WIN_KEYS)
    off = CMP_STRIDE * (LANES - 8) - (CMP_BLOCK - 1)
    front = CMP_STRIDE * LANES - off
    length = CMP_STRIDE * (LANES + 8 + 2)
    padded = jnp.pad(btab, ((0, 0), (front, length - front - BIAS_TABLE)))
    ch = padded.reshape(btab.shape[0], length // CMP_STRIDE, CMP_STRIDE)
    nwin = LANES + 1
    win = jnp.stack([ch[:, a:a + nwin] for a in range(Q_BLOCK // CMP_STRIDE)], axis=2)
    win = win.reshape(btab.shape[0], nwin, Q_BLOCK)
    cb = jnp.swapaxes(win[:, 1:nwin][:, ::-1], 1, 2)
    cfar = rel_bias.astype(F32)[REL_BUCKETS - 1]
    grp = lambda a: a.reshape((NSA_KV, NSA_REP) + a.shape[1:])

    r = np.arange(Q_BLOCK)[:, None]
    near_ok = np.stack([r - np.arange(SEL_TILE)[None, :] + SEL_TILE * (dd + 1) - Q_BLOCK >= 0
                        for dd in range(NEAR_TILES)])
    tbm = jnp.where(near_ok, tbw - cfar[:, None, None, None], NEG)
    dist_w = r - np.arange(WIN_KEYS)[None, :] + WINDOW
    wbm = jnp.where((dist_w >= 0) & (dist_w < WINDOW), wb, NEG)
    dist_c = r - CMP_STRIDE * np.arange(CMP_PAD)[None, :] + (CMP_STRIDE * CMP_PAD - Q_BLOCK - (CMP_BLOCK - 1))
    cbias = jnp.concatenate(
        [jnp.broadcast_to(cfar[:, None, None], (NSA_HEADS, Q_BLOCK, CMP_PAD - LANES)), cb], axis=2)
    cbm = jnp.where(dist_c >= 0, cbias, NEG)

    hi = cfar.astype(BF16).astype(F32)
    lo = (cfar - hi).astype(BF16).astype(F32)
    qmid = jnp.zeros((NSA_HEADS, 1, LANES), F32)
    qmid = qmid.at[:, 0, FAR_HI_LANE - HEAD_DIM].set(hi).at[:, 0, FAR_LO_LANE - HEAD_DIM].set(lo)
    qmid = qmid.at[:, 0, PAD_FLAG_LANE - HEAD_DIM].set(NEG)
    return grp(tbm), grp(wbm), grp(cbm), grp(qmid)


def _extend_keys(ks, kw):
    tp = ks.shape[2] + KEY_PAD
    pos = np.arange(tp) - KEY_PAD
    real = pos >= 0
    mid_s = np.zeros((tp, LANES - HEAD_DIM), np.float32)
    mid_s[:, FAR_HI_LANE - HEAD_DIM] = real
    mid_s[:, FAR_LO_LANE - HEAD_DIM] = real
    mid_s[:, PAD_FLAG_LANE - HEAD_DIM] = ~real
    onehot = ((pos[:, None] // SLC_BLOCK) == np.arange(LANES)[None, :]) & real[:, None]
    mid_w = np.zeros((tp, LANES - HEAD_DIM), np.float32)
    mid_w[:, PAD_FLAG_LANE - HEAD_DIM] = ~real
    padk = lambda a: jnp.pad(a, ((0, 0), (0, 0), (KEY_PAD, 0), (0, 0)))
    bc = lambda a: jnp.broadcast_to(jnp.asarray(a, ks.dtype), ks.shape[:2] + a.shape)
    ksa = jnp.concatenate([padk(ks), bc(mid_s), bc(onehot.astype(np.float32))], axis=-1)
    kwa = jnp.concatenate([padk(kw), bc(mid_w)], axis=-1)
    return ksa, kwa


def _overlap_matrix():
    c = np.arange(CMP_PAD)[:, None]
    s = np.arange(LANES)[None, :]
    per = SLC_BLOCK // CMP_STRIDE
    ov = (s == c // per) | ((c % per == per - 1) & (s == c // per + 1))
    return jnp.asarray(ov.astype(np.float32), BF16)


def kernel(x, attn_norm_g, w_in, q_norm_g, k_norm_g, cmp_pe_k, cmp_w1_k, cmp_w2_k, cmp_pe_v,
           cmp_w1_v, cmp_w2_v, rel_bias, sg_ln_g, sg_ln_b, sg_w, sg_b, out_norm_nsa, out_norm_sg,
           w_out, ffn_norm_g, peer_w_query, peer_sub_keys, peer_u, peer_v):
    b_, t_, d_ = x.shape
    n_ = b_ * t_
    assert t_ % Q_BLOCK == 0 and t_ >= WIN_KEYS and d_ == D_NSA + D_SG

    c_kv = D_NSA + 6 * D_KV
    c_gate = c_kv + 3 * NSA_HEADS
    wg = w_in[:, c_kv:c_gate].reshape(d_, NSA_KV, 3 * NSA_REP)
    wg = jnp.pad(wg, ((0, 0), (0, 0), (0, LANES - 3 * NSA_REP))).reshape(d_, NSA_KV * LANES)
    w_attn = jnp.concatenate([w_in[:, :c_kv], wg], axis=1).astype(MXU_DTYPE)
    w_sgu = w_in[:, c_gate:].astype(MXU_DTYPE)
    ng = attn_norm_g.reshape(1, d_)

    gq = jnp.tile(q_norm_g, NSA_HEADS).reshape(1, D_NSA) * (HEAD_DIM ** -0.5 * LOG2E)
    no_gain = jnp.zeros((D_KV,), F32)
    gkv = jnp.concatenate([jnp.tile(k_norm_g[1], NSA_KV), no_gain,
                           jnp.tile(k_norm_g[2], NSA_KV), no_gain]).reshape(1, 4 * D_KV)
    qn, kcv, kv, gate = _attn_proj(x, ng, w_attn, gq, gkv, tm=256)
    heads = lambda a: a.reshape(b_, t_, NSA_KV, HEAD_DIM).transpose(0, 2, 1, 3)
    ones_col = jnp.zeros((b_, NSA_KV, t_, HEAD_DIM), kv.dtype).at[..., 0].set(1)
    ks, kw = heads(kv[..., 0:D_KV]), heads(kv[..., 2 * D_KV:3 * D_KV])
    vs = jnp.concatenate([heads(kv[..., D_KV:2 * D_KV]), ones_col], axis=-1)
    vw = jnp.concatenate([heads(kv[..., 3 * D_KV:4 * D_KV]), ones_col], axis=-1)
    gu, vn = _sgu_proj(x, ng, w_sgu, sg_ln_g, sg_ln_b, tm=256)

    nch = t_ // CMP_STRIDE
    xc = kcv.reshape(b_, nch, CMP_STRIDE, 2, NSA_KV, HEAD_DIM).transpose(0, 3, 4, 1, 2, 5)
    xc = xc.reshape(b_, 2, NSA_KV, nch, CMP_STRIDE * HEAD_DIM)
    w1 = jnp.stack([cmp_w1_k, cmp_w1_v]).astype(MXU_DTYPE)
    w2 = jnp.stack([cmp_w2_k, cmp_w2_v]).astype(MXU_DTYPE)
    pe = jnp.stack([cmp_pe_k, cmp_pe_v]).transpose(0, 2, 1, 3).reshape(2, NSA_KV, 1, CMP_BLOCK * HEAD_DIM)
    kcv_c = _compress(xc, w1, w2, pe.astype(MXU_DTYPE), k_norm_g)

    padk = lambda a: jnp.pad(a, ((0, 0), (0, 0), (KEY_PAD, 0), (0, 0)))
    tbm, wbm, cbm, qmid = _bias_tiles(rel_bias)
    ksa, kwa = _extend_keys(ks, kw)
    onsa = _nsa_attention(qn, gate, kcv_c, ksa, padk(vs), kwa, padk(vw),
                          tbm, wbm, cbm, qmid, _overlap_matrix())

    x1, h2t, qp = _mix(x.reshape(n_, d_), onsa.reshape(n_, D_NSA), gu.reshape(n_, D_SG),
                      vn.reshape(n_, D_SG), sg_w, sg_b.T, out_norm_nsa.reshape(1, D_NSA),
                      out_norm_sg.reshape(1, D_SG), w_out.astype(MXU_DTYPE),
                      ffn_norm_g.reshape(1, d_), peer_w_query.astype(MXU_DTYPE), tm=256)

    e1t, e2t, eps = _peer_topk(qp, peer_sub_keys, tm=256)
    te = 512
    vt = peer_v.reshape(-1, te, d_).transpose(0, 2, 1).astype(MXU_DTYPE)
    out = _peer_mix(h2t, peer_u.astype(MXU_DTYPE), vt, e1t, e2t, eps, x1, tm=min(512, n_), te=te)
    return out.reshape(b_, t_, d_)
```

```python
import functools
import math

import jax
import jax.numpy as jnp
import numpy as np
from jax import lax
from jax.experimental import pallas as pl
from jax.experimental.pallas import tpu as pltpu

F32 = jnp.float32
BF16 = jnp.bfloat16
MXU_DTYPE = BF16

EPS = 1e-6
NEG = -1e30
SQRT_HALF = 0.7071067811865476
LOG2E = 1.4426950408889634

NSA_HEADS = 16
NSA_KV = 4
NSA_REP = 4
HEAD_DIM = 64
D_NSA = NSA_HEADS * HEAD_DIM
D_KV = NSA_KV * HEAD_DIM
CMP_STRIDE = 16
CMP_BLOCK = 32
CMP_HIDDEN = 128
SLC_BLOCK = 64
SLC_TOPN = 16
WINDOW = 512
Q_BLOCK = 128
SG_HEADS = 8
SG_DIM = 128
SG_CHUNK = 128
D_SG = SG_HEADS * SG_DIM
REL_BUCKETS = 32
REL_MAX_DIST = 1024
PEER_HEADS = 8
PEER_NKEYS = 128
PEER_HALF = 128
PEER_TOPK = 16

LANES = 128
MXU_ROWS = 256
A_KCHUNK = 1024
GATE_ROWS = 64
KEY_PAD = 1024
CMP_PAD = 512
SEL_TILE = 512
WIN_KEYS = WINDOW + Q_BLOCK
NEAR_TILES = 2
FAR_GROUPS = (8, 4, 2)
BIAS_TABLE = 2048
FAR_DIST = 897
FAR_HI_LANE = 64
FAR_LO_LANE = 65
PAD_FLAG_LANE = 66
N_FORCED = 3
VMEM_LIMIT = 56 * 1024 * 1024


def _gelu(x):
    return 0.5 * x * (1.0 + lax.erf(x * SQRT_HALF))


def _dot(a, b):
    return jnp.dot(a.astype(MXU_DTYPE), b.astype(MXU_DTYPE), preferred_element_type=F32)


def _dot_nt(a, b):
    return lax.dot_general(a.astype(MXU_DTYPE), b.astype(MXU_DTYPE),
                           (((1,), (1,)), ((), ())), preferred_element_type=F32)


def _rms(x, g):
    ms = jnp.mean(x * x, axis=-1, keepdims=True)
    return x * lax.rsqrt(ms + EPS) * g


def _params(sem):
    return pltpu.CompilerParams(dimension_semantics=sem, vmem_limit_bytes=VMEM_LIMIT)


def _attn_proj_kernel(x_ref, ng_ref, w_ref, gq_ref, gkv_ref, hs_ref, he_ref,
                      q_out, kcv_out, kv_out, gate_out):
    h = _rms(x_ref[0], ng_ref[...]).astype(MXU_DTYPE)

    def head_rms(p):
        ms = jnp.dot((p * p).astype(MXU_DTYPE), hs_ref[...], preferred_element_type=F32)
        r = lax.rsqrt(ms + EPS)
        r_hi = r.astype(BF16)
        r_lo = (r - r_hi.astype(F32)).astype(BF16)
        he = he_ref[...]
        return p * (jnp.dot(r_hi, he, preferred_element_type=F32)
                    + jnp.dot(r_lo, he, preferred_element_type=F32))

    pq = _dot(h, w_ref[:, 0:D_NSA])
    q_out[0] = (head_rms(pq) * gq_ref[...]).astype(q_out.dtype)

    kcv_out[0] = _dot(h, w_ref[:, D_NSA:D_NSA + 2 * D_KV]).astype(kcv_out.dtype)

    base = D_NSA + 2 * D_KV
    p4 = _dot(h, w_ref[:, base:base + 4 * D_KV])
    col = lax.broadcasted_iota(jnp.int32, (1, 4 * D_KV), 1)
    is_key = (col // D_KV) % 2 == 0
    kv_out[0] = jnp.where(is_key, head_rms(p4) * gkv_ref[...], p4).astype(kv_out.dtype)

    base = base + 4 * D_KV
    gate_out[0] = jax.nn.sigmoid(_dot(h, w_ref[:, base:base + NSA_KV * LANES]))


def _attn_proj(x, ng, w, gq, gkv, tm):
    b_, t_, d_ = x.shape
    nw = w.shape[1]
    heads = np.arange(D_NSA) // HEAD_DIM
    hs = (heads[:, None] == np.arange(LANES)[None, :]).astype(np.float32)
    const = lambda shape: pl.BlockSpec(shape, lambda b, i: (0,) * len(shape))
    return pl.pallas_call(
        _attn_proj_kernel,
        grid=(b_, t_ // tm),
        in_specs=[
            pl.BlockSpec((1, tm, d_), lambda b, i: (b, i, 0)),
            const((1, d_)),
            pl.BlockSpec((d_, nw), lambda b, i: (0, 0), pipeline_mode=pl.Buffered(1)),
            const((1, D_NSA)), const((1, 4 * D_KV)),
            const((D_NSA, LANES)), const((LANES, D_NSA)),
        ],
        out_specs=[
            pl.BlockSpec((1, tm, D_NSA), lambda b, i: (b, i, 0)),
            pl.BlockSpec((1, tm, 2 * D_KV), lambda b, i: (b, i, 0)),
            pl.BlockSpec((1, tm, 4 * D_KV), lambda b, i: (b, i, 0)),
            pl.BlockSpec((1, tm, NSA_KV * LANES), lambda b, i: (b, i, 0)),
        ],
        out_shape=[
            jax.ShapeDtypeStruct((b_, t_, D_NSA), MXU_DTYPE),
            jax.ShapeDtypeStruct((b_, t_, 2 * D_KV), MXU_DTYPE),
            jax.ShapeDtypeStruct((b_, t_, 4 * D_KV), MXU_DTYPE),
            jax.ShapeDtypeStruct((b_, t_, NSA_KV * LANES), F32),
        ],
        compiler_params=_params(("parallel", "parallel")),
        name="attn_proj",
    )(x, ng, w, gq, gkv, jnp.asarray(hs / HEAD_DIM, MXU_DTYPE), jnp.asarray(hs.T, BF16))


def _sgu_proj_kernel(x_ref, ng_ref, w_ref, lng_ref, lnb_ref, gu_out, vn_out):
    h = _rms(x_ref[0], ng_ref[...]).astype(MXU_DTYPE)
    gu_out[0] = _gelu(_dot(h, w_ref[:, 0:D_SG])).astype(gu_out.dtype)
    gv = _gelu(_dot(h, w_ref[:, D_SG:2 * D_SG]))
    for hh in range(SG_HEADS):
        sl = slice(SG_DIM * hh, SG_DIM * (hh + 1))
        v = gv[:, sl]
        mu = jnp.mean(v, axis=-1, keepdims=True)
        var = jnp.mean(jnp.square(v - mu), axis=-1, keepdims=True)
        y = (v - mu) * lax.rsqrt(var + EPS) * lng_ref[hh:hh + 1, :] + lnb_ref[hh:hh + 1, :]
        vn_out[0, :, sl] = y.astype(vn_out.dtype)


def _sgu_proj(x, ng, w, lng, lnb, tm):
    b_, t_, d_ = x.shape
    return pl.pallas_call(
        _sgu_proj_kernel,
        grid=(b_, t_ // tm),
        in_specs=[
            pl.BlockSpec((1, tm, d_), lambda b, i: (b, i, 0)),
            pl.BlockSpec((1, d_), lambda b, i: (0, 0)),
            pl.BlockSpec((d_, 2 * D_SG), lambda b, i: (0, 0), pipeline_mode=pl.Buffered(1)),
            pl.BlockSpec((SG_HEADS, SG_DIM), lambda b, i: (0, 0)),
            pl.BlockSpec((SG_HEADS, SG_DIM), lambda b, i: (0, 0)),
        ],
        out_specs=[
            pl.BlockSpec((1, tm, D_SG), lambda b, i: (b, i, 0)),
            pl.BlockSpec((1, tm, D_SG), lambda b, i: (b, i, 0)),
        ],
        out_shape=[
            jax.ShapeDtypeStruct((b_, t_, D_SG), MXU_DTYPE),
            jax.ShapeDtypeStruct((b_, t_, D_SG), MXU_DTYPE),
        ],
        compiler_params=_params(("parallel", "parallel")),
        name="sgu_proj",
    )(x, ng, w, lng, lnb)


def _compress_kernel(x_ref, w1_ref, w2_ref, pe_ref, kg_ref, out_ref):
    which = pl.program_id(1)
    x = x_ref[0, 0, 0]
    nch = x.shape[0]
    half = CMP_STRIDE * HEAD_DIM
    a = _dot(x, w1_ref[0, 0, 0:half, :])
    bm = _dot(x, w1_ref[0, 0, half:2 * half, :])
    pec = _dot(pe_ref[0, 0], w1_ref[0, 0])
    hid = _gelu(a + pltpu.roll(bm, shift=nch - 1, axis=0) + pec)
    y = _dot(hid, w2_ref[0, 0])
    y = jnp.where(which == 0, _rms(y, kg_ref[0:1, :]), y)
    lane = lax.broadcasted_iota(jnp.int32, (CMP_PAD, LANES), 1)
    flag = ((lane == PAD_FLAG_LANE) & (which == 0)).astype(out_ref.dtype)
    out_ref[0, 0, 0, 0:CMP_PAD, :] = flag
    out_ref[0, 0, 0, CMP_PAD:CMP_PAD + nch, 0:HEAD_DIM] = y.astype(out_ref.dtype)
    out_ref[0, 0, 0, CMP_PAD:CMP_PAD + nch, HEAD_DIM:LANES] = jnp.zeros((nch, LANES - HEAD_DIM), out_ref.dtype)


def _compress(xc, w1, w2, pe, kg):
    b_, _, g_, nch, fl = xc.shape
    return pl.pallas_call(
        _compress_kernel,
        grid=(b_, 2, g_),
        in_specs=[
            pl.BlockSpec((1, 1, 1, nch, fl), lambda b, w, g: (b, w, g, 0, 0)),
            pl.BlockSpec((1, 1, 2 * fl, CMP_HIDDEN), lambda b, w, g: (w, g, 0, 0)),
            pl.BlockSpec((1, 1, CMP_HIDDEN, HEAD_DIM), lambda b, w, g: (w, g, 0, 0)),
            pl.BlockSpec((1, 1, 1, 2 * fl), lambda b, w, g: (w, g, 0, 0)),
            pl.BlockSpec((3, HEAD_DIM), lambda b, w, g: (0, 0)),
        ],
        out_specs=pl.BlockSpec((1, 1, 1, CMP_PAD + nch, LANES), lambda b, w, g: (b, w, g, 0, 0)),
        out_shape=jax.ShapeDtypeStruct((b_, 2, g_, CMP_PAD + nch, LANES), F32),
        compiler_params=_params(("parallel", "parallel", "parallel")),
        name="nsa_compress",
    )(xc, w1, w2, pe, kg)


def _nsa_kernel(*refs):
    accs_ref, accw_ref, oc_ref = refs[-3:]
    step, last = pl.program_id(2), pl.num_programs(2) - 1

    @pl.when((pl.program_id(0) == 0) & (pl.program_id(1) == 0) & (step == 0))
    def _():
        accs_ref[...] = jnp.ones_like(accs_ref)
        accw_ref[...] = jnp.ones_like(accw_ref)
        oc_ref[...] = jnp.zeros_like(oc_ref)

    @pl.when(step < last)
    def _():
        _nsa_finish_previous(*refs)
        _nsa_attend(*refs)

    @pl.when(step == last)
    def _():
        _nsa_finish_previous(*refs)


def _nsa_finish_previous(q_ref, gate_ref, kc_ref, vc_ref, ks_ref, vs_ref, kw_ref, vw_ref,
                         tbm_ref, wbm_ref, cbm_ref, qmid_ref, ov_ref, o_ref, qa_ref,
                         accs_ref, accw_ref, oc_ref):
    qb = Q_BLOCK
    gt = gate_ref[0]
    for r in range(NSA_REP):
        rs = slice(qb * r, qb * (r + 1))
        o_s = accs_ref[rs, 0:HEAD_DIM] / accs_ref[rs, HEAD_DIM:HEAD_DIM + 1]
        o_w = accw_ref[rs, 0:HEAD_DIM] / accw_ref[rs, HEAD_DIM:HEAD_DIM + 1]
        o = (gt[:, 3 * r + 0:3 * r + 1] * oc_ref[rs, :] + gt[:, 3 * r + 1:3 * r + 2] * o_s
             + gt[:, 3 * r + 2:3 * r + 3] * o_w)
        o_ref[0, :, HEAD_DIM * r:HEAD_DIM * (r + 1)] = o.astype(o_ref.dtype)


def _nsa_attend(q_ref, gate_ref, kc_ref, vc_ref, ks_ref, vs_ref, kw_ref, vw_ref,
                tbm_ref, wbm_ref, cbm_ref, qmid_ref, ov_ref, o_ref, qa_ref,
                accs_ref, accw_ref, oc_ref):
    i = pl.program_id(2)
    qb = Q_BLOCK
    rows = NSA_REP * qb
    ncmp = CMP_PAD

    qblk = q_ref[0]
    for r in range(NSA_REP):
        rs = slice(qb * r, qb * (r + 1))
        qa_ref[rs, 0:HEAD_DIM] = qblk[:, HEAD_DIM * r:HEAD_DIM * (r + 1)]
        qa_ref[rs, HEAD_DIM:LANES] = jnp.broadcast_to(
            qmid_ref[0, r][:, 0:LANES - HEAD_DIM], (qb, LANES - HEAD_DIM)).astype(qa_ref.dtype)
    q_lo = qa_ref[:, 0:LANES]

    wstart = pl.multiple_of(KEY_PAD + qb * (i + 1) - WIN_KEYS, qb)
    s_w = _dot_nt(q_lo, kw_ref[0, 0, pl.ds(wstart, WIN_KEYS), :])

    def window_probs(r):
        s = s_w[qb * r:qb * (r + 1)] + wbm_ref[0, r]
        return jnp.exp2(s - jnp.max(s, axis=-1, keepdims=True))

    cstart = pl.multiple_of(8 * i + 8, 8)
    kc = kc_ref[0, 0, 0, pl.ds(cstart, ncmp), :]
    vc = vc_ref[0, 0, 0, pl.ds(cstart, ncmp), :]
    s_c = _dot_nt(q_lo, kc)

    kt = SEL_TILE
    n_tiles = (i * qb + qb + kt - 1) // kt

    def tile_start(dd):
        return pl.multiple_of(KEY_PAD + qb * (i + 1) - kt * (dd + 1), qb)

    near_base = [_dot_nt(q_lo, ks_ref[0, 0, pl.ds(tile_start(dd), kt), 0:LANES])
                 for dd in range(NEAR_TILES)]
    psum = jnp.zeros((qb, ncmp), F32)
    for r in range(NSA_REP):
        s = s_c[qb * r:qb * (r + 1)] + cbm_ref[0, r]
        m = jnp.max(s, axis=-1, keepdims=True)
        p = jnp.exp2(s - m)
        l = jnp.sum(p, axis=-1, keepdims=True)
        p = p * jnp.where(m > 0.5 * NEG, 1.0 / l, 0.0)
        oc_ref[qb * r:qb * (r + 1), :] = _dot(p, vc)[:, 0:HEAD_DIM]
        psum = psum + p
    p_hi = psum.astype(BF16)
    p_lo = (psum - p_hi.astype(F32)).astype(BF16)
    ov = ov_ref[...]
    imp = (jnp.dot(p_hi, ov, preferred_element_type=F32)
           + jnp.dot(p_lo, ov, preferred_element_type=F32))

    nsel = LANES
    sp = lax.broadcasted_iota(jnp.int32, (qb, nsel), 1)
    rr = lax.broadcasted_iota(jnp.int32, (qb, nsel), 0)
    cur = (nsel - 2) + (rr >= SLC_BLOCK).astype(jnp.int32)
    s_abs = sp + (2 * i + 2 - nsel)
    valid = s_abs >= 0
    forced = ((s_abs == 0) | (sp == cur) | (sp == cur - 1)) & valid
    excluded = forced | (sp > cur) | (~valid)
    cand_t = jnp.where(excluded, NEG, imp).T
    v = cand_t
    tau = None
    n_rank = SLC_TOPN - N_FORCED
    p_w = []
    for it in range(n_rank):
        tau = jnp.max(v, axis=0, keepdims=True)
        v = jnp.where(v >= tau, 3.0 * NEG, v)
        if it % (n_rank // NSA_REP) == 0 and len(p_w) < NSA_REP:
            p_w.append(window_probs(len(p_w)))
    accw_ref[...] = _dot(jnp.concatenate(p_w, axis=0), vw_ref[0, 0, pl.ds(wstart, WIN_KEYS), :])
    picked = jnp.where((cand_t >= tau) & (cand_t > 0.5 * NEG), 1.0, 0.0).T
    sel_neg = jnp.where((picked > 0.5) | forced, 0.0, NEG)
    sel_neg = pltpu.roll(sel_neg, shift=(2 * i + 2) % nsel, axis=1).astype(qa_ref.dtype)
    for r in range(NSA_REP):
        qa_ref[qb * r:qb * (r + 1), LANES:2 * LANES] = sel_neg
    q_ext = qa_ref[...]

    def tile_scores(dd):
        return _dot_nt(q_ext, ks_ref[0, 0, pl.ds(tile_start(dd), kt), :])

    def tile_softmax(s, dd):
        m_t = jnp.max(s, axis=-1, keepdims=True)
        return m_t, _dot(jnp.exp2(s - m_t), vs_ref[0, 0, pl.ds(tile_start(dd), kt), :])

    def merge(parts):
        m_new = functools.reduce(jnp.maximum, [m for m, _ in parts])
        acc = sum(jnp.exp2(m - m_new) * pv for m, pv in parts)
        return m_new, acc

    near = []
    for dd in range(NEAR_TILES):
        msk = _dot_nt(sel_neg, ks_ref[0, 0, pl.ds(tile_start(dd), kt), LANES:2 * LANES])
        near.append(jnp.concatenate([near_base[dd][qb * r:qb * (r + 1)] + (tbm_ref[0, r, dd] + msk)
                                     for r in range(NSA_REP)], axis=0))
    carry = merge([tile_softmax(s, dd) for dd, s in enumerate(near)])

    def far_group(first, width, carry):
        dds = [jnp.minimum(first + k, n_tiles) for k in range(width)]
        scores = [tile_scores(dd) for dd in dds]
        return merge([carry] + [tile_softmax(s, dd) for s, dd in zip(scores, dds)])

    first = NEAR_TILES
    left = jnp.maximum(n_tiles - NEAR_TILES, 0)
    for width in FAR_GROUPS:
        trips = (left + width - 1) // width if width == FAR_GROUPS[-1] else left // width
        carry = lax.fori_loop(
            0, trips, lambda u, c, first=first, width=width: far_group(first + width * u, width, c), carry)
        first = first + width * trips
        left = jnp.maximum(left - width * trips, 0)
    accs_ref[...] = carry[1]


def _nsa_attention(qn, gate, kcv, ksa, vsp, kwa, vwp, tbm, wbm, cbm, qmid, ov):
    b_, t_, _ = qn.shape
    tp = ksa.shape[2]
    ncp = kcv.shape[3]
    gw = NSA_REP * HEAD_DIM
    assert t_ // SLC_BLOCK <= LANES
    kvspec = lambda wd: pl.BlockSpec((1, 1, tp, wd), lambda b, g, i: (b, g, 0, 0))
    nq = t_ // Q_BLOCK
    behind = lambda b, g, i: (b, jnp.maximum(i - 1, 0), g)
    return pl.pallas_call(
        _nsa_kernel,
        grid=(b_, NSA_KV, nq + 1),
        in_specs=[
            pl.BlockSpec((1, Q_BLOCK, gw), lambda b, g, i: (b, jnp.minimum(i, nq - 1), g)),
            pl.BlockSpec((1, Q_BLOCK, LANES), behind),
            pl.BlockSpec((1, 1, 1, ncp, LANES), lambda b, g, i: (b, 0, g, 0, 0)),
            pl.BlockSpec((1, 1, 1, ncp, LANES), lambda b, g, i: (b, 1, g, 0, 0)),
            kvspec(2 * LANES), kvspec(LANES), kvspec(LANES), kvspec(LANES),
            pl.BlockSpec((1, NSA_REP, NEAR_TILES, Q_BLOCK, SEL_TILE), lambda b, g, i: (g, 0, 0, 0, 0)),
            pl.BlockSpec((1, NSA_REP, Q_BLOCK, WIN_KEYS), lambda b, g, i: (g, 0, 0, 0)),
            pl.BlockSpec((1, NSA_REP, Q_BLOCK, CMP_PAD), lambda b, g, i: (g, 0, 0, 0)),
            pl.BlockSpec((1, NSA_REP, 1, LANES), lambda b, g, i: (g, 0, 0, 0)),
            pl.BlockSpec((CMP_PAD, LANES), lambda b, g, i: (0, 0)),
        ],
        out_specs=pl.BlockSpec((1, Q_BLOCK, gw), behind),
        out_shape=jax.ShapeDtypeStruct((b_, t_, D_NSA), F32),
        scratch_shapes=[
            pltpu.VMEM((NSA_REP * Q_BLOCK, 2 * LANES), MXU_DTYPE),
            pltpu.VMEM((NSA_REP * Q_BLOCK, LANES), F32),
            pltpu.VMEM((NSA_REP * Q_BLOCK, LANES), F32),
            pltpu.VMEM((NSA_REP * Q_BLOCK, HEAD_DIM), F32),
        ],
        compiler_params=_params(("parallel", "parallel", "arbitrary")),
        name="nsa_attention",
    )(qn, gate, kcv, kcv, ksa, vsp, kwa, vwp, tbm, wbm, cbm, qmid, ov)


def _mix_kernel(x_ref, on_ref, gu_ref, vn_ref, sgw_ref, sgb_ref, gn_ref, gs_ref, wo_ref,
                fg_ref, wq_ref, x1_out, h2t_out, qp_out):
    tm = x_ref.shape[0]
    tri = (lax.broadcasted_iota(jnp.int32, (SG_CHUNK, SG_CHUNK), 0)
           >= lax.broadcasted_iota(jnp.int32, (SG_CHUNK, SG_CHUNK), 1))
    sgb = sgb_ref[...]
    y = x_ref[...] + _dot(_rms(on_ref[...], gn_ref[...]), wo_ref[0:D_NSA, :])
    parts = []
    for c in range(tm // SG_CHUNK):
        cs = slice(SG_CHUNK * c, SG_CHUNK * (c + 1))
        heads = []
        for hh in range(SG_HEADS):
            sl = slice(SG_DIM * hh, SG_DIM * (hh + 1))
            w = jnp.where(tri, sgw_ref[hh], 0.0)
            mixed = _dot(w, vn_ref[cs, sl]) + sgb[:, hh:hh + 1]
            heads.append(gu_ref[cs, sl].astype(F32) * mixed)
        parts.append(jnp.concatenate(heads, axis=1))
    o_sg = jnp.concatenate(parts, axis=0)
    y = y + _dot(_rms(o_sg, gs_ref[...]), wo_ref[D_NSA:D_NSA + D_SG, :])
    x1_out[...] = y
    h2 = _rms(y, fg_ref[...])
    h2t_out[...] = h2.T.astype(h2t_out.dtype)
    qp_out[...] = _dot(h2, wq_ref[...])


def _mix(x2, onsa, gu, vn, sgw, sgb_t, gn, gs, wo, fg, wq, tm):
    n_, d_ = x2.shape
    row = lambda wd: pl.BlockSpec((tm, wd), lambda i: (i, 0))
    const = lambda shape, **kw: pl.BlockSpec(shape, lambda i: (0,) * len(shape), **kw)
    return pl.pallas_call(
        _mix_kernel,
        grid=(n_ // tm,),
        in_specs=[
            row(d_), row(D_NSA), row(D_SG), row(D_SG),
            const((SG_HEADS, SG_CHUNK, SG_CHUNK)), const((SG_CHUNK, SG_HEADS)),
            const((1, D_NSA)), const((1, D_SG)),
            const((D_NSA + D_SG, d_), pipeline_mode=pl.Buffered(1)),
            const((1, d_)),
            const((d_, wq.shape[1]), pipeline_mode=pl.Buffered(1)),
        ],
        out_specs=[row(d_), pl.BlockSpec((d_, tm), lambda i: (0, i)), row(wq.shape[1])],
        out_shape=[
            jax.ShapeDtypeStruct((n_, d_), F32),
            jax.ShapeDtypeStruct((d_, n_), MXU_DTYPE),
            jax.ShapeDtypeStruct((n_, wq.shape[1]), F32),
        ],
        compiler_params=_params(("parallel",)),
        name="mix_out_proj",
    )(x2, onsa, gu, vn, sgw, sgb_t, gn, gs, wo, fg, wq)


def _top_rows(v, k):
    tops = []
    for _ in range(k):
        m = jnp.max(v, axis=0, keepdims=True)
        tops.append(m)
        v = jnp.where(v >= m, 0.0, v)
    return jnp.concatenate(tops, axis=0)


def _pair_products(a, b):
    cand = [a[0:1] * b]
    for ra in range(1, 8):
        cand.append(a[ra:ra + 1] * b[0:8])
    cand.append(a[8:16] * b[0:1])
    return jnp.concatenate(cand, axis=0)


def _peer_topk_kernel(qp_ref, sk_ref, e1_out, e2_out, eps_out):
    for hh in range(PEER_HEADS):
        base = 2 * PEER_HALF * hh
        s1 = _dot_nt(sk_ref[0], qp_ref[:, base:base + PEER_HALF])
        s2 = _dot_nt(sk_ref[1], qp_ref[:, base + PEER_HALF:base + 2 * PEER_HALF])
        e1 = jnp.exp(s1 - jnp.max(s1, axis=0, keepdims=True))
        e2 = jnp.exp(s2 - jnp.max(s2, axis=0, keepdims=True))
        a = _top_rows(e1, PEER_TOPK)
        b = _top_rows(e2, PEER_TOPK)
        z = jnp.sum(_top_rows(_pair_products(a, b), PEER_TOPK), axis=0, keepdims=True)
        zinv = 1.0 / z
        gates = _top_rows(_pair_products(a * zinv, b), PEER_TOPK)
        e1_out[hh] = e1 * zinv
        for j in range(e2.shape[1] // LANES):
            e2_out[hh, j] = e2[:, LANES * j:LANES * (j + 1)]
        eps_out[hh:hh + 1, :] = gates[PEER_TOPK - 1:PEER_TOPK]


def _peer_topk(qp, sub_keys, tm):
    n_, qd = qp.shape
    return pl.pallas_call(
        _peer_topk_kernel,
        grid=(n_ // tm,),
        in_specs=[
            pl.BlockSpec((tm, qd), lambda i: (i, 0)),
            pl.BlockSpec((2, PEER_NKEYS, PEER_HALF), lambda i: (0, 0, 0)),
        ],
        out_specs=[
            pl.BlockSpec((PEER_HEADS, PEER_NKEYS, tm), lambda i: (0, 0, i)),
            pl.BlockSpec((PEER_HEADS, tm // LANES, PEER_NKEYS, LANES), lambda i: (0, i, 0, 0)),
            pl.BlockSpec((PEER_HEADS, tm), lambda i: (0, i)),
        ],
        out_shape=[
            jax.ShapeDtypeStruct((PEER_HEADS, PEER_NKEYS, n_), F32),
            jax.ShapeDtypeStruct((PEER_HEADS, n_ // LANES, PEER_NKEYS, LANES), F32),
            jax.ShapeDtypeStruct((PEER_HEADS, n_), F32),
        ],
        compiler_params=_params(("parallel",)),
        name="peer_topk",
    )(qp, sub_keys)


def _peer_mix_kernel(n_tiles, ht_ref, u_ref, vt_ref, e1_ref, e2_ref, eps_ref, x1_ref, out_ref,
                     acc_ref, crow_ref, a0_ref, a1_ref, z0_ref, z1_ref):
    ie = pl.program_id(1)
    te = u_ref.shape[0]
    tm = ht_ref.shape[1]
    per_tile = te // PEER_NKEYS

    @pl.when(ie == 0)
    def _():
        acc_ref[...] = jnp.zeros_like(acc_ref)

    d_ = vt_ref.shape[1]
    n_lane = tm // LANES
    n_blocks = per_tile * n_lane
    a_units = [(r, c, kc) for r in range(te // MXU_ROWS) for c in range(tm // MXU_ROWS)
               for kc in range(d_ // A_KCHUNK)]
    c_units = [(r, c) for r in range(d_ // MXU_ROWS) for c in range(tm // MXU_ROWS)]

    def step(a_w, a_r, z_w, z_r):
        tile = ie - 1

        def a_unit(r, c, kc):
            rs = slice(MXU_ROWS * r, MXU_ROWS * (r + 1))
            cs = slice(MXU_ROWS * c, MXU_ROWS * (c + 1))
            ds = slice(A_KCHUNK * kc, A_KCHUNK * (kc + 1))
            res = jnp.dot(u_ref[rs, ds], ht_ref[ds, cs], preferred_element_type=F32)
            for jj in range(MXU_ROWS // LANES):
                j = c * (MXU_ROWS // LANES) + jj
                part = res[:, LANES * jj:LANES * (jj + 1)]
                if kc == 0:
                    a_w[j, rs, :] = part
                else:
                    a_w[j, rs, :] += part

        def c_unit(r, c):
            rs = slice(MXU_ROWS * r, MXU_ROWS * (r + 1))
            cs = slice(MXU_ROWS * c, MXU_ROWS * (c + 1))
            res = jnp.dot(vt_ref[0, rs, :], z_r[:, cs], preferred_element_type=F32)
            for jj in range(MXU_ROWS // LANES):
                acc_ref[c * (MXU_ROWS // LANES) + jj, rs, :] += res[:, LANES * jj:LANES * (jj + 1)]

        def b_block(k, j):
            ls = slice(LANES * j, LANES * (j + 1))
            for part in range(PEER_NKEYS // GATE_ROWS):
                ks = slice(GATE_ROWS * part, GATE_ROWS * (part + 1))
                rs = slice(PEER_NKEYS * k + GATE_ROWS * part, PEER_NKEYS * k + GATE_ROWS * (part + 1))
                g = jnp.zeros((GATE_ROWS, LANES), F32)
                for hh in range(PEER_HEADS):
                    row = PEER_HEADS * k + hh
                    gate = e2_ref[hh, j, ks, :] * crow_ref[row:row + 1, ls]
                    g = g + jnp.where(gate >= eps_ref[hh:hh + 1, ls], gate, 0.0)
                z_w[rs, ls] = (_gelu(a_r[j, rs, :]) * g).astype(z_w.dtype)

        do_a, do_b, do_c = a_w is not None, z_w is not None, z_r is not None
        if do_b:
            for k in range(per_tile):
                i1 = tile * per_tile + k
                for hh in range(PEER_HEADS):
                    row = PEER_HEADS * k + hh
                    crow_ref[row:row + 1, :] = e1_ref[hh, pl.ds(i1, 1), :]

        a_iter = iter(a_units if do_a else [])
        c_iter = iter(c_units if do_c else [])
        a_every = max(n_blocks // len(a_units), 1)
        c_per = -(-len(c_units) // n_blocks)
        for blk in range(n_blocks):
            k, j = divmod(blk, n_lane)
            if blk % a_every == 0:
                unit = next(a_iter, None)
                if unit is not None:
                    a_unit(*unit)
            for _ in range(c_per):
                unit = next(c_iter, None)
                if unit is not None:
                    c_unit(*unit)
            if do_b:
                b_block(k, j)
        for unit in a_iter:
            a_unit(*unit)
        for unit in c_iter:
            c_unit(*unit)

    a_bufs, z_bufs = (a0_ref, a1_ref), (z0_ref, z1_ref)

    def run(par, do_a=True, do_b=True, do_c=True):
        step(a_bufs[par] if do_a else None, a_bufs[1 - par], z_bufs[1 - par] if do_b else None,
             z_bufs[par] if do_c else None)

    last = n_tiles + 1
    pl.when(ie == 0)(lambda: run(0, do_b=False, do_c=False))
    pl.when(ie == 1)(lambda: run(1, do_c=False))
    pl.when((ie >= 2) & (ie < n_tiles) & (ie % 2 == 0))(lambda: run(0))
    pl.when((ie >= 2) & (ie < n_tiles) & (ie % 2 == 1))(lambda: run(1))
    pl.when(ie == n_tiles)(lambda: run(n_tiles % 2, do_a=False))
    pl.when(ie == last)(lambda: run(last % 2, do_a=False, do_b=False))

    @pl.when(ie == last)
    def _():
        for j in range(n_lane):
            ts = slice(LANES * j, LANES * (j + 1))
            out_ref[ts, :] = x1_ref[ts, :] + acc_ref[j].T


def _peer_mix(h2t, u, vt, e1t, e2t, eps, x1, tm, te):
    d_, n_ = h2t.shape
    n_tiles = u.shape[0] // te
    assert n_tiles >= 3
    return pl.pallas_call(
        functools.partial(_peer_mix_kernel, n_tiles),
        grid=(n_ // tm, n_tiles + 2),
        in_specs=[
            pl.BlockSpec((d_, tm), lambda it, ie: (0, it)),
            pl.BlockSpec((te, d_), lambda it, ie: (jnp.minimum(ie, n_tiles - 1), 0)),
            pl.BlockSpec((1, d_, te), lambda it, ie: (jnp.maximum(ie - 2, 0), 0, 0)),
            pl.BlockSpec((PEER_HEADS, PEER_NKEYS, tm), lambda it, ie: (0, 0, it)),
            pl.BlockSpec((PEER_HEADS, tm // LANES, PEER_NKEYS, LANES), lambda it, ie: (0, it, 0, 0)),
            pl.BlockSpec((PEER_HEADS, tm), lambda it, ie: (0, it)),
            pl.BlockSpec((tm, d_), lambda it, ie: (it, 0)),
        ],
        out_specs=pl.BlockSpec((tm, d_), lambda it, ie: (it, 0)),
        out_shape=jax.ShapeDtypeStruct((n_, d_), F32),
        scratch_shapes=[
            pltpu.VMEM((tm // LANES, d_, LANES), F32),
            pltpu.VMEM((PEER_HEADS * (te // PEER_NKEYS), tm), F32),
            pltpu.VMEM((tm // LANES, te, LANES), F32), pltpu.VMEM((tm // LANES, te, LANES), F32),
            pltpu.VMEM((te, tm), MXU_DTYPE), pltpu.VMEM((te, tm), MXU_DTYPE),
        ],
        compiler_params=_params(("parallel", "arbitrary")),
        name="peer_mix",
    )(h2t, u, vt, e1t, e2t, eps, x1)


def _rel_bucket(dist):
    n = jnp.maximum(dist, 0)
    max_exact = REL_BUCKETS // 2
    nf = jnp.maximum(n, 1).astype(F32)
    large = max_exact + (jnp.log(nf / max_exact) / math.log(REL_MAX_DIST / max_exact)
                         * (REL_BUCKETS - max_exact)).astype(jnp.int32)
    large = jnp.minimum(large, REL_BUCKETS - 1)
    return jnp.where(n < max_exact, n, large)


def _toeplitz(btab, off, rows, cols):
    per = rows + cols
    k = np.concatenate([np.arange(cols), np.zeros(1, np.int64), np.arange(-(rows - 1), 0)])
    idx = np.clip(off - k, 0, btab.shape[1] - 1)
    w = jnp.take(btab, jnp.asarray(idx, jnp.int32), axis=1)
    x = jnp.tile(w, (1, rows))[:, :rows * (per - 1)]
    return x.reshape(btab.shape[0], rows, per - 1)[:, :, :cols]


def _bias_tiles(rel_bias):
    assert SEL_TILE * NEAR_TILES - Q_BLOCK + 1 >= FAR_DIST
    assert CMP_STRIDE * (LANES + 1) - (CMP_BLOCK - 1) >= FAR_DIST
    dist = jnp.arange(BIAS_TABLE, dtype=jnp.int32)
    rel_bias = rel_bias.astype(F32) * LOG2E
    btab = rel_bias[_rel_bucket(dist)].T
    tbw = jnp.stack([_toeplitz(btab, SEL_TILE * (dd + 1) - Q_BLOCK, Q_BLOCK, SEL_TILE)
                     for dd in range(NEAR_TILES)], axis=1)
    wb = _toeplitz(btab, WINDOW, Q_BLOCK, WIN_KEYS)
    off = CMP_STRIDE * (LANES - 8) - (CMP_BLOCK - 1)
    front = CMP_STRIDE * LANES - off
    length = CMP_STRIDE * (LANES + 8 + 2)
    padded = jnp.pad(btab, ((0, 0), (front, length - front - BIAS_TABLE)))
    ch = padded.reshape(btab.shape[0], length // CMP_STRIDE, CMP_STRIDE)
    nwin = LANES + 1
    win = jnp.stack([ch[:, a:a + nwin] for a in range(Q_BLOCK // CMP_STRIDE)], axis=2)
    win = win.reshape(btab.shape[0], nwin, Q_BLOCK)
    cb = jnp.swapaxes(win[:, 1:nwin][:, ::-1], 1, 2)
    cfar = rel_bias.astype(F32)[REL_BUCKETS - 1]
    grp = lambda a: a.reshape((NSA_KV, NSA_REP) + a.shape[1:])

    r = np.arange(Q_BLOCK)[:, None]
    near_ok = np.stack([r - np.arange(SEL_TILE)[None, :] + SEL_TILE * (dd + 1) - Q_BLOCK >= 0
                        for dd in range(NEAR_TILES)])
    tbm = jnp.where(near_ok, tbw - cfar[:, None, None, None], NEG)
    dist_w = r - np.arange(WIN_KEYS)[None, :] + WINDOW
    wbm = jnp.where((dist_w >= 0) & (dist_w < WINDOW), wb, NEG)
    dist_c = r - CMP_STRIDE * np.arange(CMP_PAD)[None, :] + (CMP_STRIDE * CMP_PAD - Q_BLOCK - (CMP_BLOCK - 1))
    cbias = jnp.concatenate(
        [jnp.broadcast_to(cfar[:, None, None], (NSA_HEADS, Q_BLOCK, CMP_PAD - LANES)), cb], axis=2)
    cbm = jnp.where(dist_c >= 0, cbias, NEG)

    hi = cfar.astype(BF16).astype(F32)
    lo = (cfar - hi).astype(BF16).astype(F32)
    qmid = jnp.zeros((NSA_HEADS, 1, LANES), F32)
    qmid = qmid.at[:, 0, FAR_HI_LANE - HEAD_DIM].set(hi).at[:, 0, FAR_LO_LANE - HEAD_DIM].set(lo)
    qmid = qmid.at[:, 0, PAD_FLAG_LANE - HEAD_DIM].set(NEG)
    return grp(tbm), grp(wbm), grp(cbm), grp(qmid)


def _kv_layout_kernel(kv_ref, ts_ref, oh_ref, tw_ref, tv_ref, ksa_out, vsp_out, kwa_out, vwp_out):
    tm = kv_ref.shape[1]
    real = pl.program_id(1) >= KEY_PAD // tm
    blk = jnp.where(real, kv_ref[0], jnp.zeros_like(kv_ref[0]))
    for g in range(NSA_KV):
        for n, (out, tail) in enumerate(((ksa_out, ts_ref), (vsp_out, tv_ref),
                                         (kwa_out, tw_ref), (vwp_out, tv_ref))):
            lo = D_KV * n + HEAD_DIM * g
            out[0, g, :, 0:HEAD_DIM] = blk[:, lo:lo + HEAD_DIM]
            out[0, g, :, HEAD_DIM:LANES] = tail[...]
        ksa_out[0, g, :, LANES:2 * LANES] = oh_ref[...]


def _kv_layout(kv, tm):
    b_, t_, _ = kv.shape
    tp = t_ + KEY_PAD
    pos = np.arange(tp) - KEY_PAD
    real = pos >= 0
    tail = np.zeros((3, tp, LANES - HEAD_DIM), np.float32)
    tail[0, :, FAR_HI_LANE - HEAD_DIM] = real
    tail[0, :, FAR_LO_LANE - HEAD_DIM] = real
    tail[0, :, PAD_FLAG_LANE - HEAD_DIM] = ~real
    tail[1, :, PAD_FLAG_LANE - HEAD_DIM] = ~real
    tail[2, :, 0] = real
    onehot = ((pos[:, None] // SLC_BLOCK) == np.arange(LANES)[None, :]) & real[:, None]
    const = lambda a: jnp.asarray(a, kv.dtype)
    rows = lambda wd: pl.BlockSpec((tm, wd), lambda b, i: (i, 0))
    heads = lambda wd: pl.BlockSpec((1, NSA_KV, tm, wd), lambda b, i: (b, 0, i, 0))
    shape = lambda wd: jax.ShapeDtypeStruct((b_, NSA_KV, tp, wd), kv.dtype)
    return pl.pallas_call(
        _kv_layout_kernel,
        grid=(b_, tp // tm),
        in_specs=[
            pl.BlockSpec((1, tm, 4 * D_KV), lambda b, i: (b, jnp.maximum(i - KEY_PAD // tm, 0), 0)),
            rows(LANES - HEAD_DIM), rows(LANES), rows(LANES - HEAD_DIM), rows(LANES - HEAD_DIM),
        ],
        out_specs=[heads(2 * LANES), heads(LANES), heads(LANES), heads(LANES)],
        out_shape=[shape(2 * LANES), shape(LANES), shape(LANES), shape(LANES)],
        compiler_params=_params(("parallel", "parallel")),
        name="kv_layout",
    )(kv, const(tail[0]), const(onehot.astype(np.float32)), const(tail[1]), const(tail[2]))


def _overlap_matrix():
    c = np.arange(CMP_PAD)[:, None]
    s = np.arange(LANES)[None, :]
    per = SLC_BLOCK // CMP_STRIDE
    ov = (s == c // per) | ((c % per == per - 1) & (s == c // per + 1))
    return jnp.asarray(ov.astype(np.float32), BF16)


def kernel(x, attn_norm_g, w_in, q_norm_g, k_norm_g, cmp_pe_k, cmp_w1_k, cmp_w2_k, cmp_pe_v,
           cmp_w1_v, cmp_w2_v, rel_bias, sg_ln_g, sg_ln_b, sg_w, sg_b, out_norm_nsa, out_norm_sg,
           w_out, ffn_norm_g, peer_w_query, peer_sub_keys, peer_u, peer_v):
    b_, t_, d_ = x.shape
    n_ = b_ * t_
    assert t_ % Q_BLOCK == 0 and t_ >= WIN_KEYS and d_ == D_NSA + D_SG

    c_kv = D_NSA + 6 * D_KV
    c_gate = c_kv + 3 * NSA_HEADS
    wg = w_in[:, c_kv:c_gate].reshape(d_, NSA_KV, 3 * NSA_REP)
    wg = jnp.pad(wg, ((0, 0), (0, 0), (0, LANES - 3 * NSA_REP))).reshape(d_, NSA_KV * LANES)
    w_attn = jnp.concatenate([w_in[:, :c_kv], wg], axis=1).astype(MXU_DTYPE)
    w_sgu = w_in[:, c_gate:].astype(MXU_DTYPE)
    ng = attn_norm_g.reshape(1, d_)

    gq = jnp.tile(q_norm_g, NSA_HEADS).reshape(1, D_NSA) * (HEAD_DIM ** -0.5 * LOG2E)
    no_gain = jnp.zeros((D_KV,), F32)
    gkv = jnp.concatenate([jnp.tile(k_norm_g[1], NSA_KV), no_gain,
                           jnp.tile(k_norm_g[2], NSA_KV), no_gain]).reshape(1, 4 * D_KV)
    qn, kcv, kv, gate = _attn_proj(x, ng, w_attn, gq, gkv, tm=256)
    ksa, vsp, kwa, vwp = _kv_layout(kv, tm=512)
    gu, vn = _sgu_proj(x, ng, w_sgu, sg_ln_g, sg_ln_b, tm=256)

    nch = t_ // CMP_STRIDE
    xc = kcv.reshape(b_, nch, CMP_STRIDE, 2, NSA_KV, HEAD_DIM).transpose(0, 3, 4, 1, 2, 5)
    xc = xc.reshape(b_, 2, NSA_KV, nch, CMP_STRIDE * HEAD_DIM)
    w1 = jnp.stack([cmp_w1_k, cmp_w1_v]).astype(MXU_DTYPE)
    w2 = jnp.stack([cmp_w2_k, cmp_w2_v]).astype(MXU_DTYPE)
    pe = jnp.stack([cmp_pe_k, cmp_pe_v]).transpose(0, 2, 1, 3).reshape(2, NSA_KV, 1, CMP_BLOCK * HEAD_DIM)
    kcv_c = _compress(xc, w1, w2, pe.astype(MXU_DTYPE), k_norm_g)

    tbm, wbm, cbm, qmid = _bias_tiles(rel_bias)
    onsa = _nsa_attention(qn, gate, kcv_c, ksa, vsp, kwa, vwp, tbm, wbm, cbm, qmid, _overlap_matrix())

    x1, h2t, qp = _mix(x.reshape(n_, d_), onsa.reshape(n_, D_NSA), gu.reshape(n_, D_SG),
                      vn.reshape(n_, D_SG), sg_w, sg_b.T, out_norm_nsa.reshape(1, D_NSA),
                      out_norm_sg.reshape(1, D_SG), w_out.astype(MXU_DTYPE),
                      ffn_norm_g.reshape(1, d_), peer_w_query.astype(MXU_DTYPE), tm=256)

    e1t, e2t, eps = _peer_topk(qp, peer_sub_keys, tm=256)
    te = 512
    vt = peer_v.reshape(-1, te, d_).transpose(0, 2, 1).astype(MXU_DTYPE)
    out = _peer_mix(h2t, peer_u.astype(MXU_DTYPE), vt, e1t, e2t, eps, x1, tm=min(512, n_), te=te)
    return out.reshape(b_, t_, d_)
```

```python
import functools
import math

import jax
import jax.numpy as jnp
import numpy as np
from jax import lax
from jax.experimental import pallas as pl
from jax.experimental.pallas import tpu as pltpu

F32 = jnp.float32
BF16 = jnp.bfloat16
MXU_DTYPE = BF16

EPS = 1e-6
NEG = -1e30
SQRT_HALF = 0.7071067811865476
LOG2E = 1.4426950408889634

NSA_HEADS = 16
NSA_KV = 4
NSA_REP = 4
HEAD_DIM = 64
D_NSA = NSA_HEADS * HEAD_DIM
D_KV = NSA_KV * HEAD_DIM
CMP_STRIDE = 16
CMP_BLOCK = 32
CMP_HIDDEN = 128
SLC_BLOCK = 64
SLC_TOPN = 16
WINDOW = 512
Q_BLOCK = 128
SG_HEADS = 8
SG_DIM = 128
SG_CHUNK = 128
D_SG = SG_HEADS * SG_DIM
REL_BUCKETS = 32
REL_MAX_DIST = 1024
PEER_HEADS = 8
PEER_NKEYS = 128
PEER_HALF = 128
PEER_TOPK = 16

LANES = 128
MXU_ROWS = 256
A_KCHUNK = 1024
GATE_ROWS = 64
KEY_PAD = 1024
CMP_PAD = 512
SEL_TILE = 512
WIN_KEYS = WINDOW + Q_BLOCK
NEAR_TILES = 2
FAR_GROUPS = (8, 4, 2)
BIAS_TABLE = 2048
FAR_DIST = 897
FAR_HI_LANE = 64
FAR_LO_LANE = 65
PAD_FLAG_LANE = 66
N_FORCED = 3
VMEM_LIMIT = 56 * 1024 * 1024


def _gelu(x):
    return 0.5 * x * (1.0 + lax.erf(x * SQRT_HALF))


def _dot(a, b):
    return jnp.dot(a.astype(MXU_DTYPE), b.astype(MXU_DTYPE), preferred_element_type=F32)


def _dot_nt(a, b):
    return lax.dot_general(a.astype(MXU_DTYPE), b.astype(MXU_DTYPE),
                           (((1,), (1,)), ((), ())), preferred_element_type=F32)


def _rms(x, g):
    ms = jnp.mean(x * x, axis=-1, keepdims=True)
    return x * lax.rsqrt(ms + EPS) * g


def _params(sem):
    return pltpu.CompilerParams(dimension_semantics=sem, vmem_limit_bytes=VMEM_LIMIT)


def _attn_proj_kernel(x_ref, ng_ref, w_ref, gq_ref, gkv_ref, hs_ref, he_ref,
                      q_out, kcv_out, kv_out, gate_out):
    h = _rms(x_ref[0], ng_ref[...]).astype(MXU_DTYPE)

    def head_rms(p):
        ms = jnp.dot((p * p).astype(MXU_DTYPE), hs_ref[...], preferred_element_type=F32)
        r = lax.rsqrt(ms + EPS)
        r_hi = r.astype(BF16)
        r_lo = (r - r_hi.astype(F32)).astype(BF16)
        he = he_ref[...]
        return p * (jnp.dot(r_hi, he, preferred_element_type=F32)
                    + jnp.dot(r_lo, he, preferred_element_type=F32))

    pq = _dot(h, w_ref[:, 0:D_NSA])
    q_out[0] = (head_rms(pq) * gq_ref[...]).astype(q_out.dtype)

    kcv_out[0] = _dot(h, w_ref[:, D_NSA:D_NSA + 2 * D_KV]).astype(kcv_out.dtype)

    base = D_NSA + 2 * D_KV
    p4 = _dot(h, w_ref[:, base:base + 4 * D_KV])
    col = lax.broadcasted_iota(jnp.int32, (1, 4 * D_KV), 1)
    is_key = (col // D_KV) % 2 == 0
    kv_out[0] = jnp.where(is_key, head_rms(p4) * gkv_ref[...], p4).astype(kv_out.dtype)

    base = base + 4 * D_KV
    gate_out[0] = jax.nn.sigmoid(_dot(h, w_ref[:, base:base + NSA_KV * LANES]))


def _attn_proj(x, ng, w, gq, gkv, tm):
    b_, t_, d_ = x.shape
    nw = w.shape[1]
    heads = np.arange(D_NSA) // HEAD_DIM
    hs = (heads[:, None] == np.arange(LANES)[None, :]).astype(np.float32)
    const = lambda shape: pl.BlockSpec(shape, lambda b, i: (0,) * len(shape))
    return pl.pallas_call(
        _attn_proj_kernel,
        grid=(b_, t_ // tm),
        in_specs=[
            pl.BlockSpec((1, tm, d_), lambda b, i: (b, i, 0)),
            const((1, d_)),
            pl.BlockSpec((d_, nw), lambda b, i: (0, 0), pipeline_mode=pl.Buffered(1)),
            const((1, D_NSA)), const((1, 4 * D_KV)),
            const((D_NSA, LANES)), const((LANES, D_NSA)),
        ],
        out_specs=[
            pl.BlockSpec((1, tm, D_NSA), lambda b, i: (b, i, 0)),
            pl.BlockSpec((1, tm, 2 * D_KV), lambda b, i: (b, i, 0)),
            pl.BlockSpec((1, tm, 4 * D_KV), lambda b, i: (b, i, 0)),
            pl.BlockSpec((1, tm, NSA_KV * LANES), lambda b, i: (b, i, 0)),
        ],
        out_shape=[
            jax.ShapeDtypeStruct((b_, t_, D_NSA), MXU_DTYPE),
            jax.ShapeDtypeStruct((b_, t_, 2 * D_KV), MXU_DTYPE),
            jax.ShapeDtypeStruct((b_, t_, 4 * D_KV), MXU_DTYPE),
            jax.ShapeDtypeStruct((b_, t_, NSA_KV * LANES), F32),
        ],
        compiler_params=_params(("parallel", "parallel")),
        name="attn_proj",
    )(x, ng, w, gq, gkv, jnp.asarray(hs / HEAD_DIM, MXU_DTYPE), jnp.asarray(hs.T, BF16))


def _sgu_proj_kernel(x_ref, ng_ref, w_ref, lng_ref, lnb_ref, gu_out, vn_out):
    h = _rms(x_ref[0], ng_ref[...]).astype(MXU_DTYPE)
    gu_out[0] = _gelu(_dot(h, w_ref[:, 0:D_SG])).astype(gu_out.dtype)
    gv = _gelu(_dot(h, w_ref[:, D_SG:2 * D_SG]))
    for hh in range(SG_HEADS):
        sl = slice(SG_DIM * hh, SG_DIM * (hh + 1))
        v = gv[:, sl]
        mu = jnp.mean(v, axis=-1, keepdims=True)
        var = jnp.mean(jnp.square(v - mu), axis=-1, keepdims=True)
        y = (v - mu) * lax.rsqrt(var + EPS) * lng_ref[hh:hh + 1, :] + lnb_ref[hh:hh + 1, :]
        vn_out[0, :, sl] = y.astype(vn_out.dtype)


def _sgu_proj(x, ng, w, lng, lnb, tm):
    b_, t_, d_ = x.shape
    return pl.pallas_call(
        _sgu_proj_kernel,
        grid=(b_, t_ // tm),
        in_specs=[
            pl.BlockSpec((1, tm, d_), lambda b, i: (b, i, 0)),
            pl.BlockSpec((1, d_), lambda b, i: (0, 0)),
            pl.BlockSpec((d_, 2 * D_SG), lambda b, i: (0, 0), pipeline_mode=pl.Buffered(1)),
            pl.BlockSpec((SG_HEADS, SG_DIM), lambda b, i: (0, 0)),
            pl.BlockSpec((SG_HEADS, SG_DIM), lambda b, i: (0, 0)),
        ],
        out_specs=[
            pl.BlockSpec((1, tm, D_SG), lambda b, i: (b, i, 0)),
            pl.BlockSpec((1, tm, D_SG), lambda b, i: (b, i, 0)),
        ],
        out_shape=[
            jax.ShapeDtypeStruct((b_, t_, D_SG), MXU_DTYPE),
            jax.ShapeDtypeStruct((b_, t_, D_SG), MXU_DTYPE),
        ],
        compiler_params=_params(("parallel", "parallel")),
        name="sgu_proj",
    )(x, ng, w, lng, lnb)


def _compress_kernel(x_ref, w1_ref, w2_ref, pe_ref, kg_ref, out_ref):
    which = pl.program_id(1)
    x = x_ref[0, 0, 0]
    nch = x.shape[0]
    half = CMP_STRIDE * HEAD_DIM
    a = _dot(x, w1_ref[0, 0, 0:half, :])
    bm = _dot(x, w1_ref[0, 0, half:2 * half, :])
    pec = _dot(pe_ref[0, 0], w1_ref[0, 0])
    hid = _gelu(a + pltpu.roll(bm, shift=nch - 1, axis=0) + pec)
    y = _dot(hid, w2_ref[0, 0])
    y = jnp.where(which == 0, _rms(y, kg_ref[0:1, :]), y)
    lane = lax.broadcasted_iota(jnp.int32, (CMP_PAD, LANES), 1)
    flag = ((lane == PAD_FLAG_LANE) & (which == 0)).astype(out_ref.dtype)
    out_ref[0, 0, 0, 0:CMP_PAD, :] = flag
    out_ref[0, 0, 0, CMP_PAD:CMP_PAD + nch, 0:HEAD_DIM] = y.astype(out_ref.dtype)
    out_ref[0, 0, 0, CMP_PAD:CMP_PAD + nch, HEAD_DIM:LANES] = jnp.zeros((nch, LANES - HEAD_DIM), out_ref.dtype)


def _compress(xc, w1, w2, pe, kg):
    b_, _, g_, nch, fl = xc.shape
    return pl.pallas_call(
        _compress_kernel,
        grid=(b_, 2, g_),
        in_specs=[
            pl.BlockSpec((1, 1, 1, nch, fl), lambda b, w, g: (b, w, g, 0, 0)),
            pl.BlockSpec((1, 1, 2 * fl, CMP_HIDDEN), lambda b, w, g: (w, g, 0, 0)),
            pl.BlockSpec((1, 1, CMP_HIDDEN, HEAD_DIM), lambda b, w, g: (w, g, 0, 0)),
            pl.BlockSpec((1, 1, 1, 2 * fl), lambda b, w, g: (w, g, 0, 0)),
            pl.BlockSpec((3, HEAD_DIM), lambda b, w, g: (0, 0)),
        ],
        out_specs=pl.BlockSpec((1, 1, 1, CMP_PAD + nch, LANES), lambda b, w, g: (b, w, g, 0, 0)),
        out_shape=jax.ShapeDtypeStruct((b_, 2, g_, CMP_PAD + nch, LANES), F32),
        compiler_params=_params(("parallel", "parallel", "parallel")),
        name="nsa_compress",
    )(xc, w1, w2, pe, kg)


def _nsa_kernel(*refs):
    accs_ref, accw_ref, oc_ref = refs[-3:]
    step, last = pl.program_id(2), pl.num_programs(2) - 1

    @pl.when((pl.program_id(0) == 0) & (pl.program_id(1) == 0) & (step == 0))
    def _():
        accs_ref[...] = jnp.ones_like(accs_ref)
        accw_ref[...] = jnp.ones_like(accw_ref)
        oc_ref[...] = jnp.zeros_like(oc_ref)

    @pl.when(step < last)
    def _():
        _nsa_finish_previous(*refs)
        _nsa_attend(*refs)

    @pl.when(step == last)
    def _():
        _nsa_finish_previous(*refs)


def _nsa_finish_previous(q_ref, gate_ref, kc_ref, vc_ref, ks_ref, vs_ref, kw_ref, vw_ref,
                         tbm_ref, wbm_ref, cbm_ref, qmid_ref, ov_ref, o_ref, qa_ref,
                         accs_ref, accw_ref, oc_ref):
    qb = Q_BLOCK
    gt = gate_ref[0]
    for r in range(NSA_REP):
        rs = slice(qb * r, qb * (r + 1))
        o_s = accs_ref[rs, 0:HEAD_DIM] / accs_ref[rs, HEAD_DIM:HEAD_DIM + 1]
        o_w = accw_ref[rs, 0:HEAD_DIM] / accw_ref[rs, HEAD_DIM:HEAD_DIM + 1]
        o = (gt[:, 3 * r + 0:3 * r + 1] * oc_ref[rs, :] + gt[:, 3 * r + 1:3 * r + 2] * o_s
             + gt[:, 3 * r + 2:3 * r + 3] * o_w)
        o_ref[0, :, HEAD_DIM * r:HEAD_DIM * (r + 1)] = o.astype(o_ref.dtype)


def _nsa_attend(q_ref, gate_ref, kc_ref, vc_ref, ks_ref, vs_ref, kw_ref, vw_ref,
                tbm_ref, wbm_ref, cbm_ref, qmid_ref, ov_ref, o_ref, qa_ref,
                accs_ref, accw_ref, oc_ref):
    i = pl.program_id(2)
    qb = Q_BLOCK
    rows = NSA_REP * qb
    ncmp = CMP_PAD

    qblk = q_ref[0]
    for r in range(NSA_REP):
        rs = slice(qb * r, qb * (r + 1))
        qa_ref[rs, 0:HEAD_DIM] = qblk[:, HEAD_DIM * r:HEAD_DIM * (r + 1)]
        qa_ref[rs, HEAD_DIM:LANES] = jnp.broadcast_to(
            qmid_ref[0, r][:, 0:LANES - HEAD_DIM], (qb, LANES - HEAD_DIM)).astype(qa_ref.dtype)
    q_lo = qa_ref[:, 0:LANES]

    wstart = pl.multiple_of(KEY_PAD + qb * (i + 1) - WIN_KEYS, qb)
    s_w = _dot_nt(q_lo, kw_ref[0, 0, pl.ds(wstart, WIN_KEYS), :])

    def window_probs(r):
        s = s_w[qb * r:qb * (r + 1)] + wbm_ref[0, r]
        return jnp.exp2(s - jnp.max(s, axis=-1, keepdims=True))

    cstart = pl.multiple_of(8 * i + 8, 8)
    kc = kc_ref[0, 0, 0, pl.ds(cstart, ncmp), :]
    vc = vc_ref[0, 0, 0, pl.ds(cstart, ncmp), :]
    s_c = _dot_nt(q_lo, kc)

    kt = SEL_TILE
    n_tiles = (i * qb + qb + kt - 1) // kt

    def tile_start(dd):
        return pl.multiple_of(KEY_PAD + qb * (i + 1) - kt * (dd + 1), qb)

    near_base = [_dot_nt(q_lo, ks_ref[0, 0, pl.ds(tile_start(dd), kt), 0:LANES])
                 for dd in range(NEAR_TILES)]
    psum = jnp.zeros((qb, ncmp), F32)
    for r in range(NSA_REP):
        s = s_c[qb * r:qb * (r + 1)] + cbm_ref[0, r]
        m = jnp.max(s, axis=-1, keepdims=True)
        p = jnp.exp2(s - m)
        l = jnp.sum(p, axis=-1, keepdims=True)
        p = p * jnp.where(m > 0.5 * NEG, 1.0 / l, 0.0)
        oc_ref[qb * r:qb * (r + 1), :] = _dot(p, vc)[:, 0:HEAD_DIM]
        psum = psum + p
    p_hi = psum.astype(BF16)
    p_lo = (psum - p_hi.astype(F32)).astype(BF16)
    ov = ov_ref[...]
    imp = (jnp.dot(p_hi, ov, preferred_element_type=F32)
           + jnp.dot(p_lo, ov, preferred_element_type=F32))

    nsel = LANES
    sp = lax.broadcasted_iota(jnp.int32, (qb, nsel), 1)
    rr = lax.broadcasted_iota(jnp.int32, (qb, nsel), 0)
    cur = (nsel - 2) + (rr >= SLC_BLOCK).astype(jnp.int32)
    s_abs = sp + (2 * i + 2 - nsel)
    valid = s_abs >= 0
    forced = ((s_abs == 0) | (sp == cur) | (sp == cur - 1)) & valid
    excluded = forced | (sp > cur) | (~valid)
    cand_t = jnp.where(excluded, NEG, imp).T
    stack = _sort_stack([cand_t[8 * r:8 * (r + 1)] for r in range(nsel // 8)])
    tau = None
    n_rank = SLC_TOPN - N_FORCED
    p_w = []
    for it in range(n_rank):
        tau, stack = _pop_max(stack, min(len(stack), n_rank - it), 3.0 * NEG)
        if it % (n_rank // NSA_REP) == 0 and len(p_w) < NSA_REP:
            p_w.append(window_probs(len(p_w)))
    accw_ref[...] = _dot(jnp.concatenate(p_w, axis=0), vw_ref[0, 0, pl.ds(wstart, WIN_KEYS), :])
    picked = jnp.where((cand_t >= tau) & (cand_t > 0.5 * NEG), 1.0, 0.0).T
    sel_neg = jnp.where((picked > 0.5) | forced, 0.0, NEG)
    sel_neg = pltpu.roll(sel_neg, shift=(2 * i + 2) % nsel, axis=1).astype(qa_ref.dtype)
    for r in range(NSA_REP):
        qa_ref[qb * r:qb * (r + 1), LANES:2 * LANES] = sel_neg
    q_ext = qa_ref[...]

    def tile_scores(dd):
        return _dot_nt(q_ext, ks_ref[0, 0, pl.ds(tile_start(dd), kt), :])

    def tile_softmax(s, dd):
        m_t = jnp.max(s, axis=-1, keepdims=True)
        return m_t, _dot(jnp.exp2(s - m_t), vs_ref[0, 0, pl.ds(tile_start(dd), kt), :])

    def merge(parts):
        m_new = functools.reduce(jnp.maximum, [m for m, _ in parts])
        acc = sum(jnp.exp2(m - m_new) * pv for m, pv in parts)
        return m_new, acc

    near = []
    for dd in range(NEAR_TILES):
        msk = _dot_nt(sel_neg, ks_ref[0, 0, pl.ds(tile_start(dd), kt), LANES:2 * LANES])
        near.append(jnp.concatenate([near_base[dd][qb * r:qb * (r + 1)] + (tbm_ref[0, r, dd] + msk)
                                     for r in range(NSA_REP)], axis=0))
    carry = merge([tile_softmax(s, dd) for dd, s in enumerate(near)])

    def far_group(first, width, carry):
        dds = [jnp.minimum(first + k, n_tiles) for k in range(width)]
        scores = [tile_scores(dd) for dd in dds]
        return merge([carry] + [tile_softmax(s, dd) for s, dd in zip(scores, dds)])

    first = NEAR_TILES
    left = jnp.maximum(n_tiles - NEAR_TILES, 0)
    for width in FAR_GROUPS:
        trips = (left + width - 1) // width if width == FAR_GROUPS[-1] else left // width
        carry = lax.fori_loop(
            0, trips, lambda u, c, first=first, width=width: far_group(first + width * u, width, c), carry)
        first = first + width * trips
        left = jnp.maximum(left - width * trips, 0)
    accs_ref[...] = carry[1]


def _nsa_attention(qn, gate, kcv, ksa, vsp, kwa, vwp, tbm, wbm, cbm, qmid, ov):
    b_, t_, _ = qn.shape
    tp = ksa.shape[2]
    ncp = kcv.shape[3]
    gw = NSA_REP * HEAD_DIM
    assert t_ // SLC_BLOCK <= LANES
    kvspec = lambda wd: pl.BlockSpec((1, 1, tp, wd), lambda b, g, i: (b, g, 0, 0))
    nq = t_ // Q_BLOCK
    behind = lambda b, g, i: (b, jnp.maximum(i - 1, 0), g)
    return pl.pallas_call(
        _nsa_kernel,
        grid=(b_, NSA_KV, nq + 1),
        in_specs=[
            pl.BlockSpec((1, Q_BLOCK, gw), lambda b, g, i: (b, jnp.minimum(i, nq - 1), g)),
            pl.BlockSpec((1, Q_BLOCK, LANES), behind),
            pl.BlockSpec((1, 1, 1, ncp, LANES), lambda b, g, i: (b, 0, g, 0, 0)),
            pl.BlockSpec((1, 1, 1, ncp, LANES), lambda b, g, i: (b, 1, g, 0, 0)),
            kvspec(2 * LANES), kvspec(LANES), kvspec(LANES), kvspec(LANES),
            pl.BlockSpec((1, NSA_REP, NEAR_TILES, Q_BLOCK, SEL_TILE), lambda b, g, i: (g, 0, 0, 0, 0)),
            pl.BlockSpec((1, NSA_REP, Q_BLOCK, WIN_KEYS), lambda b, g, i: (g, 0, 0, 0)),
            pl.BlockSpec((1, NSA_REP, Q_BLOCK, CMP_PAD), lambda b, g, i: (g, 0, 0, 0)),
            pl.BlockSpec((1, NSA_REP, 1, LANES), lambda b, g, i: (g, 0, 0, 0)),
            pl.BlockSpec((CMP_PAD, LANES), lambda b, g, i: (0, 0)),
        ],
        out_specs=pl.BlockSpec((1, Q_BLOCK, gw), behind),
        out_shape=jax.ShapeDtypeStruct((b_, t_, D_NSA), F32),
        scratch_shapes=[
            pltpu.VMEM((NSA_REP * Q_BLOCK, 2 * LANES), MXU_DTYPE),
            pltpu.VMEM((NSA_REP * Q_BLOCK, LANES), F32),
            pltpu.VMEM((NSA_REP * Q_BLOCK, LANES), F32),
            pltpu.VMEM((NSA_REP * Q_BLOCK, HEAD_DIM), F32),
        ],
        compiler_params=_params(("parallel", "parallel", "arbitrary")),
        name="nsa_attention",
    )(qn, gate, kcv, kcv, ksa, vsp, kwa, vwp, tbm, wbm, cbm, qmid, ov)


def _mix_kernel(x_ref, on_ref, gu_ref, vn_ref, sgw_ref, sgb_ref, gn_ref, gs_ref, wo_ref,
                fg_ref, wq_ref, x1_out, h2t_out, qp_out):
    tm = x_ref.shape[0]
    tri = (lax.broadcasted_iota(jnp.int32, (SG_CHUNK, SG_CHUNK), 0)
           >= lax.broadcasted_iota(jnp.int32, (SG_CHUNK, SG_CHUNK), 1))
    sgb = sgb_ref[...]
    y = x_ref[...] + _dot(_rms(on_ref[...], gn_ref[...]), wo_ref[0:D_NSA, :])
    parts = []
    for c in range(tm // SG_CHUNK):
        cs = slice(SG_CHUNK * c, SG_CHUNK * (c + 1))
        heads = []
        for hh in range(SG_HEADS):
            sl = slice(SG_DIM * hh, SG_DIM * (hh + 1))
            w = jnp.where(tri, sgw_ref[hh], 0.0)
            mixed = _dot(w, vn_ref[cs, sl]) + sgb[:, hh:hh + 1]
            heads.append(gu_ref[cs, sl].astype(F32) * mixed)
        parts.append(jnp.concatenate(heads, axis=1))
    o_sg = jnp.concatenate(parts, axis=0)
    y = y + _dot(_rms(o_sg, gs_ref[...]), wo_ref[D_NSA:D_NSA + D_SG, :])
    x1_out[...] = y
    h2 = _rms(y, fg_ref[...])
    h2t_out[...] = h2.T.astype(h2t_out.dtype)
    qp_out[...] = _dot(h2, wq_ref[...])


def _mix(x2, onsa, gu, vn, sgw, sgb_t, gn, gs, wo, fg, wq, tm):
    n_, d_ = x2.shape
    row = lambda wd: pl.BlockSpec((tm, wd), lambda i: (i, 0))
    const = lambda shape, **kw: pl.BlockSpec(shape, lambda i: (0,) * len(shape), **kw)
    return pl.pallas_call(
        _mix_kernel,
        grid=(n_ // tm,),
        in_specs=[
            row(d_), row(D_NSA), row(D_SG), row(D_SG),
            const((SG_HEADS, SG_CHUNK, SG_CHUNK)), const((SG_CHUNK, SG_HEADS)),
            const((1, D_NSA)), const((1, D_SG)),
            const((D_NSA + D_SG, d_), pipeline_mode=pl.Buffered(1)),
            const((1, d_)),
            const((d_, wq.shape[1]), pipeline_mode=pl.Buffered(1)),
        ],
        out_specs=[row(d_), pl.BlockSpec((d_, tm), lambda i: (0, i)), row(wq.shape[1])],
        out_shape=[
            jax.ShapeDtypeStruct((n_, d_), F32),
            jax.ShapeDtypeStruct((d_, n_), MXU_DTYPE),
            jax.ShapeDtypeStruct((n_, wq.shape[1]), F32),
        ],
        compiler_params=_params(("parallel",)),
        name="mix_out_proj",
    )(x2, onsa, gu, vn, sgw, sgb_t, gn, gs, wo, fg, wq)


def _sort_stack(stack):
    stack = list(stack)
    n = len(stack)
    k = 2
    while k <= n:
        j = k // 2
        while j > 0:
            for i in range(n):
                l = i ^ j
                if l > i:
                    hi, lo = jnp.maximum(stack[i], stack[l]), jnp.minimum(stack[i], stack[l])
                    stack[i], stack[l] = (hi, lo) if (i & k) == 0 else (lo, hi)
            j //= 2
        k *= 2
    return stack


def _insert_sorted(stack, x):
    out = []
    for s in stack:
        out.append(jnp.maximum(s, x))
        x = jnp.minimum(s, x)
    return out + [x]


def _pop_max(stack, depth, floor):
    m = jnp.max(stack[0], axis=0, keepdims=True)
    hit = stack[0] >= m
    popped = [jnp.where(hit, stack[d + 1], stack[d]) for d in range(depth - 1)]
    popped.append(jnp.where(hit, floor, stack[depth - 1]))
    return m, popped + list(stack[depth:])


def _pop_tops(stack, k):
    tops = []
    for t in range(k):
        m, stack = _pop_max(stack, min(len(stack), k - t), 0.0)
        tops.append(m)
    return jnp.concatenate(tops, axis=0)


def _top_rows(v, k):
    return _pop_tops(_sort_stack([v[8 * r:8 * (r + 1)] for r in range(v.shape[0] // 8)]), k)


def _top_pair_products(a, b, k):
    stack = [a[ra:ra + 1] * b[0:8] for ra in range(8)]
    stack = _insert_sorted(stack, a[0:1] * b[8:16])
    stack = _insert_sorted(stack, a[8:16] * b[0:1])
    return _pop_tops(stack, k)


def _peer_topk_kernel(qp_ref, sk_ref, e1_out, e2_out, eps_out):
    for hh in range(PEER_HEADS):
        base = 2 * PEER_HALF * hh
        s1 = _dot_nt(sk_ref[0], qp_ref[:, base:base + PEER_HALF])
        s2 = _dot_nt(sk_ref[1], qp_ref[:, base + PEER_HALF:base + 2 * PEER_HALF])
        e1 = jnp.exp(s1 - jnp.max(s1, axis=0, keepdims=True))
        e2 = jnp.exp(s2 - jnp.max(s2, axis=0, keepdims=True))
        a = _top_rows(e1, PEER_TOPK)
        b = _top_rows(e2, PEER_TOPK)
        z = jnp.sum(_top_pair_products(a, b, PEER_TOPK), axis=0, keepdims=True)
        zinv = 1.0 / z
        gates = _top_pair_products(a * zinv, b, PEER_TOPK)
        e1_out[hh] = e1 * zinv
        for j in range(e2.shape[1] // LANES):
            e2_out[hh, j] = e2[:, LANES * j:LANES * (j + 1)]
        eps_out[hh:hh + 1, :] = gates[PEER_TOPK - 1:PEER_TOPK]


def _peer_topk(qp, sub_keys, tm):
    n_, qd = qp.shape
    return pl.pallas_call(
        _peer_topk_kernel,
        grid=(n_ // tm,),
        in_specs=[
            pl.BlockSpec((tm, qd), lambda i: (i, 0)),
            pl.BlockSpec((2, PEER_NKEYS, PEER_HALF), lambda i: (0, 0, 0)),
        ],
        out_specs=[
            pl.BlockSpec((PEER_HEADS, PEER_NKEYS, tm), lambda i: (0, 0, i)),
            pl.BlockSpec((PEER_HEADS, tm // LANES, PEER_NKEYS, LANES), lambda i: (0, i, 0, 0)),
            pl.BlockSpec((PEER_HEADS, tm), lambda i: (0, i)),
        ],
        out_shape=[
            jax.ShapeDtypeStruct((PEER_HEADS, PEER_NKEYS, n_), F32),
            jax.ShapeDtypeStruct((PEER_HEADS, n_ // LANES, PEER_NKEYS, LANES), F32),
            jax.ShapeDtypeStruct((PEER_HEADS, n_), F32),
        ],
        compiler_params=_params(("parallel",)),
        name="peer_topk",
    )(qp, sub_keys)


def _peer_mix_kernel(n_tiles, ht_ref, u_ref, vt_ref, e1_ref, e2_ref, eps_ref, x1_ref, out_ref,
                     acc_ref, crow_ref, a0_ref, a1_ref, z0_ref, z1_ref):
    ie = pl.program_id(1)
    te = u_ref.shape[0]
    tm = ht_ref.shape[1]
    per_tile = te // PEER_NKEYS

    @pl.when(ie == 0)
    def _():
        acc_ref[...] = jnp.zeros_like(acc_ref)

    d_ = vt_ref.shape[1]
    n_lane = tm // LANES
    n_blocks = per_tile * n_lane
    a_units = [(r, c, kc) for r in range(te // MXU_ROWS) for c in range(tm // MXU_ROWS)
               for kc in range(d_ // A_KCHUNK)]
    c_units = [(r, c) for r in range(d_ // MXU_ROWS) for c in range(tm // MXU_ROWS)]

    def step(a_w, a_r, z_w, z_r):
        tile = ie - 1

        def a_unit(r, c, kc):
            rs = slice(MXU_ROWS * r, MXU_ROWS * (r + 1))
            cs = slice(MXU_ROWS * c, MXU_ROWS * (c + 1))
            ds = slice(A_KCHUNK * kc, A_KCHUNK * (kc + 1))
            res = jnp.dot(u_ref[rs, ds], ht_ref[ds, cs], preferred_element_type=F32)
            for jj in range(MXU_ROWS // LANES):
                j = c * (MXU_ROWS // LANES) + jj
                part = res[:, LANES * jj:LANES * (jj + 1)]
                if kc == 0:
                    a_w[j, rs, :] = part
                else:
                    a_w[j, rs, :] += part

        def c_unit(r, c):
            rs = slice(MXU_ROWS * r, MXU_ROWS * (r + 1))
            cs = slice(MXU_ROWS * c, MXU_ROWS * (c + 1))
            res = jnp.dot(vt_ref[0, rs, :], z_r[:, cs], preferred_element_type=F32)
            for jj in range(MXU_ROWS // LANES):
                acc_ref[c * (MXU_ROWS // LANES) + jj, rs, :] += res[:, LANES * jj:LANES * (jj + 1)]

        def b_block(k, j):
            ls = slice(LANES * j, LANES * (j + 1))
            for part in range(PEER_NKEYS // GATE_ROWS):
                ks = slice(GATE_ROWS * part, GATE_ROWS * (part + 1))
                rs = slice(PEER_NKEYS * k + GATE_ROWS * part, PEER_NKEYS * k + GATE_ROWS * (part + 1))
                g = jnp.zeros((GATE_ROWS, LANES), F32)
                for hh in range(PEER_HEADS):
                    row = PEER_HEADS * k + hh
                    gate = e2_ref[hh, j, ks, :] * crow_ref[row:row + 1, ls]
                    g = g + jnp.where(gate >= eps_ref[hh:hh + 1, ls], gate, 0.0)
                z_w[rs, ls] = (_gelu(a_r[j, rs, :]) * g).astype(z_w.dtype)

        do_a, do_b, do_c = a_w is not None, z_w is not None, z_r is not None
        if do_b:
            for k in range(per_tile):
                i1 = tile * per_tile + k
                for hh in range(PEER_HEADS):
                    row = PEER_HEADS * k + hh
                    crow_ref[row:row + 1, :] = e1_ref[hh, pl.ds(i1, 1), :]

        a_iter = iter(a_units if do_a else [])
        c_iter = iter(c_units if do_c else [])
        a_every = max(n_blocks // len(a_units), 1)
        c_per = -(-len(c_units) // n_blocks)
        for blk in range(n_blocks):
            k, j = divmod(blk, n_lane)
            if blk % a_every == 0:
                unit = next(a_iter, None)
                if unit is not None:
                    a_unit(*unit)
            for _ in range(c_per):
                unit = next(c_iter, None)
                if unit is not None:
                    c_unit(*unit)
            if do_b:
                b_block(k, j)
        for unit in a_iter:
            a_unit(*unit)
        for unit in c_iter:
            c_unit(*unit)

    a_bufs, z_bufs = (a0_ref, a1_ref), (z0_ref, z1_ref)

    def run(par, do_a=True, do_b=True, do_c=True):
        step(a_bufs[par] if do_a else None, a_bufs[1 - par], z_bufs[1 - par] if do_b else None,
             z_bufs[par] if do_c else None)

    last = n_tiles + 1
    pl.when(ie == 0)(lambda: run(0, do_b=False, do_c=False))
    pl.when(ie == 1)(lambda: run(1, do_c=False))
    pl.when((ie >= 2) & (ie < n_tiles) & (ie % 2 == 0))(lambda: run(0))
    pl.when((ie >= 2) & (ie < n_tiles) & (ie % 2 == 1))(lambda: run(1))
    pl.when(ie == n_tiles)(lambda: run(n_tiles % 2, do_a=False))
    pl.when(ie == last)(lambda: run(last % 2, do_a=False, do_b=False))

    @pl.when(ie == last)
    def _():
        for j in range(n_lane):
            ts = slice(LANES * j, LANES * (j + 1))
            out_ref[ts, :] = x1_ref[ts, :] + acc_ref[j].T


def _peer_mix(h2t, u, vt, e1t, e2t, eps, x1, tm, te):
    d_, n_ = h2t.shape
    n_tiles = u.shape[0] // te
    assert n_tiles >= 3
    return pl.pallas_call(
        functools.partial(_peer_mix_kernel, n_tiles),
        grid=(n_ // tm, n_tiles + 2),
        in_specs=[
            pl.BlockSpec((d_, tm), lambda it, ie: (0, it)),
            pl.BlockSpec((te, d_), lambda it, ie: (jnp.minimum(ie, n_tiles - 1), 0)),
            pl.BlockSpec((1, d_, te), lambda it, ie: (jnp.maximum(ie - 2, 0), 0, 0)),
            pl.BlockSpec((PEER_HEADS, PEER_NKEYS, tm), lambda it, ie: (0, 0, it)),
            pl.BlockSpec((PEER_HEADS, tm // LANES, PEER_NKEYS, LANES), lambda it, ie: (0, it, 0, 0)),
            pl.BlockSpec((PEER_HEADS, tm), lambda it, ie: (0, it)),
            pl.BlockSpec((tm, d_), lambda it, ie: (it, 0)),
        ],
        out_specs=pl.BlockSpec((tm, d_), lambda it, ie: (it, 0)),
        out_shape=jax.ShapeDtypeStruct((n_, d_), F32),
        scratch_shapes=[
            pltpu.VMEM((tm // LANES, d_, LANES), F32),
            pltpu.VMEM((PEER_HEADS * (te // PEER_NKEYS), tm), F32),
            pltpu.VMEM((tm // LANES, te, LANES), F32), pltpu.VMEM((tm // LANES, te, LANES), F32),
            pltpu.VMEM((te, tm), MXU_DTYPE), pltpu.VMEM((te, tm), MXU_DTYPE),
        ],
        compiler_params=_params(("parallel", "arbitrary")),
        name="peer_mix",
    )(h2t, u, vt, e1t, e2t, eps, x1)


def _rel_bucket(dist):
    n = jnp.maximum(dist, 0)
    max_exact = REL_BUCKETS // 2
    nf = jnp.maximum(n, 1).astype(F32)
    large = max_exact + (jnp.log(nf / max_exact) / math.log(REL_MAX_DIST / max_exact)
                         * (REL_BUCKETS - max_exact)).astype(jnp.int32)
    large = jnp.minimum(large, REL_BUCKETS - 1)
    return jnp.where(n < max_exact, n, large)


def _toeplitz(btab, off, rows, cols):
    per = rows + cols
    k = np.concatenate([np.arange(cols), np.zeros(1, np.int64), np.arange(-(rows - 1), 0)])
    idx = np.clip(off - k, 0, btab.shape[1] - 1)
    w = jnp.take(btab, jnp.asarray(idx, jnp.int32), axis=1)
    x = jnp.tile(w, (1, rows))[:, :rows * (per - 1)]
    return x.reshape(btab.shape[0], rows, per - 1)[:, :, :cols]


def _bias_tiles(rel_bias):
    assert SEL_TILE * NEAR_TILES - Q_BLOCK + 1 >= FAR_DIST
    assert CMP_STRIDE * (LANES + 1) - (CMP_BLOCK - 1) >= FAR_DIST
    dist = jnp.arange(BIAS_TABLE, dtype=jnp.int32)
    rel_bias = rel_bias.astype(F32) * LOG2E
    btab = rel_bias[_rel_bucket(dist)].T
    tbw = jnp.stack([_toeplitz(btab, SEL_TILE * (dd + 1) - Q_BLOCK, Q_BLOCK, SEL_TILE)
                     for dd in range(NEAR_TILES)], axis=1)
    wb = _toeplitz(btab, WINDOW, Q_BLOCK, WIN_KEYS)
    off = CMP_STRIDE * (LANES - 8) - (CMP_BLOCK - 1)
    front = CMP_STRIDE * LANES - off
    length = CMP_STRIDE * (LANES + 8 + 2)
    padded = jnp.pad(btab, ((0, 0), (front, length - front - BIAS_TABLE)))
    ch = padded.reshape(btab.shape[0], length // CMP_STRIDE, CMP_STRIDE)
    nwin = LANES + 1
    win = jnp.stack([ch[:, a:a + nwin] for a in range(Q_BLOCK // CMP_STRIDE)], axis=2)
    win = win.reshape(btab.shape[0], nwin, Q_BLOCK)
    cb = jnp.swapaxes(win[:, 1:nwin][:, ::-1], 1, 2)
    cfar = rel_bias.astype(F32)[REL_BUCKETS - 1]
    grp = lambda a: a.reshape((NSA_KV, NSA_REP) + a.shape[1:])

    r = np.arange(Q_BLOCK)[:, None]
    near_ok = np.stack([r - np.arange(SEL_TILE)[None, :] + SEL_TILE * (dd + 1) - Q_BLOCK >= 0
                        for dd in range(NEAR_TILES)])
    tbm = jnp.where(near_ok, tbw - cfar[:, None, None, None], NEG)
    dist_w = r - np.arange(WIN_KEYS)[None, :] + WINDOW
    wbm = jnp.where((dist_w >= 0) & (dist_w < WINDOW), wb, NEG)
    dist_c = r - CMP_STRIDE * np.arange(CMP_PAD)[None, :] + (CMP_STRIDE * CMP_PAD - Q_BLOCK - (CMP_BLOCK - 1))
    cbias = jnp.concatenate(
        [jnp.broadcast_to(cfar[:, None, None], (NSA_HEADS, Q_BLOCK, CMP_PAD - LANES)), cb], axis=2)
    cbm = jnp.where(dist_c >= 0, cbias, NEG)

    hi = cfar.astype(BF16).astype(F32)
    lo = (cfar - hi).astype(BF16).astype(F32)
    qmid = jnp.zeros((NSA_HEADS, 1, LANES), F32)
    qmid = qmid.at[:, 0, FAR_HI_LANE - HEAD_DIM].set(hi).at[:, 0, FAR_LO_LANE - HEAD_DIM].set(lo)
    qmid = qmid.at[:, 0, PAD_FLAG_LANE - HEAD_DIM].set(NEG)
    return grp(tbm), grp(wbm), grp(cbm), grp(qmid)


def _kv_layout_kernel(kv_ref, ts_ref, oh_ref, tw_ref, tv_ref, ksa_out, vsp_out, kwa_out, vwp_out):
    tm = kv_ref.shape[1]
    real = pl.program_id(1) >= KEY_PAD // tm
    blk = jnp.where(real, kv_ref[0], jnp.zeros_like(kv_ref[0]))
    for g in range(NSA_KV):
        for n, (out, tail) in enumerate(((ksa_out, ts_ref), (vsp_out, tv_ref),
                                         (kwa_out, tw_ref), (vwp_out, tv_ref))):
            lo = D_KV * n + HEAD_DIM * g
            out[0, g, :, 0:HEAD_DIM] = blk[:, lo:lo + HEAD_DIM]
            out[0, g, :, HEAD_DIM:LANES] = tail[...]
        ksa_out[0, g, :, LANES:2 * LANES] = oh_ref[...]


def _kv_layout(kv, tm):
    b_, t_, _ = kv.shape
    tp = t_ + KEY_PAD
    pos = np.arange(tp) - KEY_PAD
    real = pos >= 0
    tail = np.zeros((3, tp, LANES - HEAD_DIM), np.float32)
    tail[0, :, FAR_HI_LANE - HEAD_DIM] = real
    tail[0, :, FAR_LO_LANE - HEAD_DIM] = real
    tail[0, :, PAD_FLAG_LANE - HEAD_DIM] = ~real
    tail[1, :, PAD_FLAG_LANE - HEAD_DIM] = ~real
    tail[2, :, 0] = real
    onehot = ((pos[:, None] // SLC_BLOCK) == np.arange(LANES)[None, :]) & real[:, None]
    const = lambda a: jnp.asarray(a, kv.dtype)
    rows = lambda wd: pl.BlockSpec((tm, wd), lambda b, i: (i, 0))
    heads = lambda wd: pl.BlockSpec((1, NSA_KV, tm, wd), lambda b, i: (b, 0, i, 0))
    shape = lambda wd: jax.ShapeDtypeStruct((b_, NSA_KV, tp, wd), kv.dtype)
    return pl.pallas_call(
        _kv_layout_kernel,
        grid=(b_, tp // tm),
        in_specs=[
            pl.BlockSpec((1, tm, 4 * D_KV), lambda b, i: (b, jnp.maximum(i - KEY_PAD // tm, 0), 0)),
            rows(LANES - HEAD_DIM), rows(LANES), rows(LANES - HEAD_DIM), rows(LANES - HEAD_DIM),
        ],
        out_specs=[heads(2 * LANES), heads(LANES), heads(LANES), heads(LANES)],
        out_shape=[shape(2 * LANES), shape(LANES), shape(LANES), shape(LANES)],
        compiler_params=_params(("parallel", "parallel")),
        name="kv_layout",
    )(kv, const(tail[0]), const(onehot.astype(np.float32)), const(tail[1]), const(tail[2]))


def _overlap_matrix():
    c = np.arange(CMP_PAD)[:, None]
    s = np.arange(LANES)[None, :]
    per = SLC_BLOCK // CMP_STRIDE
    ov = (s == c // per) | ((c % per == per - 1) & (s == c // per + 1))
    return jnp.asarray(ov.astype(np.float32), BF16)


def kernel(x, attn_norm_g, w_in, q_norm_g, k_norm_g, cmp_pe_k, cmp_w1_k, cmp_w2_k, cmp_pe_v,
           cmp_w1_v, cmp_w2_v, rel_bias, sg_ln_g, sg_ln_b, sg_w, sg_b, out_norm_nsa, out_norm_sg,
           w_out, ffn_norm_g, peer_w_query, peer_sub_keys, peer_u, peer_v):
    b_, t_, d_ = x.shape
    n_ = b_ * t_
    assert t_ % Q_BLOCK == 0 and t_ >= WIN_KEYS and d_ == D_NSA + D_SG

    c_kv = D_NSA + 6 * D_KV
    c_gate = c_kv + 3 * NSA_HEADS
    wg = w_in[:, c_kv:c_gate].reshape(d_, NSA_KV, 3 * NSA_REP)
    wg = jnp.pad(wg, ((0, 0), (0, 0), (0, LANES - 3 * NSA_REP))).reshape(d_, NSA_KV * LANES)
    w_attn = jnp.concatenate([w_in[:, :c_kv], wg], axis=1).astype(MXU_DTYPE)
    w_sgu = w_in[:, c_gate:].astype(MXU_DTYPE)
    ng = attn_norm_g.reshape(1, d_)

    gq = jnp.tile(q_norm_g, NSA_HEADS).reshape(1, D_NSA) * (HEAD_DIM ** -0.5 * LOG2E)
    no_gain = jnp.zeros((D_KV,), F32)
    gkv = jnp.concatenate([jnp.tile(k_norm_g[1], NSA_KV), no_gain,
                           jnp.tile(k_norm_g[2], NSA_KV), no_gain]).reshape(1, 4 * D_KV)
    qn, kcv, kv, gate = _attn_proj(x, ng, w_attn, gq, gkv, tm=256)
    ksa, vsp, kwa, vwp = _kv_layout(kv, tm=512)
    gu, vn = _sgu_proj(x, ng, w_sgu, sg_ln_g, sg_ln_b, tm=256)

    nch = t_ // CMP_STRIDE
    xc = kcv.reshape(b_, nch, CMP_STRIDE, 2, NSA_KV, HEAD_DIM).transpose(0, 3, 4, 1, 2, 5)
    xc = xc.reshape(b_, 2, NSA_KV, nch, CMP_STRIDE * HEAD_DIM)
    w1 = jnp.stack([cmp_w1_k, cmp_w1_v]).astype(MXU_DTYPE)
    w2 = jnp.stack([cmp_w2_k, cmp_w2_v]).astype(MXU_DTYPE)
    pe = jnp.stack([cmp_pe_k, cmp_pe_v]).transpose(0, 2, 1, 3).reshape(2, NSA_KV, 1, CMP_BLOCK * HEAD_DIM)
    kcv_c = _compress(xc, w1, w2, pe.astype(MXU_DTYPE), k_norm_g)

    tbm, wbm, cbm, qmid = _bias_tiles(rel_bias)
    onsa = _nsa_attention(qn, gate, kcv_c, ksa, vsp, kwa, vwp, tbm, wbm, cbm, qmid, _overlap_matrix())

    x1, h2t, qp = _mix(x.reshape(n_, d_), onsa.reshape(n_, D_NSA), gu.reshape(n_, D_SG),
                      vn.reshape(n_, D_SG), sg_w, sg_b.T, out_norm_nsa.reshape(1, D_NSA),
                      out_norm_sg.reshape(1, D_SG), w_out.astype(MXU_DTYPE),
                      ffn_norm_g.reshape(1, d_), peer_w_query.astype(MXU_DTYPE), tm=256)

    e1t, e2t, eps = _peer_topk(qp, peer_sub_keys, tm=256)
    te = 512
    vt = peer_v.reshape(-1, te, d_).transpose(0, 2, 1).astype(MXU_DTYPE)
    out = _peer_mix(h2t, peer_u.astype(MXU_DTYPE), vt, e1t, e2t, eps, x1, tm=min(512, n_), te=te)
    return out.reshape(b_, t_, d_)
```

```python
import functools
import math

import jax
import jax.numpy as jnp
import numpy as np
from jax import lax
from jax.experimental import pallas as pl
from jax.experimental.pallas import tpu as pltpu

F32 = jnp.float32
BF16 = jnp.bfloat16
MXU_DTYPE = BF16

EPS = 1e-6
NEG = -1e30
SQRT_HALF = 0.7071067811865476
LOG2E = 1.4426950408889634

NSA_HEADS = 16
NSA_KV = 4
NSA_REP = 4
HEAD_DIM = 64
D_NSA = NSA_HEADS * HEAD_DIM
D_KV = NSA_KV * HEAD_DIM
CMP_STRIDE = 16
CMP_BLOCK = 32
CMP_HIDDEN = 128
SLC_BLOCK = 64
SLC_TOPN = 16
WINDOW = 512
Q_BLOCK = 128
SG_HEADS = 8
SG_DIM = 128
SG_CHUNK = 128
D_SG = SG_HEADS * SG_DIM
REL_BUCKETS = 32
REL_MAX_DIST = 1024
PEER_HEADS = 8
PEER_NKEYS = 128
PEER_HALF = 128
PEER_TOPK = 16

LANES = 128
MXU_ROWS = 256
A_KCHUNK = 1024
GATE_ROWS = 64
KEY_PAD = 1024
CMP_PAD = 512
SEL_TILE = 512
WIN_KEYS = WINDOW + Q_BLOCK
NEAR_TILES = 2
FAR_GROUPS = (8, 4, 2)
BIAS_TABLE = 2048
FAR_DIST = 897
FAR_HI_LANE = 64
FAR_LO_LANE = 65
PAD_FLAG_LANE = 66
N_FORCED = 3
VMEM_LIMIT = 56 * 1024 * 1024


def _gelu(x):
    return 0.5 * x * (1.0 + lax.erf(x * SQRT_HALF))


def _dot(a, b):
    return jnp.dot(a.astype(MXU_DTYPE), b.astype(MXU_DTYPE), preferred_element_type=F32)


def _dot_nt(a, b):
    return lax.dot_general(a.astype(MXU_DTYPE), b.astype(MXU_DTYPE),
                           (((1,), (1,)), ((), ())), preferred_element_type=F32)


def _rms(x, g):
    ms = jnp.mean(x * x, axis=-1, keepdims=True)
    return x * lax.rsqrt(ms + EPS) * g


def _params(sem):
    return pltpu.CompilerParams(dimension_semantics=sem, vmem_limit_bytes=VMEM_LIMIT)


def _attn_proj_kernel(x_ref, ng_ref, w_ref, gq_ref, gkv_ref, hs_ref, he_ref,
                      q_out, kcv_out, kv_out, gate_out):
    h = _rms(x_ref[0], ng_ref[...]).astype(MXU_DTYPE)

    def head_rms(p):
        ms = jnp.dot((p * p).astype(MXU_DTYPE), hs_ref[...], preferred_element_type=F32)
        r = lax.rsqrt(ms + EPS)
        r_hi = r.astype(BF16)
        r_lo = (r - r_hi.astype(F32)).astype(BF16)
        he = he_ref[...]
        return p * (jnp.dot(r_hi, he, preferred_element_type=F32)
                    + jnp.dot(r_lo, he, preferred_element_type=F32))

    pq = _dot(h, w_ref[:, 0:D_NSA])
    q_out[0] = (head_rms(pq) * gq_ref[...]).astype(q_out.dtype)

    kcv_out[0] = _dot(h, w_ref[:, D_NSA:D_NSA + 2 * D_KV]).astype(kcv_out.dtype)

    base = D_NSA + 2 * D_KV
    p4 = _dot(h, w_ref[:, base:base + 4 * D_KV])
    col = lax.broadcasted_iota(jnp.int32, (1, 4 * D_KV), 1)
    is_key = (col // D_KV) % 2 == 0
    kv_out[0] = jnp.where(is_key, head_rms(p4) * gkv_ref[...], p4).astype(kv_out.dtype)

    base = base + 4 * D_KV
    gate_out[0] = jax.nn.sigmoid(_dot(h, w_ref[:, base:base + NSA_KV * LANES]))


def _attn_proj(x, ng, w, gq, gkv, tm):
    b_, t_, d_ = x.shape
    nw = w.shape[1]
    heads = np.arange(D_NSA) // HEAD_DIM
    hs = (heads[:, None] == np.arange(LANES)[None, :]).astype(np.float32)
    const = lambda shape: pl.BlockSpec(shape, lambda b, i: (0,) * len(shape))
    return pl.pallas_call(
        _attn_proj_kernel,
        grid=(b_, t_ // tm),
        in_specs=[
            pl.BlockSpec((1, tm, d_), lambda b, i: (b, i, 0)),
            const((1, d_)),
            pl.BlockSpec((d_, nw), lambda b, i: (0, 0), pipeline_mode=pl.Buffered(1)),
            const((1, D_NSA)), const((1, 4 * D_KV)),
            const((D_NSA, LANES)), const((LANES, D_NSA)),
        ],
        out_specs=[
            pl.BlockSpec((1, tm, D_NSA), lambda b, i: (b, i, 0)),
            pl.BlockSpec((1, tm, 2 * D_KV), lambda b, i: (b, i, 0)),
            pl.BlockSpec((1, tm, 4 * D_KV), lambda b, i: (b, i, 0)),
            pl.BlockSpec((1, tm, NSA_KV * LANES), lambda b, i: (b, i, 0)),
        ],
        out_shape=[
            jax.ShapeDtypeStruct((b_, t_, D_NSA), MXU_DTYPE),
            jax.ShapeDtypeStruct((b_, t_, 2 * D_KV), MXU_DTYPE),
            jax.ShapeDtypeStruct((b_, t_, 4 * D_KV), MXU_DTYPE),
            jax.ShapeDtypeStruct((b_, t_, NSA_KV * LANES), F32),
        ],
        compiler_params=_params(("parallel", "parallel")),
        name="attn_proj",
    )(x, ng, w, gq, gkv, jnp.asarray(hs / HEAD_DIM, MXU_DTYPE), jnp.asarray(hs.T, BF16))


def _sgu_proj_kernel(x_ref, ng_ref, w_ref, lng_ref, lnb_ref, gu_out, vn_out):
    h = _rms(x_ref[0], ng_ref[...]).astype(MXU_DTYPE)
    gu_out[0] = _gelu(_dot(h, w_ref[:, 0:D_SG])).astype(gu_out.dtype)
    gv = _gelu(_dot(h, w_ref[:, D_SG:2 * D_SG]))
    for hh in range(SG_HEADS):
        sl = slice(SG_DIM * hh, SG_DIM * (hh + 1))
        v = gv[:, sl]
        mu = jnp.mean(v, axis=-1, keepdims=True)
        var = jnp.mean(jnp.square(v - mu), axis=-1, keepdims=True)
        y = (v - mu) * lax.rsqrt(var + EPS) * lng_ref[hh:hh + 1, :] + lnb_ref[hh:hh + 1, :]
        vn_out[0, :, sl] = y.astype(vn_out.dtype)


def _sgu_proj(x, ng, w, lng, lnb, tm):
    b_, t_, d_ = x.shape
    return pl.pallas_call(
        _sgu_proj_kernel,
        grid=(b_, t_ // tm),
        in_specs=[
            pl.BlockSpec((1, tm, d_), lambda b, i: (b, i, 0)),
            pl.BlockSpec((1, d_), lambda b, i: (0, 0)),
            pl.BlockSpec((d_, 2 * D_SG), lambda b, i: (0, 0), pipeline_mode=pl.Buffered(1)),
            pl.BlockSpec((SG_HEADS, SG_DIM), lambda b, i: (0, 0)),
            pl.BlockSpec((SG_HEADS, SG_DIM), lambda b, i: (0, 0)),
        ],
        out_specs=[
            pl.BlockSpec((1, tm, D_SG), lambda b, i: (b, i, 0)),
            pl.BlockSpec((1, tm, D_SG), lambda b, i: (b, i, 0)),
        ],
        out_shape=[
            jax.ShapeDtypeStruct((b_, t_, D_SG), MXU_DTYPE),
            jax.ShapeDtypeStruct((b_, t_, D_SG), MXU_DTYPE),
        ],
        compiler_params=_params(("parallel", "parallel")),
        name="sgu_proj",
    )(x, ng, w, lng, lnb)


def _compress_kernel(x_ref, w1_ref, w2_ref, pe_ref, kg_ref, out_ref):
    which = pl.program_id(1)
    x = x_ref[0, 0, 0]
    nch = x.shape[0]
    half = CMP_STRIDE * HEAD_DIM
    a = _dot(x, w1_ref[0, 0, 0:half, :])
    bm = _dot(x, w1_ref[0, 0, half:2 * half, :])
    pec = _dot(pe_ref[0, 0], w1_ref[0, 0])
    hid = _gelu(a + pltpu.roll(bm, shift=nch - 1, axis=0) + pec)
    y = _dot(hid, w2_ref[0, 0])
    y = jnp.where(which == 0, _rms(y, kg_ref[0:1, :]), y)
    lane = lax.broadcasted_iota(jnp.int32, (CMP_PAD, LANES), 1)
    flag = ((lane == PAD_FLAG_LANE) & (which == 0)).astype(out_ref.dtype)
    out_ref[0, 0, 0, 0:CMP_PAD, :] = flag
    out_ref[0, 0, 0, CMP_PAD:CMP_PAD + nch, 0:HEAD_DIM] = y.astype(out_ref.dtype)
    out_ref[0, 0, 0, CMP_PAD:CMP_PAD + nch, HEAD_DIM:LANES] = jnp.zeros((nch, LANES - HEAD_DIM), out_ref.dtype)


def _compress(xc, w1, w2, pe, kg):
    b_, _, g_, nch, fl = xc.shape
    return pl.pallas_call(
        _compress_kernel,
        grid=(b_, 2, g_),
        in_specs=[
            pl.BlockSpec((1, 1, 1, nch, fl), lambda b, w, g: (b, w, g, 0, 0)),
            pl.BlockSpec((1, 1, 2 * fl, CMP_HIDDEN), lambda b, w, g: (w, g, 0, 0)),
            pl.BlockSpec((1, 1, CMP_HIDDEN, HEAD_DIM), lambda b, w, g: (w, g, 0, 0)),
            pl.BlockSpec((1, 1, 1, 2 * fl), lambda b, w, g: (w, g, 0, 0)),
            pl.BlockSpec((3, HEAD_DIM), lambda b, w, g: (0, 0)),
        ],
        out_specs=pl.BlockSpec((1, 1, 1, CMP_PAD + nch, LANES), lambda b, w, g: (b, w, g, 0, 0)),
        out_shape=jax.ShapeDtypeStruct((b_, 2, g_, CMP_PAD + nch, LANES), F32),
        compiler_params=_params(("parallel", "parallel", "parallel")),
        name="nsa_compress",
    )(xc, w1, w2, pe, kg)


def _nsa_kernel(*refs):
    accs_ref, accw_ref, oc_ref = refs[-3:]
    step, last = pl.program_id(2), pl.num_programs(2) - 1

    @pl.when((pl.program_id(0) == 0) & (pl.program_id(1) == 0) & (step == 0))
    def _():
        accs_ref[...] = jnp.ones_like(accs_ref)
        accw_ref[...] = jnp.ones_like(accw_ref)
        oc_ref[...] = jnp.zeros_like(oc_ref)

    @pl.when(step < last)
    def _():
        _nsa_finish_previous(*refs)
        _nsa_attend(*refs)

    @pl.when(step == last)
    def _():
        _nsa_finish_previous(*refs)


def _nsa_finish_previous(q_ref, gate_ref, kc_ref, vc_ref, ks_ref, vs_ref, kw_ref, vw_ref,
                         tbm_ref, wbm_ref, cbm_ref, qmid_ref, ov_ref, o_ref, qa_ref,
                         accs_ref, accw_ref, oc_ref):
    qb = Q_BLOCK
    gt = gate_ref[0]
    for r in range(NSA_REP):
        rs = slice(qb * r, qb * (r + 1))
        o_s = accs_ref[rs, 0:HEAD_DIM] / accs_ref[rs, HEAD_DIM:HEAD_DIM + 1]
        o_w = accw_ref[rs, 0:HEAD_DIM] / accw_ref[rs, HEAD_DIM:HEAD_DIM + 1]
        o = (gt[:, 3 * r + 0:3 * r + 1] * oc_ref[rs, :] + gt[:, 3 * r + 1:3 * r + 2] * o_s
             + gt[:, 3 * r + 2:3 * r + 3] * o_w)
        o_ref[0, :, HEAD_DIM * r:HEAD_DIM * (r + 1)] = o.astype(o_ref.dtype)


def _nsa_attend(q_ref, gate_ref, kc_ref, vc_ref, ks_ref, vs_ref, kw_ref, vw_ref,
                tbm_ref, wbm_ref, cbm_ref, qmid_ref, ov_ref, o_ref, qa_ref,
                accs_ref, accw_ref, oc_ref):
    i = pl.program_id(2)
    qb = Q_BLOCK
    rows = NSA_REP * qb
    ncmp = CMP_PAD

    qblk = q_ref[0]
    for r in range(NSA_REP):
        rs = slice(qb * r, qb * (r + 1))
        qa_ref[rs, 0:HEAD_DIM] = qblk[:, HEAD_DIM * r:HEAD_DIM * (r + 1)]
        qa_ref[rs, HEAD_DIM:LANES] = jnp.broadcast_to(
            qmid_ref[0, r][:, 0:LANES - HEAD_DIM], (qb, LANES - HEAD_DIM)).astype(qa_ref.dtype)
    q_lo = qa_ref[:, 0:LANES]

    wstart = pl.multiple_of(KEY_PAD + qb * (i + 1) - WIN_KEYS, qb)
    s_w = _dot_nt(q_lo, kw_ref[0, 0, pl.ds(wstart, WIN_KEYS), :])

    def window_probs(r):
        s = s_w[qb * r:qb * (r + 1)] + wbm_ref[0, r]
        return jnp.exp2(s - jnp.max(s, axis=-1, keepdims=True))

    cstart = pl.multiple_of(8 * i + 8, 8)
    kc = kc_ref[0, 0, 0, pl.ds(cstart, ncmp), :]
    vc = vc_ref[0, 0, 0, pl.ds(cstart, ncmp), :]
    s_c = _dot_nt(q_lo, kc)

    kt = SEL_TILE
    n_tiles = (i * qb + qb + kt - 1) // kt

    def tile_start(dd):
        return pl.multiple_of(KEY_PAD + qb * (i + 1) - kt * (dd + 1), qb)

    near_base = [_dot_nt(q_lo, ks_ref[0, 0, pl.ds(tile_start(dd), kt), 0:LANES])
                 for dd in range(NEAR_TILES)]
    psum = jnp.zeros((qb, ncmp), F32)
    for r in range(NSA_REP):
        s = s_c[qb * r:qb * (r + 1)] + cbm_ref[0, r]
        m = jnp.max(s, axis=-1, keepdims=True)
        p = jnp.exp2(s - m)
        l = jnp.sum(p, axis=-1, keepdims=True)
        p = p * jnp.where(m > 0.5 * NEG, 1.0 / l, 0.0)
        oc_ref[qb * r:qb * (r + 1), :] = _dot(p, vc)[:, 0:HEAD_DIM]
        psum = psum + p
    p_hi = psum.astype(BF16)
    p_lo = (psum - p_hi.astype(F32)).astype(BF16)
    ov = ov_ref[...]
    imp = (jnp.dot(p_hi, ov, preferred_element_type=F32)
           + jnp.dot(p_lo, ov, preferred_element_type=F32))

    nsel = LANES
    sp = lax.broadcasted_iota(jnp.int32, (qb, nsel), 1)
    rr = lax.broadcasted_iota(jnp.int32, (qb, nsel), 0)
    cur = (nsel - 2) + (rr >= SLC_BLOCK).astype(jnp.int32)
    s_abs = sp + (2 * i + 2 - nsel)
    valid = s_abs >= 0
    forced = ((s_abs == 0) | (sp == cur) | (sp == cur - 1)) & valid
    excluded = forced | (sp > cur) | (~valid)
    cand_t = jnp.where(excluded, NEG, imp).T
    stack = _sort_stack([cand_t[8 * r:8 * (r + 1)] for r in range(nsel // 8)])
    tau = None
    n_rank = SLC_TOPN - N_FORCED
    p_w = []
    for it in range(n_rank):
        tau, stack = _pop_max(stack, min(len(stack), n_rank - it), 3.0 * NEG)
        if it % (n_rank // NSA_REP) == 0 and len(p_w) < NSA_REP:
            p_w.append(window_probs(len(p_w)))
    accw_ref[...] = _dot(jnp.concatenate(p_w, axis=0), vw_ref[0, 0, pl.ds(wstart, WIN_KEYS), :])
    picked = jnp.where((cand_t >= tau) & (cand_t > 0.5 * NEG), 1.0, 0.0).T
    sel_neg = jnp.where((picked > 0.5) | forced, 0.0, NEG)
    sel_neg = pltpu.roll(sel_neg, shift=(2 * i + 2) % nsel, axis=1).astype(qa_ref.dtype)
    for r in range(NSA_REP):
        qa_ref[qb * r:qb * (r + 1), LANES:2 * LANES] = sel_neg
    q_ext = qa_ref[...]

    def tile_scores(dd):
        return _dot_nt(q_ext, ks_ref[0, 0, pl.ds(tile_start(dd), kt), :])

    def tile_softmax(s, dd):
        m_t = jnp.max(s, axis=-1, keepdims=True)
        return m_t, _dot(jnp.exp2(s - m_t), vs_ref[0, 0, pl.ds(tile_start(dd), kt), :])

    def merge(parts):
        m_new = functools.reduce(jnp.maximum, [m for m, _ in parts])
        acc = sum(jnp.exp2(m - m_new) * pv for m, pv in parts)
        return m_new, acc

    near = []
    for dd in range(NEAR_TILES):
        msk = _dot_nt(sel_neg, ks_ref[0, 0, pl.ds(tile_start(dd), kt), LANES:2 * LANES])
        near.append(jnp.concatenate([near_base[dd][qb * r:qb * (r + 1)] + (tbm_ref[0, r, dd] + msk)
                                     for r in range(NSA_REP)], axis=0))
    carry = merge([tile_softmax(s, dd) for dd, s in enumerate(near)])

    def far_group(first, width, carry):
        dds = [jnp.minimum(first + k, n_tiles) for k in range(width)]
        scores = [tile_scores(dd) for dd in dds]
        return merge([carry] + [tile_softmax(s, dd) for s, dd in zip(scores, dds)])

    first = NEAR_TILES
    left = jnp.maximum(n_tiles - NEAR_TILES, 0)
    for width in FAR_GROUPS:
        trips = (left + width - 1) // width if width == FAR_GROUPS[-1] else left // width
        carry = lax.fori_loop(
            0, trips, lambda u, c, first=first, width=width: far_group(first + width * u, width, c), carry)
        first = first + width * trips
        left = jnp.maximum(left - width * trips, 0)
    accs_ref[...] = carry[1]


def _nsa_attention(qn, gate, kcv, ksa, vsp, kwa, vwp, tbm, wbm, cbm, qmid, ov):
    b_, t_, _ = qn.shape
    tp = ksa.shape[2]
    ncp = kcv.shape[3]
    gw = NSA_REP * HEAD_DIM
    assert t_ // SLC_BLOCK <= LANES
    kvspec = lambda wd: pl.BlockSpec((1, 1, tp, wd), lambda b, g, i: (b, g, 0, 0))
    nq = t_ // Q_BLOCK
    behind = lambda b, g, i: (b, jnp.maximum(i - 1, 0), g)
    return pl.pallas_call(
        _nsa_kernel,
        grid=(b_, NSA_KV, nq + 1),
        in_specs=[
            pl.BlockSpec((1, Q_BLOCK, gw), lambda b, g, i: (b, jnp.minimum(i, nq - 1), g)),
            pl.BlockSpec((1, Q_BLOCK, LANES), behind),
            pl.BlockSpec((1, 1, 1, ncp, LANES), lambda b, g, i: (b, 0, g, 0, 0)),
            pl.BlockSpec((1, 1, 1, ncp, LANES), lambda b, g, i: (b, 1, g, 0, 0)),
            kvspec(2 * LANES), kvspec(LANES), kvspec(LANES), kvspec(LANES),
            pl.BlockSpec((1, NSA_REP, NEAR_TILES, Q_BLOCK, SEL_TILE), lambda b, g, i: (g, 0, 0, 0, 0)),
            pl.BlockSpec((1, NSA_REP, Q_BLOCK, WIN_KEYS), lambda b, g, i: (g, 0, 0, 0)),
            pl.BlockSpec((1, NSA_REP, Q_BLOCK, CMP_PAD), lambda b, g, i: (g, 0, 0, 0)),
            pl.BlockSpec((1, NSA_REP, 1, LANES), lambda b, g, i: (g, 0, 0, 0)),
            pl.BlockSpec((CMP_PAD, LANES), lambda b, g, i: (0, 0)),
        ],
        out_specs=pl.BlockSpec((1, Q_BLOCK, gw), behind),
        out_shape=jax.ShapeDtypeStruct((b_, t_, D_NSA), F32),
        scratch_shapes=[
            pltpu.VMEM((NSA_REP * Q_BLOCK, 2 * LANES), MXU_DTYPE),
            pltpu.VMEM((NSA_REP * Q_BLOCK, LANES), F32),
            pltpu.VMEM((NSA_REP * Q_BLOCK, LANES), F32),
            pltpu.VMEM((NSA_REP * Q_BLOCK, HEAD_DIM), F32),
        ],
        compiler_params=_params(("parallel", "parallel", "arbitrary")),
        name="nsa_attention",
    )(qn, gate, kcv, kcv, ksa, vsp, kwa, vwp, tbm, wbm, cbm, qmid, ov)


def _mix_kernel(x_ref, on_ref, gu_ref, vn_ref, sgw_ref, sgb_ref, gn_ref, gs_ref, wo_ref,
                fg_ref, wq_ref, x1_out, h2t_out, qp_out):
    tm = x_ref.shape[0]
    tri = (lax.broadcasted_iota(jnp.int32, (SG_CHUNK, SG_CHUNK), 0)
           >= lax.broadcasted_iota(jnp.int32, (SG_CHUNK, SG_CHUNK), 1))
    sgb = sgb_ref[...]
    y = x_ref[...] + _dot(_rms(on_ref[...], gn_ref[...]), wo_ref[0:D_NSA, :])
    parts = []
    for c in range(tm // SG_CHUNK):
        cs = slice(SG_CHUNK * c, SG_CHUNK * (c + 1))
        heads = []
        for hh in range(SG_HEADS):
            sl = slice(SG_DIM * hh, SG_DIM * (hh + 1))
            w = jnp.where(tri, sgw_ref[hh], 0.0)
            mixed = _dot(w, vn_ref[cs, sl]) + sgb[:, hh:hh + 1]
            heads.append(gu_ref[cs, sl].astype(F32) * mixed)
        parts.append(jnp.concatenate(heads, axis=1))
    o_sg = jnp.concatenate(parts, axis=0)
    y = y + _dot(_rms(o_sg, gs_ref[...]), wo_ref[D_NSA:D_NSA + D_SG, :])
    x1_out[...] = y
    h2 = _rms(y, fg_ref[...])
    h2t_out[...] = h2.T.astype(h2t_out.dtype)
    qp_out[...] = _dot(h2, wq_ref[...]).astype(qp_out.dtype)


def _mix(x2, onsa, gu, vn, sgw, sgb_t, gn, gs, wo, fg, wq, tm):
    n_, d_ = x2.shape
    row = lambda wd: pl.BlockSpec((tm, wd), lambda i: (i, 0))
    const = lambda shape, **kw: pl.BlockSpec(shape, lambda i: (0,) * len(shape), **kw)
    return pl.pallas_call(
        _mix_kernel,
        grid=(n_ // tm,),
        in_specs=[
            row(d_), row(D_NSA), row(D_SG), row(D_SG),
            const((SG_HEADS, SG_CHUNK, SG_CHUNK)), const((SG_CHUNK, SG_HEADS)),
            const((1, D_NSA)), const((1, D_SG)),
            const((D_NSA + D_SG, d_), pipeline_mode=pl.Buffered(1)),
            const((1, d_)),
            const((d_, wq.shape[1]), pipeline_mode=pl.Buffered(1)),
        ],
        out_specs=[row(d_), pl.BlockSpec((d_, tm), lambda i: (0, i)), row(wq.shape[1])],
        out_shape=[
            jax.ShapeDtypeStruct((n_, d_), F32),
            jax.ShapeDtypeStruct((d_, n_), MXU_DTYPE),
            jax.ShapeDtypeStruct((n_, wq.shape[1]), MXU_DTYPE),
        ],
        compiler_params=_params(("parallel",)),
        name="mix_out_proj",
    )(x2, onsa, gu, vn, sgw, sgb_t, gn, gs, wo, fg, wq)


def _sort_stack(stack):
    stack = list(stack)
    n = len(stack)
    k = 2
    while k <= n:
        j = k // 2
        while j > 0:
            for i in range(n):
                l = i ^ j
                if l > i:
                    hi, lo = jnp.maximum(stack[i], stack[l]), jnp.minimum(stack[i], stack[l])
                    stack[i], stack[l] = (hi, lo) if (i & k) == 0 else (lo, hi)
            j //= 2
        k *= 2
    return stack


def _insert_sorted(stack, x):
    out = []
    for s in stack:
        out.append(jnp.maximum(s, x))
        x = jnp.minimum(s, x)
    return out + [x]


def _pop_max(stack, depth, floor):
    m = jnp.max(stack[0], axis=0, keepdims=True)
    hit = stack[0] >= m
    popped = [jnp.where(hit, stack[d + 1], stack[d]) for d in range(depth - 1)]
    popped.append(jnp.where(hit, floor, stack[depth - 1]))
    return m, popped + list(stack[depth:])


def _pop_tops(stack, k):
    tops = []
    for t in range(k):
        m, stack = _pop_max(stack, min(len(stack), k - t), 0.0)
        tops.append(m)
    return jnp.concatenate(tops, axis=0)


def _top_rows(v, k):
    return _pop_tops(_sort_stack([v[8 * r:8 * (r + 1)] for r in range(v.shape[0] // 8)]), k)


def _top_pair_products(a, b, k):
    stack = [a[ra:ra + 1] * b[0:8] for ra in range(8)]
    stack = _insert_sorted(stack, a[0:1] * b[8:16])
    stack = _insert_sorted(stack, a[8:16] * b[0:1])
    return _pop_tops(stack, k)


def _peer_topk_kernel(qp_ref, sk_ref, e1_out, e2_out, eps_out):
    for hh in range(PEER_HEADS):
        base = 2 * PEER_HALF * hh
        s1 = _dot_nt(sk_ref[0], qp_ref[:, base:base + PEER_HALF])
        s2 = _dot_nt(sk_ref[1], qp_ref[:, base + PEER_HALF:base + 2 * PEER_HALF])
        e1 = jnp.exp(s1 - jnp.max(s1, axis=0, keepdims=True))
        e2 = jnp.exp(s2 - jnp.max(s2, axis=0, keepdims=True))
        a = _top_rows(e1, PEER_TOPK)
        b = _top_rows(e2, PEER_TOPK)
        z = jnp.sum(_top_pair_products(a, b, PEER_TOPK), axis=0, keepdims=True)
        zinv = 1.0 / z
        gates = _top_pair_products(a * zinv, b, PEER_TOPK)
        e1_out[hh] = e1 * zinv
        for j in range(e2.shape[1] // LANES):
            e2_out[hh, j] = e2[:, LANES * j:LANES * (j + 1)]
        eps_out[hh:hh + 1, :] = gates[PEER_TOPK - 1:PEER_TOPK]


def _peer_topk(qp, sub_keys, tm):
    n_, qd = qp.shape
    return pl.pallas_call(
        _peer_topk_kernel,
        grid=(n_ // tm,),
        in_specs=[
            pl.BlockSpec((tm, qd), lambda i: (i, 0)),
            pl.BlockSpec((2, PEER_NKEYS, PEER_HALF), lambda i: (0, 0, 0)),
        ],
        out_specs=[
            pl.BlockSpec((PEER_HEADS, PEER_NKEYS, tm), lambda i: (0, 0, i)),
            pl.BlockSpec((PEER_HEADS, tm // LANES, PEER_NKEYS, LANES), lambda i: (0, i, 0, 0)),
            pl.BlockSpec((PEER_HEADS, tm), lambda i: (0, i)),
        ],
        out_shape=[
            jax.ShapeDtypeStruct((PEER_HEADS, PEER_NKEYS, n_), F32),
            jax.ShapeDtypeStruct((PEER_HEADS, n_ // LANES, PEER_NKEYS, LANES), F32),
            jax.ShapeDtypeStruct((PEER_HEADS, n_), F32),
        ],
        compiler_params=_params(("parallel",)),
        name="peer_topk",
    )(qp, sub_keys)


def _peer_mix_kernel(n_tiles, ht_ref, u_ref, vt_ref, e1_ref, e2_ref, eps_ref, x1_ref, out_ref,
                     acc_ref, crow_ref, a0_ref, a1_ref, z0_ref, z1_ref):
    ie = pl.program_id(1)
    te = u_ref.shape[0]
    tm = ht_ref.shape[1]
    per_tile = te // PEER_NKEYS

    @pl.when(ie == 0)
    def _():
        acc_ref[...] = jnp.zeros_like(acc_ref)

    d_ = vt_ref.shape[1]
    n_lane = tm // LANES
    n_blocks = per_tile * n_lane
    a_units = [(r, c, kc) for r in range(te // MXU_ROWS) for c in range(tm // MXU_ROWS)
               for kc in range(d_ // A_KCHUNK)]
    c_units = [(r, c) for r in range(d_ // MXU_ROWS) for c in range(tm // MXU_ROWS)]

    def step(a_w, a_r, z_w, z_r):
        tile = ie - 1

        def a_unit(r, c, kc):
            rs = slice(MXU_ROWS * r, MXU_ROWS * (r + 1))
            cs = slice(MXU_ROWS * c, MXU_ROWS * (c + 1))
            ds = slice(A_KCHUNK * kc, A_KCHUNK * (kc + 1))
            res = jnp.dot(u_ref[rs, ds], ht_ref[ds, cs], preferred_element_type=F32)
            for jj in range(MXU_ROWS // LANES):
                j = c * (MXU_ROWS // LANES) + jj
                part = res[:, LANES * jj:LANES * (jj + 1)]
                if kc == 0:
                    a_w[j, rs, :] = part
                else:
                    a_w[j, rs, :] += part

        def c_unit(r, c):
            rs = slice(MXU_ROWS * r, MXU_ROWS * (r + 1))
            cs = slice(MXU_ROWS * c, MXU_ROWS * (c + 1))
            res = jnp.dot(vt_ref[0, rs, :], z_r[:, cs], preferred_element_type=F32)
            for jj in range(MXU_ROWS // LANES):
                acc_ref[c * (MXU_ROWS // LANES) + jj, rs, :] += res[:, LANES * jj:LANES * (jj + 1)]

        def b_block(k, j):
            ls = slice(LANES * j, LANES * (j + 1))
            for part in range(PEER_NKEYS // GATE_ROWS):
                ks = slice(GATE_ROWS * part, GATE_ROWS * (part + 1))
                rs = slice(PEER_NKEYS * k + GATE_ROWS * part, PEER_NKEYS * k + GATE_ROWS * (part + 1))
                g = jnp.zeros((GATE_ROWS, LANES), F32)
                for hh in range(PEER_HEADS):
                    row = PEER_HEADS * k + hh
                    gate = e2_ref[hh, j, ks, :] * crow_ref[row:row + 1, ls]
                    g = g + jnp.where(gate >= eps_ref[hh:hh + 1, ls], gate, 0.0)
                z_w[rs, ls] = (_gelu(a_r[j, rs, :]) * g).astype(z_w.dtype)

        do_a, do_b, do_c = a_w is not None, z_w is not None, z_r is not None
        if do_b:
            for k in range(per_tile):
                i1 = tile * per_tile + k
                for hh in range(PEER_HEADS):
                    row = PEER_HEADS * k + hh
                    crow_ref[row:row + 1, :] = e1_ref[hh, pl.ds(i1, 1), :]

        a_iter = iter(a_units if do_a else [])
        c_iter = iter(c_units if do_c else [])
        a_every = max(n_blocks // len(a_units), 1)
        c_per = -(-len(c_units) // n_blocks)
        for blk in range(n_blocks):
            k, j = divmod(blk, n_lane)
            if blk % a_every == 0:
                unit = next(a_iter, None)
                if unit is not None:
                    a_unit(*unit)
            for _ in range(c_per):
                unit = next(c_iter, None)
                if unit is not None:
                    c_unit(*unit)
            if do_b:
                b_block(k, j)
        for unit in a_iter:
            a_unit(*unit)
        for unit in c_iter:
            c_unit(*unit)

    a_bufs, z_bufs = (a0_ref, a1_ref), (z0_ref, z1_ref)

    def run(par, do_a=True, do_b=True, do_c=True):
        step(a_bufs[par] if do_a else None, a_bufs[1 - par], z_bufs[1 - par] if do_b else None,
             z_bufs[par] if do_c else None)

    last = n_tiles + 1
    pl.when(ie == 0)(lambda: run(0, do_b=False, do_c=False))
    pl.when(ie == 1)(lambda: run(1, do_c=False))
    pl.when((ie >= 2) & (ie < n_tiles) & (ie % 2 == 0))(lambda: run(0))
    pl.when((ie >= 2) & (ie < n_tiles) & (ie % 2 == 1))(lambda: run(1))
    pl.when(ie == n_tiles)(lambda: run(n_tiles % 2, do_a=False))
    pl.when(ie == last)(lambda: run(last % 2, do_a=False, do_b=False))

    @pl.when(ie == last)
    def _():
        for j in range(n_lane):
            ts = slice(LANES * j, LANES * (j + 1))
            out_ref[ts, :] = x1_ref[ts, :] + acc_ref[j].T


def _peer_mix(h2t, u, vt, e1t, e2t, eps, x1, tm, te):
    d_, n_ = h2t.shape
    n_tiles = u.shape[0] // te
    assert n_tiles >= 3
    return pl.pallas_call(
        functools.partial(_peer_mix_kernel, n_tiles),
        grid=(n_ // tm, n_tiles + 2),
        in_specs=[
            pl.BlockSpec((d_, tm), lambda it, ie: (0, it)),
            pl.BlockSpec((te, d_), lambda it, ie: (jnp.minimum(ie, n_tiles - 1), 0)),
            pl.BlockSpec((1, d_, te), lambda it, ie: (jnp.maximum(ie - 2, 0), 0, 0)),
            pl.BlockSpec((PEER_HEADS, PEER_NKEYS, tm), lambda it, ie: (0, 0, it)),
            pl.BlockSpec((PEER_HEADS, tm // LANES, PEER_NKEYS, LANES), lambda it, ie: (0, it, 0, 0)),
            pl.BlockSpec((PEER_HEADS, tm), lambda it, ie: (0, it)),
            pl.BlockSpec((tm, d_), lambda it, ie: (it, 0)),
        ],
        out_specs=pl.BlockSpec((tm, d_), lambda it, ie: (it, 0)),
        out_shape=jax.ShapeDtypeStruct((n_, d_), F32),
        scratch_shapes=[
            pltpu.VMEM((tm // LANES, d_, LANES), F32),
            pltpu.VMEM((PEER_HEADS * (te // PEER_NKEYS), tm), F32),
            pltpu.VMEM((tm // LANES, te, LANES), F32), pltpu.VMEM((tm // LANES, te, LANES), F32),
            pltpu.VMEM((te, tm), MXU_DTYPE), pltpu.VMEM((te, tm), MXU_DTYPE),
        ],
        compiler_params=_params(("parallel", "arbitrary")),
        name="peer_mix",
    )(h2t, u, vt, e1t, e2t, eps, x1)


def _rel_bucket(dist):
    n = jnp.maximum(dist, 0)
    max_exact = REL_BUCKETS // 2
    nf = jnp.maximum(n, 1).astype(F32)
    large = max_exact + (jnp.log(nf / max_exact) / math.log(REL_MAX_DIST / max_exact)
                         * (REL_BUCKETS - max_exact)).astype(jnp.int32)
    large = jnp.minimum(large, REL_BUCKETS - 1)
    return jnp.where(n < max_exact, n, large)


def _toeplitz(btab, off, rows, cols):
    per = rows + cols
    k = np.concatenate([np.arange(cols), np.zeros(1, np.int64), np.arange(-(rows - 1), 0)])
    idx = np.clip(off - k, 0, btab.shape[1] - 1)
    w = jnp.take(btab, jnp.asarray(idx, jnp.int32), axis=1)
    x = jnp.tile(w, (1, rows))[:, :rows * (per - 1)]
    return x.reshape(btab.shape[0], rows, per - 1)[:, :, :cols]


def _bias_tiles(rel_bias):
    assert SEL_TILE * NEAR_TILES - Q_BLOCK + 1 >= FAR_DIST
    assert CMP_STRIDE * (LANES + 1) - (CMP_BLOCK - 1) >= FAR_DIST
    dist = jnp.arange(BIAS_TABLE, dtype=jnp.int32)
    rel_bias = rel_bias.astype(F32) * LOG2E
    btab = rel_bias[_rel_bucket(dist)].T
    tbw = jnp.stack([_toeplitz(btab, SEL_TILE * (dd + 1) - Q_BLOCK, Q_BLOCK, SEL_TILE)
                     for dd in range(NEAR_TILES)], axis=1)
    wb = _toeplitz(btab, WINDOW, Q_BLOCK, WIN_KEYS)
    off = CMP_STRIDE * (LANES - 8) - (CMP_BLOCK - 1)
    front = CMP_STRIDE * LANES - off
    length = CMP_STRIDE * (LANES + 8 + 2)
    padded = jnp.pad(btab, ((0, 0), (front, length - front - BIAS_TABLE)))
    ch = padded.reshape(btab.shape[0], length // CMP_STRIDE, CMP_STRIDE)
    nwin = LANES + 1
    win = jnp.stack([ch[:, a:a + nwin] for a in range(Q_BLOCK // CMP_STRIDE)], axis=2)
    win = win.reshape(btab.shape[0], nwin, Q_BLOCK)
    cb = jnp.swapaxes(win[:, 1:nwin][:, ::-1], 1, 2)
    cfar = rel_bias.astype(F32)[REL_BUCKETS - 1]
    grp = lambda a: a.reshape((NSA_KV, NSA_REP) + a.shape[1:])

    r = np.arange(Q_BLOCK)[:, None]
    near_ok = np.stack([r - np.arange(SEL_TILE)[None, :] + SEL_TILE * (dd + 1) - Q_BLOCK >= 0
                        for dd in range(NEAR_TILES)])
    tbm = jnp.where(near_ok, tbw - cfar[:, None, None, None], NEG)
    dist_w = r - np.arange(WIN_KEYS)[None, :] + WINDOW
    wbm = jnp.where((dist_w >= 0) & (dist_w < WINDOW), wb, NEG)
    dist_c = r - CMP_STRIDE * np.arange(CMP_PAD)[None, :] + (CMP_STRIDE * CMP_PAD - Q_BLOCK - (CMP_BLOCK - 1))
    cbias = jnp.concatenate(
        [jnp.broadcast_to(cfar[:, None, None], (NSA_HEADS, Q_BLOCK, CMP_PAD - LANES)), cb], axis=2)
    cbm = jnp.where(dist_c >= 0, cbias, NEG)

    hi = cfar.astype(BF16).astype(F32)
    lo = (cfar - hi).astype(BF16).astype(F32)
    qmid = jnp.zeros((NSA_HEADS, 1, LANES), F32)
    qmid = qmid.at[:, 0, FAR_HI_LANE - HEAD_DIM].set(hi).at[:, 0, FAR_LO_LANE - HEAD_DIM].set(lo)
    qmid = qmid.at[:, 0, PAD_FLAG_LANE - HEAD_DIM].set(NEG)
    return grp(tbm), grp(wbm), grp(cbm), grp(qmid)


def _kv_layout_kernel(kv_ref, ts_ref, oh_ref, tw_ref, tv_ref, ksa_out, vsp_out, kwa_out, vwp_out):
    tm = kv_ref.shape[1]
    real = pl.program_id(1) >= KEY_PAD // tm
    blk = jnp.where(real, kv_ref[0], jnp.zeros_like(kv_ref[0]))
    for g in range(NSA_KV):
        for n, (out, tail) in enumerate(((ksa_out, ts_ref), (vsp_out, tv_ref),
                                         (kwa_out, tw_ref), (vwp_out, tv_ref))):
            lo = D_KV * n + HEAD_DIM * g
            out[0, g, :, 0:HEAD_DIM] = blk[:, lo:lo + HEAD_DIM]
            out[0, g, :, HEAD_DIM:LANES] = tail[...]
        ksa_out[0, g, :, LANES:2 * LANES] = oh_ref[...]


def _kv_layout(kv, tm):
    b_, t_, _ = kv.shape
    tp = t_ + KEY_PAD
    pos = np.arange(tp) - KEY_PAD
    real = pos >= 0
    tail = np.zeros((3, tp, LANES - HEAD_DIM), np.float32)
    tail[0, :, FAR_HI_LANE - HEAD_DIM] = real
    tail[0, :, FAR_LO_LANE - HEAD_DIM] = real
    tail[0, :, PAD_FLAG_LANE - HEAD_DIM] = ~real
    tail[1, :, PAD_FLAG_LANE - HEAD_DIM] = ~real
    tail[2, :, 0] = real
    onehot = ((pos[:, None] // SLC_BLOCK) == np.arange(LANES)[None, :]) & real[:, None]
    const = lambda a: jnp.asarray(a, kv.dtype)
    rows = lambda wd: pl.BlockSpec((tm, wd), lambda b, i: (i, 0))
    heads = lambda wd: pl.BlockSpec((1, NSA_KV, tm, wd), lambda b, i: (b, 0, i, 0))
    shape = lambda wd: jax.ShapeDtypeStruct((b_, NSA_KV, tp, wd), kv.dtype)
    return pl.pallas_call(
        _kv_layout_kernel,
        grid=(b_, tp // tm),
        in_specs=[
            pl.BlockSpec((1, tm, 4 * D_KV), lambda b, i: (b, jnp.maximum(i - KEY_PAD // tm, 0), 0)),
            rows(LANES - HEAD_DIM), rows(LANES), rows(LANES - HEAD_DIM), rows(LANES - HEAD_DIM),
        ],
        out_specs=[heads(2 * LANES), heads(LANES), heads(LANES), heads(LANES)],
        out_shape=[shape(2 * LANES), shape(LANES), shape(LANES), shape(LANES)],
        compiler_params=_params(("parallel", "parallel")),
        name="kv_layout",
    )(kv, const(tail[0]), const(onehot.astype(np.float32)), const(tail[1]), const(tail[2]))


def _overlap_matrix():
    c = np.arange(CMP_PAD)[:, None]
    s = np.arange(LANES)[None, :]
    per = SLC_BLOCK // CMP_STRIDE
    ov = (s == c // per) | ((c % per == per - 1) & (s == c // per + 1))
    return jnp.asarray(ov.astype(np.float32), BF16)


def kernel(x, attn_norm_g, w_in, q_norm_g, k_norm_g, cmp_pe_k, cmp_w1_k, cmp_w2_k, cmp_pe_v,
           cmp_w1_v, cmp_w2_v, rel_bias, sg_ln_g, sg_ln_b, sg_w, sg_b, out_norm_nsa, out_norm_sg,
           w_out, ffn_norm_g, peer_w_query, peer_sub_keys, peer_u, peer_v):
    b_, t_, d_ = x.shape
    n_ = b_ * t_
    assert t_ % Q_BLOCK == 0 and t_ >= WIN_KEYS and d_ == D_NSA + D_SG

    c_kv = D_NSA + 6 * D_KV
    c_gate = c_kv + 3 * NSA_HEADS
    wg = w_in[:, c_kv:c_gate].reshape(d_, NSA_KV, 3 * NSA_REP)
    wg = jnp.pad(wg, ((0, 0), (0, 0), (0, LANES - 3 * NSA_REP))).reshape(d_, NSA_KV * LANES)
    w_attn = jnp.concatenate([w_in[:, :c_kv], wg], axis=1).astype(MXU_DTYPE)
    w_sgu = w_in[:, c_gate:].astype(MXU_DTYPE)
    ng = attn_norm_g.reshape(1, d_)

    gq = jnp.tile(q_norm_g, NSA_HEADS).reshape(1, D_NSA) * (HEAD_DIM ** -0.5 * LOG2E)
    no_gain = jnp.zeros((D_KV,), F32)
    gkv = jnp.concatenate([jnp.tile(k_norm_g[1], NSA_KV), no_gain,
                           jnp.tile(k_norm_g[2], NSA_KV), no_gain]).reshape(1, 4 * D_KV)
    qn, kcv, kv, gate = _attn_proj(x, ng, w_attn, gq, gkv, tm=256)
    ksa, vsp, kwa, vwp = _kv_layout(kv, tm=512)
    gu, vn = _sgu_proj(x, ng, w_sgu, sg_ln_g, sg_ln_b, tm=256)

    nch = t_ // CMP_STRIDE
    xc = kcv.reshape(b_, nch, CMP_STRIDE, 2, NSA_KV, HEAD_DIM).transpose(0, 3, 4, 1, 2, 5)
    xc = xc.reshape(b_, 2, NSA_KV, nch, CMP_STRIDE * HEAD_DIM)
    w1 = jnp.stack([cmp_w1_k, cmp_w1_v]).astype(MXU_DTYPE)
    w2 = jnp.stack([cmp_w2_k, cmp_w2_v]).astype(MXU_DTYPE)
    pe = jnp.stack([cmp_pe_k, cmp_pe_v]).transpose(0, 2, 1, 3).reshape(2, NSA_KV, 1, CMP_BLOCK * HEAD_DIM)
    kcv_c = _compress(xc, w1, w2, pe.astype(MXU_DTYPE), k_norm_g)

    tbm, wbm, cbm, qmid = _bias_tiles(rel_bias)
    onsa = _nsa_attention(qn, gate, kcv_c, ksa, vsp, kwa, vwp, tbm, wbm, cbm, qmid, _overlap_matrix())

    x1, h2t, qp = _mix(x.reshape(n_, d_), onsa.reshape(n_, D_NSA), gu.reshape(n_, D_SG),
                      vn.reshape(n_, D_SG), sg_w, sg_b.T, out_norm_nsa.reshape(1, D_NSA),
                      out_norm_sg.reshape(1, D_SG), w_out.astype(MXU_DTYPE),
                      ffn_norm_g.reshape(1, d_), peer_w_query.astype(MXU_DTYPE), tm=256)

    e1t, e2t, eps = _peer_topk(qp, peer_sub_keys, tm=256)
    te = 512
    vt = peer_v.reshape(-1, te, d_).transpose(0, 2, 1).astype(MXU_DTYPE)
    out = _peer_mix(h2t, peer_u.astype(MXU_DTYPE), vt, e1t, e2t, eps, x1, tm=min(512, n_), te=te)
    return out.reshape(b_, t_, d_)
```
